```python
import math
import jax, jax.numpy as jnp
from jax import lax
import numpy as np

D_MODEL = 2048
BATCH = 4
SEQ = 8192
DEPTH = 1

MIX_WIDTH = D_MODEL
ATTN_WIDTH = MIX_WIDTH // 2
RWKV_WIDTH = MIX_WIDTH - ATTN_WIDTH

DA_HEAD_DIM = 64
DA_V_DIM = 2 * DA_HEAD_DIM
DA_HEADS = ATTN_WIDTH // DA_V_DIM
DA_QK_COLS = DA_HEADS * 2 * DA_HEAD_DIM
DA_V_COLS = DA_HEADS * DA_V_DIM
DA_COLS = 2 * DA_QK_COLS + DA_V_COLS
Q_BLOCK = 128

RWKV_HEAD = 64
RWKV_HEADS = RWKV_WIDTH // RWKV_HEAD
W_LORA = max(32, int(round(1.8 * RWKV_WIDTH ** 0.5 / 32)) * 32)
A_LORA = max(32, int(round(1.8 * RWKV_WIDTH ** 0.5 / 32)) * 32)
G_LORA = max(32, int(round(0.6 * RWKV_WIDTH ** 0.8 / 32)) * 32)
RWKV_COLS = 3 * RWKV_WIDTH + W_LORA + A_LORA + G_LORA
RWKV_SPLITS = [RWKV_WIDTH, 2 * RWKV_WIDTH, 3 * RWKV_WIDTH,
               3 * RWKV_WIDTH + W_LORA, 3 * RWKV_WIDTH + W_LORA + A_LORA]
LNX_EPS = 64e-5

IN_COLS = DA_COLS + RWKV_COLS

PEER_HEADS = 8
PEER_KEYS = 128
N_EXPERTS = PEER_KEYS * PEER_KEYS
PEER_TOPK = 16
PEER_QDIM = 256
PEER_HALF = PEER_QDIM // 2
PEER_CHUNK = 128

RMS_EPS = 1e-6

kernel_name = "hymba_diffattn_rwkv7_peer_block"


def rms_norm(x, w, eps=RMS_EPS):
    xf = x.astype(jnp.float32)
    y = xf * lax.rsqrt(jnp.mean(xf * xf, axis=-1, keepdims=True) + eps)
    return (y * w.astype(jnp.float32)).astype(x.dtype)


def head_group_norm(x, w, b, eps):
    B, S, H, N = x.shape
    xf = x.astype(jnp.float32)
    mu = jnp.mean(xf, axis=-1, keepdims=True)
    var = jnp.mean(jnp.square(xf - mu), axis=-1, keepdims=True)
    y = ((xf - mu) * lax.rsqrt(var + eps)).reshape(B, S, H * N)
    return y * w.astype(jnp.float32) + b.astype(jnp.float32)


def diff_attention_group(cols, lam_q1, lam_k1, lam_q2, lam_k2, subln_w, layer_idx):
    B, S, _ = cols.shape
    q, k, v = jnp.split(cols, [DA_QK_COLS, 2 * DA_QK_COLS], axis=-1)
    q = q.reshape(B, S, DA_HEADS, 2, DA_HEAD_DIM) * (DA_HEAD_DIM ** -0.5)
    k = k.reshape(B, S, DA_HEADS, 2, DA_HEAD_DIM)
    v = v.reshape(B, S, DA_HEADS, DA_V_DIM)
    lam_init = 0.8 - 0.6 * math.exp(-0.3 * layer_idx)
    f32 = jnp.float32
    lam = (jnp.exp(jnp.sum(lam_q1.astype(f32) * lam_k1.astype(f32)))
           - jnp.exp(jnp.sum(lam_q2.astype(f32) * lam_k2.astype(f32))) + lam_init)
    slopes = jnp.exp2(-(8.0 / DA_HEADS) * jnp.arange(1, DA_HEADS + 1, dtype=f32))
    kpos = jnp.arange(S)
    nblk = S // Q_BLOCK
    qb = q.reshape(B, nblk, Q_BLOCK, DA_HEADS, 2, DA_HEAD_DIM).swapaxes(0, 1)

    def one_block(args):
        q_blk, start = args
        qpos = start + jnp.arange(Q_BLOCK)
        dist = qpos[:, None] - kpos[None, :]
        s = jnp.einsum('bqhcd,bkhcd->bhcqk', q_blk, k).astype(f32)
        s = s - slopes[:, None, None, None] * dist.astype(f32)
        s = jnp.where(dist >= 0, s, -jnp.inf)
        p = jax.nn.softmax(s, axis=-1)
        a = p[:, :, 0] - lam * p[:, :, 1]
        return jnp.einsum('bhqk,bkhd->bqhd', a.astype(v.dtype), v)

    o = lax.map(one_block, (qb, jnp.arange(nblk) * Q_BLOCK))
    o = o.swapaxes(0, 1).reshape(B, S, DA_HEADS, DA_V_DIM)
    o = rms_norm(o, subln_w) * (1.0 - lam_init)
    return o.reshape(B, S, ATTN_WIDTH)


def wkv7_scan(r, w, k, v, a_vec, b_vec):
    B, S, H, N = r.shape

    def step(state, inp):
        r_t, w_t, k_t, v_t, a_t, b_t = inp
        sa = jnp.einsum('bhvk,bhk->bhv', state, a_t)
        state = (state * w_t[:, :, None, :] + sa[..., None] * b_t[:, :, None, :]
                 + v_t[..., None] * k_t[:, :, None, :])
        out = jnp.einsum('bhvk,bhk->bhv', state, r_t)
        return state, out

    xs = (r.swapaxes(0, 1), w.swapaxes(0, 1), k.swapaxes(0, 1), v.swapaxes(0, 1),
          a_vec.swapaxes(0, 1), b_vec.swapaxes(0, 1))
    init = jnp.zeros((B, H, N, N), jnp.float32)
    _, out = lax.scan(step, init, xs)
    return out.swapaxes(0, 1)


def rwkv7_group(cols, mu, w0, w2, a0, a2, g2, k_k, k_a, r_k, lnx_w, lnx_b):
    B, S, _ = cols.shape
    H, N = RWKV_HEADS, RWKV_HEAD
    f32 = jnp.float32
    prev = jnp.pad(cols, ((0, 0), (1, 0), (0, 0)))[:, :S]
    cols = cols + (prev - cols) * mu
    r, k, v, w_low, a_low, g_low = jnp.split(cols, RWKV_SPLITS, axis=-1)
    w = -jax.nn.softplus(-(w0 + jnp.tanh(w_low) @ w2)) - 0.5
    decay = jnp.exp(-jnp.exp(w.astype(f32)))
    a = jax.nn.sigmoid((a0 + a_low @ a2).astype(f32))
    g = (jax.nn.sigmoid(g_low) @ g2).astype(f32)
    heads = lambda t: t.reshape(B, S, H, N).astype(f32)
    kk = heads(k * k_k)
    kk = kk / jnp.maximum(jnp.sqrt(jnp.sum(kk * kk, axis=-1, keepdims=True)), 1e-12)
    kf = k.astype(f32) * (1.0 + (a - 1.0) * k_a.astype(f32))
    rh, kh, vh, ah = heads(r), heads(kf), heads(v), heads(a)
    wkv = wkv7_scan(rh, heads(decay), kh, vh, -kk, kk * ah)
    o = head_group_norm(wkv, lnx_w, lnx_b, LNX_EPS)
    bonus = jnp.sum(rh * kh * r_k.astype(f32), axis=-1, keepdims=True) * vh
    o = (o + bonus.reshape(B, S, H * N)) * g
    return o.astype(cols.dtype)


def peer_ffn(h, w_query, sub_keys, expert_down, expert_up):
    B, S, D = h.shape
    T = B * S
    ht = h.reshape(T, D)
    q = (ht @ w_query).reshape(T, PEER_HEADS, 2, PEER_HALF)
    s = jnp.einsum('thcd,hcnd->thcn', q, sub_keys).astype(jnp.float32)
    v1, i1 = lax.top_k(s[:, :, 0], PEER_TOPK)
    v2, i2 = lax.top_k(s[:, :, 1], PEER_TOPK)
    cand = (v1[..., :, None] + v2[..., None, :]).reshape(T, PEER_HEADS, PEER_TOPK * PEER_TOPK)
    cand_idx = (i1[..., :, None] * PEER_KEYS + i2[..., None, :]).reshape(T, PEER_HEADS, PEER_TOPK * PEER_TOPK)
    best, pos = lax.top_k(cand, PEER_TOPK)
    idx = jnp.take_along_axis(cand_idx, pos, axis=-1)
    gate = jax.nn.softmax(best, axis=-1).astype(h.dtype)
    nchunk = T // PEER_CHUNK

    def one_chunk(args):
        hc, ic, gc = args
        u = expert_down[ic]
        vv = expert_up[ic]
        z = jnp.einsum('cd,chkd->chk', hc, u)
        act = jax.nn.gelu(z, approximate=False) * gc
        return jnp.einsum('chk,chkd->cd', act, vv)

    out = lax.map(one_chunk, (ht.reshape(nchunk, PEER_CHUNK, D),
                              idx.reshape(nchunk, PEER_CHUNK, PEER_HEADS, PEER_TOPK),
                              gate.reshape(nchunk, PEER_CHUNK, PEER_HEADS, PEER_TOPK)))
    return out.reshape(B, S, D)


def setup_inputs(seed: int = 0) -> dict:
    key = jax.random.key(seed)
    ks = jax.random.split(key, 32)
    L, D, C = DEPTH, D_MODEL, RWKV_WIDTH
    nrm = lambda i, shape, scale: jax.random.normal(ks[i], shape, jnp.float32) * scale
    return {
        'x': nrm(0, (BATCH, SEQ, D), 1.0),
        'norm1_w': 1.0 + nrm(1, (L, D), 0.02),
        'w_in': nrm(2, (L, D, IN_COLS), D ** -0.5),
        'tshift_mu': jax.random.uniform(ks[3], (L, RWKV_COLS), jnp.float32),
        'w0': jax.random.uniform(ks[4], (L, C), jnp.float32, -6.5, -1.5),
        'w2': nrm(5, (L, W_LORA, C), 0.5 * W_LORA ** -0.5),
        'a0': nrm(6, (L, C), 0.1),
        'a2': nrm(7, (L, A_LORA, C), 0.5 * A_LORA ** -0.5),
        'g2': nrm(8, (L, G_LORA, C), G_LORA ** -0.5),
        'k_k': 0.85 + nrm(9, (L, C), 0.02),
        'k_a': 1.0 + nrm(10, (L, C), 0.02),
        'r_k': nrm(11, (L, RWKV_HEADS, RWKV_HEAD), 0.1),
        'lnx_w': 1.0 + nrm(12, (L, C), 0.02),
        'lnx_b': nrm(13, (L, C), 0.01),
        'lambda_q1': nrm(14, (L, DA_HEAD_DIM), 0.1),
        'lambda_k1': nrm(15, (L, DA_HEAD_DIM), 0.1),
        'lambda_q2': nrm(16, (L, DA_HEAD_DIM), 0.1),
        'lambda_k2': nrm(17, (L, DA_HEAD_DIM), 0.1),
        'subln_w': 1.0 + nrm(18, (L, DA_V_DIM), 0.02),
        'w_out': nrm(19, (L, MIX_WIDTH, D), MIX_WIDTH ** -0.5),
        'norm2_w': 1.0 + nrm(20, (L, D), 0.02),
        'peer_w_query': nrm(21, (L, D, PEER_HEADS * PEER_QDIM), D ** -0.5),
        'peer_sub_keys': nrm(22, (L, PEER_HEADS, 2, PEER_KEYS, PEER_HALF), PEER_HALF ** -0.5),
        'peer_down': nrm(23, (L, N_EXPERTS, D), D ** -0.5),
        'peer_up': nrm(24, (L, N_EXPERTS, D), 0.5),
        'norm_f_w': 1.0 + nrm(25, (D,), 0.02),
    }


def reference(x, norm1_w, w_in, tshift_mu, w0, w2, a0, a2, g2, k_k, k_a, r_k, lnx_w, lnx_b,
              lambda_q1, lambda_k1, lambda_q2, lambda_k2, subln_w, w_out, norm2_w,
              peer_w_query, peer_sub_keys, peer_down, peer_up, norm_f_w):
    for l in range(DEPTH):
        h = rms_norm(x, norm1_w[l])
        proj = h @ w_in[l]
        attn_cols, rwkv_cols = proj[..., :DA_COLS], proj[..., DA_COLS:]
        o_attn = diff_attention_group(attn_cols, lambda_q1[l], lambda_k1[l], lambda_q2[l],
                                      lambda_k2[l], subln_w[l], l)
        o_rwkv = rwkv7_group(rwkv_cols, tshift_mu[l], w0[l], w2[l], a0[l], a2[l], g2[l],
                             k_k[l], k_a[l], r_k[l], lnx_w[l], lnx_b[l])
        mixed = jnp.concatenate([o_attn, o_rwkv.astype(o_attn.dtype)], axis=-1)
        x = x + mixed @ w_out[l]
        h2 = rms_norm(x, norm2_w[l])
        x = x + peer_ffn(h2, peer_w_query[l], peer_sub_keys[l], peer_down[l], peer_up[l])
    return rms_norm(x, norm_f_w)
```

```python
import functools
import math

import jax
import jax.numpy as jnp
from jax import lax
from jax.experimental import pallas as pl
from jax.experimental.pallas import tpu as pltpu

F32 = jnp.float32
BF16 = jnp.bfloat16
HI = lax.Precision.HIGHEST

LANES = 128
RMS_EPS = 1e-6
LNX_EPS = 64e-5
PEER_TOPK = 16
RWKV_CHUNK = 64
VMEM_LIMIT = 56 * 1024 * 1024
NEG = -1e30

_NT = (((1,), (1,)), ((), ()))


def _cparams(*sem):
    return pltpu.CompilerParams(dimension_semantics=sem, vmem_limit_bytes=VMEM_LIMIT)


def _rms(x, w):
    ms = jnp.mean(x * x, axis=-1, keepdims=True)
    return x * lax.rsqrt(ms + RMS_EPS) * w


def _proj_kernel(x_ref, nw_ref, w_ref, o_ref, h_ref):
    @pl.when(pl.program_id(1) == 0)
    def _():
        h_ref[...] = _rms(x_ref[...], nw_ref[...]).astype(BF16)

    o_ref[...] = jnp.dot(h_ref[...], w_ref[...], preferred_element_type=F32).astype(o_ref.dtype)


def _proj_shift_kernel(x_ref, nw_ref, w_ref, mu_ref, o_ref, h_ref, carry_ref, *, tiles_per_seq):
    i = pl.program_id(0)
    j = pl.program_id(1)

    @pl.when(j == 0)
    def _():
        h_ref[...] = _rms(x_ref[...], nw_ref[...]).astype(BF16)

    @pl.when(i % tiles_per_seq == 0)
    def _():
        carry_ref[j] = jnp.zeros(carry_ref.shape[1:], F32)

    p = jnp.dot(h_ref[...], w_ref[...], preferred_element_type=F32)
    tm = p.shape[0]
    rolled = pltpu.roll(p, 1, 0)
    row = lax.broadcasted_iota(jnp.int32, p.shape, 0)
    prev = jnp.where(row == 0, carry_ref[j][7:8, :], rolled)
    o_ref[...] = (p + (prev - p) * mu_ref[...]).astype(o_ref.dtype)
    carry_ref[j] = p[tm - 8:tm, :]


def _in_proj(x2, nw, w, mu, *, seq, tm, tn):
    T, D = x2.shape
    N = w.shape[1]
    grid = (T // tm, N // tn)
    in_specs = [
        pl.BlockSpec((tm, D), lambda i, j: (i, 0)),
        pl.BlockSpec((1, D), lambda i, j: (0, 0)),
        pl.BlockSpec((D, tn), lambda i, j: (0, j)),
    ]
    args = [x2, nw, w]
    scratch = [pltpu.VMEM((tm, D), BF16)]
    if mu is None:
        body = _proj_kernel
    else:
        body = functools.partial(_proj_shift_kernel, tiles_per_seq=seq // tm)
        in_specs.append(pl.BlockSpec((1, tn), lambda i, j: (0, j)))
        args.append(mu)
        scratch.append(pltpu.VMEM((N // tn, 8, tn), F32))
    return pl.pallas_call(
        body,
        grid=grid,
        in_specs=in_specs,
        out_specs=pl.BlockSpec((tm, tn), lambda i, j: (i, j)),
        out_shape=jax.ShapeDtypeStruct((T, N), BF16),
        scratch_shapes=scratch,
        compiler_params=_cparams("arbitrary", "arbitrary"),
        name="in_proj" if mu is None else "in_proj_shift",
    )(*args)


def _attn_kernel(q_ref, k_ref, v_ref, lq1_ref, lk1_ref, lq2_ref, lk2_ref, sw_ref, o_ref,
                 *, tq, n_heads, lam_init):
    h = pl.program_id(1)
    qi = pl.program_id(2)
    tk = tq
    dh = LANES // 2
    lane = lax.broadcasted_iota(jnp.int32, (1, LANES), 1)
    q = q_ref[...] * jnp.asarray(dh ** -0.5, BF16)
    zero = jnp.zeros_like(q)
    q1 = jnp.where(lane < dh, q, zero)
    q2 = jnp.where(lane >= dh, q, zero)

    hv = jnp.full((1, tk), h + 1, jnp.int32).astype(F32)
    slope = jnp.exp2(hv * (-8.0 / n_heads))
    col = lax.broadcasted_iota(jnp.int32, (1, tk), 1)
    row = lax.broadcasted_iota(jnp.int32, (tq, 1), 0)

    def step(kc, carry, masked):
        ks = pl.multiple_of(kc * tk, tk)
        kb = k_ref[pl.ds(ks, tk), :]
        vb = v_ref[pl.ds(ks, tk), :]
        rel = (ks - qi * tq) + col
        bias = slope * rel.astype(F32)
        s_both = (lax.dot_general(q1, kb, _NT, preferred_element_type=F32) + bias,
                  lax.dot_general(q2, kb, _NT, preferred_element_type=F32) + bias)
        out = []
        for c in range(2):
            m, l, a = carry[3 * c:3 * c + 3]
            s = s_both[c]
            if masked:
                s = jnp.where(rel <= row, s, NEG)
            mn = jnp.maximum(m, jnp.max(s, axis=-1, keepdims=True))
            p = jnp.exp(s - mn)
            al = jnp.exp(m - mn)
            l = al * l + jnp.sum(p, axis=-1, keepdims=True)
            a = al * a + jnp.dot(p.astype(BF16), vb, preferred_element_type=F32)
            out += [mn, l, a]
        return tuple(out)

    init = (jnp.full((tq, 1), NEG, F32), jnp.zeros((tq, 1), F32), jnp.zeros((tq, LANES), F32)) * 2
    carry = lax.fori_loop(0, qi, lambda kc, c: step(kc, c, False), init)
    m1, l1, a1, m2, l2, a2 = step(qi, carry, True)

    lam = (jnp.exp(jnp.sum(lq1_ref[...] * lk1_ref[...], axis=-1, keepdims=True))
           - jnp.exp(jnp.sum(lq2_ref[...] * lk2_ref[...], axis=-1, keepdims=True)) + lam_init)
    o = a1 / l1 - lam * (a2 / l2)
    o_ref[...] = (_rms(o, sw_ref[...]) * (1.0 - lam_init)).astype(o_ref.dtype)


def _diff_attention(qkv, lq1, lk1, lq2, lk2, subln_w, *, batch, seq, n_heads, tq, lam_init):
    T = qkv.shape[0]
    nq = seq // tq
    vec = lambda n: pl.BlockSpec((1, n), lambda b, h, i: (0, 0))
    return pl.pallas_call(
        functools.partial(_attn_kernel, tq=tq, n_heads=n_heads, lam_init=lam_init),
        grid=(batch, n_heads, nq),
        in_specs=[
            pl.BlockSpec((tq, LANES), lambda b, h, i: (b * nq + i, h)),
            pl.BlockSpec((seq, LANES), lambda b, h, i: (b, n_heads + h)),
            pl.BlockSpec((seq, LANES), lambda b, h, i: (b, 2 * n_heads + h)),
            vec(lq1.shape[1]), vec(lq1.shape[1]), vec(lq1.shape[1]), vec(lq1.shape[1]),
            vec(LANES),
        ],
        out_specs=pl.BlockSpec((tq, LANES), lambda b, h, i: (b * nq + i, h)),
        out_shape=jax.ShapeDtypeStruct((T, n_heads * LANES), BF16),
        compiler_params=_cparams("arbitrary", "arbitrary", "arbitrary"),
        name="diff_attn",
    )(qkv, qkv, qkv, lq1, lk1, lq2, lk2, subln_w)


def _hdot(a, b):
    return jnp.dot(a, b, preferred_element_type=F32, precision=HI)


def _rwkv_kernel(r_ref, k_ref, v_ref, lo_ref, w0_ref, a0_ref, kk_ref, ka_ref, rk_ref, lw_ref, lb_ref,
                 w2_ref, a2_ref, g2_ref, o_ref, st_ref, *, rows):
    L = RWKV_CHUNK
    N = LANES // 2

    @pl.when(pl.program_id(2) == 0)
    def _():
        st_ref[...] = jnp.zeros(st_ref.shape, F32)

    r = r_ref[...].astype(F32)
    k = k_ref[...].astype(F32)
    v = v_ref[...].astype(F32)
    lo = lo_ref[...].astype(F32)
    lo_wa = lo[:, :LANES]
    wl = _hdot(jnp.tanh(lo_wa), w2_ref[...])
    al = _hdot(lo_wa, a2_ref[...])
    g = _hdot(jax.nn.sigmoid(lo[:, LANES:]), g2_ref[...])

    wx = -(w0_ref[...] + wl)
    softplus = jnp.maximum(wx, 0.0) + jnp.log1p(jnp.exp(-jnp.abs(wx)))
    logdecay = -jnp.exp(-softplus - 0.5)
    a = jax.nn.sigmoid(a0_ref[...] + al)

    ri = lax.broadcasted_iota(jnp.int32, (LANES, LANES), 0)
    ci = lax.broadcasted_iota(jnp.int32, (LANES, LANES), 1)
    same_head = (ri // N) == (ci // N)
    ones_bd = same_head.astype(F32)
    eye = (ri == ci).astype(F32)
    strict = ((ri % L) > (ci % L)).astype(F32)
    lower = ((ri % L) >= (ci % L)).astype(F32)
    tri = (lax.broadcasted_iota(jnp.int32, (L, L), 0) >= lax.broadcasted_iota(jnp.int32, (L, L), 1)).astype(F32)
    head0 = lax.broadcasted_iota(jnp.int32, (1, LANES), 1) < N

    kkr = k * kk_ref[...]
    norm = jnp.sqrt(_hdot(kkr * kkr, ones_bd))
    kk = kkr / jnp.maximum(norm, 1e-12)
    kf = k * (1.0 + (a - 1.0) * ka_ref[...])
    avec = -kk
    bvec = kk * a

    def bd(x):
        return jnp.concatenate([jnp.where(head0, x, 0.0), jnp.where(head0, 0.0, x)], axis=0)

    outs = []
    st = st_ref[...]
    for c in range(rows // L):
        sl = slice(c * L, (c + 1) * L)
        ld = logdecay[sl]
        cum = _hdot(tri, ld)
        tot = cum[L - 1:L, :]
        at = bd(avec[sl] * jnp.exp(cum - ld))
        rt = bd(r[sl] * jnp.exp(cum))
        inv = jnp.exp(-cum)
        bt = bd(bvec[sl] * inv)
        kt = bd(kf[sl] * inv)
        rest = jnp.exp(tot - cum)
        bl_t = bd(bvec[sl] * rest).T
        kl_t = bd(kf[sl] * rest).T
        vd = bd(v[sl])

        amat = lax.dot_general(jnp.concatenate([at, rt], axis=0), jnp.concatenate([bt, kt], axis=0), _NT,
                               preferred_element_type=F32, precision=HI)
        a_ab = amat[:LANES, :LANES] * strict
        a_ak = amat[:LANES, LANES:] * strict
        a_rb = amat[LANES:, :LANES] * lower
        a_rk = amat[LANES:, LANES:] * lower

        tinv = eye + a_ab
        pw = a_ab
        for _ in range(int(math.log2(L)) - 1):
            pw = _hdot(pw, pw)
            tinv = tinv + _hdot(tinv, pw)

        ahat = _hdot(tinv, at)
        uhat = _hdot(tinv, _hdot(a_ak, vd))
        rhat = rt + _hdot(a_rb, ahat)
        ohat = _hdot(a_rb, uhat) + _hdot(a_rk, vd)
        mmat = eye * jnp.exp(tot) + _hdot(bl_t, ahat)
        cmat = _hdot(bl_t, uhat) + _hdot(kl_t, vd)

        od = _hdot(rhat, st) + ohat
        st = _hdot(mmat, st) + cmat
        outs.append(od[:L] + od[L:])
    st_ref[...] = st
    wkv = jnp.concatenate(outs, axis=0)

    inv_n = 1.0 / N
    mu = _hdot(wkv, ones_bd) * inv_n
    xc = wkv - mu
    var = _hdot(xc * xc, ones_bd) * inv_n
    o = xc * lax.rsqrt(var + LNX_EPS) * lw_ref[...] + lb_ref[...]
    bonus = _hdot(r * kf * rk_ref[...], ones_bd) * v
    o_ref[...] = ((o + bonus) * g).astype(o_ref.dtype)


def _rwkv(cols, w0, a0, k_k, k_a, r_k, lnx_w, lnx_b, w2p, a2p, g2p, *, batch, seq, n_pairs, rows):
    T = cols.shape[0]
    nt = seq // rows
    lora_blk = (cols.shape[1] - 3 * n_pairs * LANES) // LANES
    row_map = lambda off: (lambda b, p, t: (b * nt + t, off + p))
    vec = pl.BlockSpec((1, LANES), lambda b, p, t: (0, p))
    return pl.pallas_call(
        functools.partial(_rwkv_kernel, rows=rows),
        grid=(batch, n_pairs, nt),
        in_specs=[
            pl.BlockSpec((rows, LANES), row_map(0)),
            pl.BlockSpec((rows, LANES), row_map(n_pairs)),
            pl.BlockSpec((rows, LANES), row_map(2 * n_pairs)),
            pl.BlockSpec((rows, lora_blk * LANES), lambda b, p, t: (b * nt + t, 3 * n_pairs // lora_blk)),
            vec, vec, vec, vec, vec, vec, vec,
            pl.BlockSpec((LANES, LANES), lambda b, p, t: (0, p)),
            pl.BlockSpec((LANES, LANES), lambda b, p, t: (0, p)),
            pl.BlockSpec((2 * LANES, LANES), lambda b, p, t: (0, p)),
        ],
        out_specs=pl.BlockSpec((rows, LANES), row_map(0)),
        out_shape=jax.ShapeDtypeStruct((T, n_pairs * LANES), BF16),
        scratch_shapes=[pltpu.VMEM((LANES, LANES), F32)],
        compiler_params=_cparams("arbitrary", "arbitrary", "arbitrary"),
        name="rwkv7",
    )(cols, cols, cols, cols, w0, a0, k_k, k_a, r_k, lnx_w, lnx_b, w2p, a2p, g2p)


def _out_proj_kernel(x_ref, oa_ref, or_ref, wa_ref, wr_ref, nw_ref, x1_ref, h2_ref):
    x1 = (x_ref[...] + jnp.dot(oa_ref[...], wa_ref[...], preferred_element_type=F32)
          + jnp.dot(or_ref[...], wr_ref[...], preferred_element_type=F32))
    x1_ref[...] = x1
    h2_ref[...] = _rms(x1, nw_ref[...]).astype(BF16)


def _out_proj(x2, oa, orw, wa, wr, nw, *, tm):
    T, D = x2.shape
    W = oa.shape[1]
    return pl.pallas_call(
        _out_proj_kernel,
        grid=(T // tm,),
        in_specs=[
            pl.BlockSpec((tm, D), lambda i: (i, 0)),
            pl.BlockSpec((tm, W), lambda i: (i, 0)),
            pl.BlockSpec((tm, W), lambda i: (i, 0)),
            pl.BlockSpec((W, D), lambda i: (0, 0)),
            pl.BlockSpec((W, D), lambda i: (0, 0)),
            pl.BlockSpec((1, D), lambda i: (0, 0)),
        ],
        out_specs=[pl.BlockSpec((tm, D), lambda i: (i, 0)), pl.BlockSpec((tm, D), lambda i: (i, 0))],
        out_shape=[jax.ShapeDtypeStruct((T, D), F32), jax.ShapeDtypeStruct((T, D), BF16)],
        compiler_params=_cparams("arbitrary"),
        name="out_proj",
    )(x2, oa, orw, wa, wr, nw)


def _topk_rows(cur, k):
    rows = []
    for _ in range(k):
        m = jnp.max(cur, axis=0, keepdims=True)
        rows.append(m)
        cur = jnp.where(cur == m, -jnp.inf, cur)
    return rows


_CANDS = [(a, b) for a in range(PEER_TOPK) for b in range(PEER_TOPK) if (a + 1) * (b + 1) <= PEER_TOPK]
_NCAND = -(-len(_CANDS) // 8) * 8


def _route_kernel(h2_ref, wq_ref, keys_ref, s1_ref, e1_ref, s2_ref, e2_ref, tau_ref, sc_ref, cand_ref,
                  *, n_heads):
    q = jnp.dot(h2_ref[...], wq_ref[...], preferred_element_type=F32).astype(BF16)
    for g in range(2 * n_heads):
        sc_ref[g] = lax.dot_general(keys_ref[g], q[:, g * LANES:(g + 1) * LANES], _NT,
                                    preferred_element_type=F32)
    cand_ref[...] = jnp.full(cand_ref.shape, -jnp.inf, F32)

    def per_head(h, _):
        s1 = sc_ref[2 * h]
        s2 = sc_ref[2 * h + 1]
        v1 = _topk_rows(s1, PEER_TOPK)
        v2 = _topk_rows(s2, PEER_TOPK)
        for n, (ia, ib) in enumerate(_CANDS):
            cand_ref[n:n + 1, :] = v1[ia] + v2[ib]
        cand = cand_ref[...]
        tau = _topk_rows(cand, PEER_TOPK)[-1]
        z = jnp.sum(jnp.where(cand >= tau, jnp.exp(cand - (v1[0] + v2[0])), 0.0), axis=0, keepdims=True)
        s1_ref[h] = s1
        e1_ref[h] = jnp.exp(s1 - v1[0])
        s2_ref[h] = s2
        e2_ref[h] = jnp.exp(s2 - v2[0]) / z
        tau_ref[pl.ds(h, 1), :] = tau
        return 0

    lax.fori_loop(0, n_heads, per_head, 0)


def _route(h2, wq, keys, *, tt, n_heads):
    T, D = h2.shape
    nk = keys.shape[1]
    big = jax.ShapeDtypeStruct((n_heads, nk, T), F32)
    blk = pl.BlockSpec((n_heads, nk, tt), lambda i: (0, 0, i))
    return pl.pallas_call(
        functools.partial(_route_kernel, n_heads=n_heads),
        grid=(T // tt,),
        in_specs=[
            pl.BlockSpec((tt, D), lambda i: (i, 0)),
            pl.BlockSpec(wq.shape, lambda i: (0, 0)),
            pl.BlockSpec(keys.shape, lambda i: (0, 0, 0)),
        ],
        out_specs=[blk, blk, blk, blk, pl.BlockSpec((n_heads, tt), lambda i: (0, i))],
        out_shape=[big, big, big, big, jax.ShapeDtypeStruct((n_heads, T), F32)],
        scratch_shapes=[pltpu.VMEM((2 * n_heads, nk, tt), F32), pltpu.VMEM((_NCAND, tt), F32)],
        compiler_params=_cparams("arbitrary"),
        name="peer_route",
    )(h2, wq, keys)


def _peer_kernel(h2_ref, down_ref, upt_ref, s1_ref, e1_ref, s2_ref, e2_ref, tau_ref, o_ref, act_ref,
                 *, n_heads, nk):
    @pl.when(pl.program_id(1) == 0)
    def _():
        o_ref[...] = jnp.zeros(o_ref.shape, F32)

    z = lax.dot_general(down_ref[...], h2_ref[...], _NT, preferred_element_type=F32)
    for ii in range(down_ref.shape[0] // nk):
        gate = None
        for h in range(n_heads):
            total = s2_ref[h] + s1_ref[h, ii:ii + 1, :]
            val = jnp.where(total >= tau_ref[h:h + 1, :], e2_ref[h] * e1_ref[h, ii:ii + 1, :], 0.0)
            gate = val if gate is None else gate + val
        zz = z[ii * nk:(ii + 1) * nk]
        gelu = 0.5 * zz * (1.0 + lax.erf(zz * math.sqrt(0.5)))
        act_ref[ii * nk:(ii + 1) * nk, :] = (gelu * gate).astype(BF16)
    o_ref[...] += jnp.dot(upt_ref[...], act_ref[...], preferred_element_type=F32)


def _peer(h2, down, upt, s1, e1, s2, e2, tau, *, tt, et, n_heads, nk):
    T, D = h2.shape
    E = down.shape[0]
    i1_blk = et // nk
    row_blk = pl.BlockSpec((n_heads, i1_blk, tt), lambda i, j: (0, j, i))
    full_blk = pl.BlockSpec((n_heads, nk, tt), lambda i, j: (0, 0, i))
    return pl.pallas_call(
        functools.partial(_peer_kernel, n_heads=n_heads, nk=nk),
        grid=(T // tt, E // et),
        in_specs=[
            pl.BlockSpec((tt, D), lambda i, j: (i, 0)),
            pl.BlockSpec((et, D), lambda i, j: (j, 0)),
            pl.BlockSpec((D, et), lambda i, j: (0, j)),
            row_blk, row_blk, full_blk, full_blk,
            pl.BlockSpec((n_heads, tt), lambda i, j: (0, i)),
        ],
        out_specs=pl.BlockSpec((D, tt), lambda i, j: (0, i)),
        out_shape=jax.ShapeDtypeStruct((D, T), F32),
        scratch_shapes=[pltpu.VMEM((et, tt), BF16)],
        compiler_params=_cparams("arbitrary", "arbitrary"),
        name="peer_experts",
    )(h2, down, upt, s1, e1, s2, e2, tau)


def _final_kernel(x1_ref, pt_ref, nw_ref, o_ref):
    o_ref[...] = _rms(x1_ref[...] + pt_ref[...].T, nw_ref[...])


def _final(x1, peer_t, nw, *, tm):
    T, D = x1.shape
    return pl.pallas_call(
        _final_kernel,
        grid=(T // tm,),
        in_specs=[
            pl.BlockSpec((tm, D), lambda i: (i, 0)),
            pl.BlockSpec((D, tm), lambda i: (0, i)),
            pl.BlockSpec((1, D), lambda i: (0, 0)),
        ],
        out_specs=pl.BlockSpec((tm, D), lambda i: (i, 0)),
        out_shape=jax.ShapeDtypeStruct((T, D), F32),
        compiler_params=_cparams("arbitrary"),
        name="final_norm",
    )(x1, peer_t, nw)


def _tiles(seq, tokens):
    return dict(
        proj_tm=min(512, seq),
        attn_tq=min(512, seq),
        rwkv_rows=min(256, seq),
        out_tm=min(512, tokens),
        route_tt=min(256, tokens),
        peer_tt=min(512, tokens),
        peer_et=1024,
        final_tm=min(256, tokens),
    )


def _pad_rows(w, rows_before, rows_total):
    return jnp.pad(w, ((rows_before, rows_total - rows_before - w.shape[0]), (0, 0)))


def kernel(x, norm1_w, w_in, tshift_mu, w0, w2, a0, a2, g2, k_k, k_a, r_k, lnx_w, lnx_b, lambda_q1, lambda_k1,
           lambda_q2, lambda_k2, subln_w, w_out, norm2_w, peer_w_query, peer_sub_keys, peer_down, peer_up,
           norm_f_w):
    B, S, D = x.shape
    T = B * S
    depth = norm1_w.shape[0]
    dh = lambda_q1.shape[-1]
    n_rheads, rhead = r_k.shape[1], r_k.shape[2]
    rw = n_rheads * rhead
    n_pairs = rw // LANES
    w_lora, a_lora, g_lora = w2.shape[1], a2.shape[1], g2.shape[1]
    da_cols = w_in.shape[2] - (3 * rw + w_lora + a_lora + g_lora)
    n_aheads = da_cols // (3 * 2 * dh)
    p_heads, nk = peer_sub_keys.shape[1], peer_sub_keys.shape[3]
    assert 2 * dh == LANES and 2 * rhead == LANES and rhead == RWKV_CHUNK and nk == LANES
    assert w_lora + a_lora == LANES and g_lora <= 2 * LANES
    t = _tiles(S, T)

    xt = x.reshape(T, D)
    for l in range(depth):
        row = lambda p: p[l].reshape(1, -1)
        lam_init = 0.8 - 0.6 * math.exp(-0.3 * l)

        w_attn = w_in[l][:, :da_cols].astype(BF16)
        rcols = w_in[l].shape[1] - da_cols
        rpad = 3 * rw + 3 * LANES - rcols
        w_rwkv = jnp.pad(w_in[l][:, da_cols:], ((0, 0), (0, rpad))).astype(BF16)
        mu = jnp.pad(row(tshift_mu), ((0, 0), (0, rpad)))
        w2p = _pad_rows(w2[l], 0, LANES)
        a2p = _pad_rows(a2[l], w_lora, LANES)
        g2p = _pad_rows(g2[l], 0, 2 * LANES)

        qkv = _in_proj(xt, row(norm1_w), w_attn, None, seq=S, tm=t["proj_tm"], tn=da_cols // 3)
        rcol = _in_proj(xt, row(norm1_w), w_rwkv, mu, seq=S, tm=t["proj_tm"], tn=w_rwkv.shape[1] // 3)

        o_attn = _diff_attention(qkv, row(lambda_q1), row(lambda_k1), row(lambda_q2), row(lambda_k2),
                                 row(subln_w), batch=B, seq=S, n_heads=n_aheads, tq=t["attn_tq"],
                                 lam_init=lam_init)
        o_rwkv = _rwkv(rcol, row(w0), row(a0), row(k_k), row(k_a), row(r_k), row(lnx_w), row(lnx_b),
                       w2p, a2p, g2p, batch=B, seq=S, n_pairs=n_pairs, rows=t["rwkv_rows"])

        aw = o_attn.shape[1]
        x1, h2 = _out_proj(xt, o_attn, o_rwkv, w_out[l][:aw].astype(BF16), w_out[l][aw:].astype(BF16),
                           row(norm2_w), tm=t["out_tm"])

        keys = peer_sub_keys[l].reshape(2 * p_heads, nk, -1).astype(BF16)
        s1, e1, s2, e2, tau = _route(h2, peer_w_query[l].astype(BF16), keys, tt=t["route_tt"], n_heads=p_heads)
        peer_t = _peer(h2, peer_down[l].astype(BF16), peer_up[l].T.astype(BF16), s1, e1, s2, e2, tau,
                       tt=t["peer_tt"], et=t["peer_et"], n_heads=p_heads, nk=nk)
        if l + 1 < depth:
            xt = x1 + peer_t.T
        else:
            return _final(x1, peer_t, norm_f_w.reshape(1, -1), tm=t["final_tm"]).reshape(B, S, D)
```

```python
import functools
import math

import jax
import jax.numpy as jnp
from jax import lax
from jax.experimental import pallas as pl
from jax.experimental.pallas import tpu as pltpu

F32 = jnp.float32
BF16 = jnp.bfloat16

LANES = 128
RMS_EPS = 1e-6
LNX_EPS = 64e-5
PEER_TOPK = 16
RWKV_CHUNK = 64
VMEM_LIMIT = 56 * 1024 * 1024
NEG = -1e30

_NT = (((1,), (1,)), ((), ()))


def _cparams(*sem):
    return pltpu.CompilerParams(dimension_semantics=sem, vmem_limit_bytes=VMEM_LIMIT)


def _rms(x, w):
    ms = jnp.mean(x * x, axis=-1, keepdims=True)
    return x * lax.rsqrt(ms + RMS_EPS) * w


def _proj_kernel(x_ref, nw_ref, w_ref, o_ref, h_ref):
    @pl.when(pl.program_id(1) == 0)
    def _():
        h_ref[...] = _rms(x_ref[...], nw_ref[...]).astype(BF16)

    o_ref[...] = jnp.dot(h_ref[...], w_ref[...], preferred_element_type=F32).astype(o_ref.dtype)


def _proj_shift_kernel(x_ref, nw_ref, w_ref, mu_ref, o_ref, h_ref, carry_ref, *, tiles_per_seq):
    i = pl.program_id(0)
    j = pl.program_id(1)

    @pl.when(j == 0)
    def _():
        h_ref[...] = _rms(x_ref[...], nw_ref[...]).astype(BF16)

    @pl.when(i % tiles_per_seq == 0)
    def _():
        carry_ref[j] = jnp.zeros(carry_ref.shape[1:], F32)

    p = jnp.dot(h_ref[...], w_ref[...], preferred_element_type=F32)
    tm = p.shape[0]
    rolled = pltpu.roll(p, 1, 0)
    row = lax.broadcasted_iota(jnp.int32, p.shape, 0)
    prev = jnp.where(row == 0, carry_ref[j][7:8, :], rolled)
    o_ref[...] = (p + (prev - p) * mu_ref[...]).astype(o_ref.dtype)
    carry_ref[j] = p[tm - 8:tm, :]


def _in_proj(x2, nw, w, mu, *, seq, tm, tn):
    T, D = x2.shape
    N = w.shape[1]
    grid = (T // tm, N // tn)
    in_specs = [
        pl.BlockSpec((tm, D), lambda i, j: (i, 0)),
        pl.BlockSpec((1, D), lambda i, j: (0, 0)),
        pl.BlockSpec((D, tn), lambda i, j: (0, j)),
    ]
    args = [x2, nw, w]
    scratch = [pltpu.VMEM((tm, D), BF16)]
    if mu is None:
        body = _proj_kernel
    else:
        body = functools.partial(_proj_shift_kernel, tiles_per_seq=seq // tm)
        in_specs.append(pl.BlockSpec((1, tn), lambda i, j: (0, j)))
        args.append(mu)
        scratch.append(pltpu.VMEM((N // tn, 8, tn), F32))
    return pl.pallas_call(
        body,
        grid=grid,
        in_specs=in_specs,
        out_specs=pl.BlockSpec((tm, tn), lambda i, j: (i, j)),
        out_shape=jax.ShapeDtypeStruct((T, N), BF16),
        scratch_shapes=scratch,
        compiler_params=_cparams("arbitrary", "arbitrary"),
        name="in_proj" if mu is None else "in_proj_shift",
    )(*args)


def _attn_kernel(q_ref, k_ref, v_ref, lq1_ref, lk1_ref, lq2_ref, lk2_ref, sw_ref, o_ref,
                 *, tq, n_heads, lam_init):
    h = pl.program_id(1)
    qi = pl.program_id(2)
    tk = tq
    dh = LANES // 2
    lane = lax.broadcasted_iota(jnp.int32, (1, LANES), 1)
    q = q_ref[...] * jnp.asarray(dh ** -0.5, BF16)
    zero = jnp.zeros_like(q)
    q1 = jnp.where(lane < dh, q, zero)
    q2 = jnp.where(lane >= dh, q, zero)

    hv = jnp.full((1, tk), h + 1, jnp.int32).astype(F32)
    slope = jnp.exp2(hv * (-8.0 / n_heads))
    col = lax.broadcasted_iota(jnp.int32, (1, tk), 1)
    row = lax.broadcasted_iota(jnp.int32, (tq, 1), 0)

    def step(kc, carry, masked):
        ks = pl.multiple_of(kc * tk, tk)
        kb = k_ref[pl.ds(ks, tk), :]
        vb = v_ref[pl.ds(ks, tk), :]
        rel = (ks - qi * tq) + col
        bias = slope * rel.astype(F32)
        s_both = (lax.dot_general(q1, kb, _NT, preferred_element_type=F32) + bias,
                  lax.dot_general(q2, kb, _NT, preferred_element_type=F32) + bias)
        out = []
        for c in range(2):
            m, l, a = carry[3 * c:3 * c + 3]
            s = s_both[c]
            if masked:
                s = jnp.where(rel <= row, s, NEG)
            mn = jnp.maximum(m, jnp.max(s, axis=-1, keepdims=True))
            p = jnp.exp(s - mn)
            al = jnp.exp(m - mn)
            l = al * l + jnp.sum(p, axis=-1, keepdims=True)
            a = al * a + jnp.dot(p.astype(BF16), vb, preferred_element_type=F32)
            out += [mn, l, a]
        return tuple(out)

    init = (jnp.full((tq, 1), NEG, F32), jnp.zeros((tq, 1), F32), jnp.zeros((tq, LANES), F32)) * 2
    carry = lax.fori_loop(0, qi, lambda kc, c: step(kc, c, False), init)
    m1, l1, a1, m2, l2, a2 = step(qi, carry, True)

    lam = (jnp.exp(jnp.sum(lq1_ref[...] * lk1_ref[...], axis=-1, keepdims=True))
           - jnp.exp(jnp.sum(lq2_ref[...] * lk2_ref[...], axis=-1, keepdims=True)) + lam_init)
    o = a1 / l1 - lam * (a2 / l2)
    o_ref[...] = (_rms(o, sw_ref[...]) * (1.0 - lam_init)).astype(o_ref.dtype)


def _diff_attention(qkv, lq1, lk1, lq2, lk2, subln_w, *, batch, seq, n_heads, tq, lam_init):
    T = qkv.shape[0]
    nq = seq // tq
    vec = lambda n: pl.BlockSpec((1, n), lambda b, h, i: (0, 0))
    return pl.pallas_call(
        functools.partial(_attn_kernel, tq=tq, n_heads=n_heads, lam_init=lam_init),
        grid=(batch, n_heads, nq),
        in_specs=[
            pl.BlockSpec((tq, LANES), lambda b, h, i: (b * nq + i, h)),
            pl.BlockSpec((seq, LANES), lambda b, h, i: (b, n_heads + h)),
            pl.BlockSpec((seq, LANES), lambda b, h, i: (b, 2 * n_heads + h)),
            vec(lq1.shape[1]), vec(lq1.shape[1]), vec(lq1.shape[1]), vec(lq1.shape[1]),
            vec(LANES),
        ],
        out_specs=pl.BlockSpec((tq, LANES), lambda b, h, i: (b * nq + i, h)),
        out_shape=jax.ShapeDtypeStruct((T, n_heads * LANES), BF16),
        compiler_params=_cparams("arbitrary", "arbitrary", "arbitrary"),
        name="diff_attn",
    )(qkv, qkv, qkv, lq1, lk1, lq2, lk2, subln_w)


def _split_bf16(x, n):
    parts = []
    for _ in range(n):
        p = x.astype(BF16)
        parts.append(p)
        x = x - p.astype(F32)
    return parts


def _dot(a, b):
    return jnp.dot(a.astype(BF16), b.astype(BF16), preferred_element_type=F32)


def _dot_exact_rhs(x, m_bf16, n):
    out = None
    for p in _split_bf16(x, n):
        t = jnp.dot(p, m_bf16, preferred_element_type=F32)
        out = t if out is None else out + t
    return out


def _dot_exact_lhs(m_bf16, x, n):
    out = None
    for p in _split_bf16(x, n):
        t = jnp.dot(m_bf16, p, preferred_element_type=F32)
        out = t if out is None else out + t
    return out


def _rwkv_kernel(r_ref, k_ref, v_ref, lo_ref, w0_ref, a0_ref, kk_ref, ka_ref, rk_ref, lw_ref, lb_ref,
                 w2_ref, a2_ref, g2_ref, o_ref, st_ref, *, rows):
    L = RWKV_CHUNK
    N = LANES // 2
    P = LANES

    @pl.when(pl.program_id(2) == 0)
    def _():
        st_ref[...] = jnp.zeros(st_ref.shape, F32)

    r = r_ref[...].astype(F32)
    k = k_ref[...].astype(F32)
    v = v_ref[...].astype(F32)
    lo = lo_ref[...]
    lo_wa = lo[:, :P]
    wl = jnp.dot(jnp.tanh(lo_wa.astype(F32)).astype(BF16), w2_ref[...], preferred_element_type=F32)
    al = jnp.dot(lo_wa, a2_ref[...], preferred_element_type=F32)
    g = jnp.dot(jax.nn.sigmoid(lo[:, P:].astype(F32)).astype(BF16), g2_ref[...], preferred_element_type=F32)

    wx = -(w0_ref[...] + wl)
    softplus = jnp.maximum(wx, 0.0) + jnp.log1p(jnp.exp(-jnp.abs(wx)))
    logdecay = -jnp.exp(-softplus - 0.5)
    a = jax.nn.sigmoid(a0_ref[...] + al)

    ri = lax.broadcasted_iota(jnp.int32, (P, P), 0)
    ci = lax.broadcasted_iota(jnp.int32, (P, P), 1)
    ones_bd = ((ri // N) == (ci // N)).astype(BF16)
    eye = (ri == ci).astype(F32)
    strict = ((ri % L) > (ci % L)).astype(F32)
    lower = ((ri % L) >= (ci % L)).astype(F32)
    ti = lax.broadcasted_iota(jnp.int32, (rows, rows), 0)
    tj = lax.broadcasted_iota(jnp.int32, (rows, rows), 1)
    tri_all = ((ti >= tj) & ((ti // L) == (tj // L))).astype(BF16)
    head0 = lax.broadcasted_iota(jnp.int32, (1, P), 1) < N

    kkr = k * kk_ref[...]
    norm = jnp.sqrt(_dot_exact_rhs(kkr * kkr, ones_bd, 2))
    kk = kkr / jnp.maximum(norm, 1e-12)
    kf = k * (1.0 + (a - 1.0) * ka_ref[...])
    avec = -kk
    bvec = kk * a

    def bd(x):
        x = x.astype(BF16)
        z = jnp.zeros_like(x)
        return jnp.concatenate([jnp.where(head0, x, z), jnp.where(head0, z, x)], axis=0)

    nc = rows // L
    chunks = range(nc)
    sls = [slice(c * L, (c + 1) * L) for c in chunks]
    cum_all = _dot_exact_lhs(tri_all, logdecay, 3)
    cum = [cum_all[s] for s in sls]
    tot = [cm[L - 1:L, :] for cm in cum]
    at = [bd(avec[s] * jnp.exp(cm - logdecay[s])) for s, cm in zip(sls, cum)]
    rt = [bd(r[s] * jnp.exp(cm)) for s, cm in zip(sls, cum)]
    inv = [jnp.exp(-cm) for cm in cum]
    bt = [bd(bvec[s] * iv) for s, iv in zip(sls, inv)]
    kt = [bd(kf[s] * iv) for s, iv in zip(sls, inv)]
    rest = [jnp.exp(t - cm) for t, cm in zip(tot, cum)]
    blkl_t = [jnp.concatenate([bd(bvec[s] * rs), bd(kf[s] * rs)], axis=1).astype(F32).T.astype(BF16)
              for s, rs in zip(sls, rest)]
    vd = [bd(v[s]) for s in sls]

    amat = [lax.dot_general(jnp.concatenate([at[c], rt[c]], axis=0), jnp.concatenate([bt[c], kt[c]], axis=0), _NT,
                            preferred_element_type=F32) for c in chunks]
    a_ab = [m[:P, :P] * strict for m in amat]
    a_ak = [(m[:P, P:] * strict).astype(BF16) for m in amat]
    a_rb = [(m[P:, :P] * lower).astype(BF16) for m in amat]
    a_rk = [(m[P:, P:] * lower).astype(BF16) for m in amat]

    tinv = [eye + n for n in a_ab]
    pw = [_dot(n, n) for n in a_ab]
    for _ in range(int(math.log2(L)) - 2):
        both = [_dot(pw[c], jnp.concatenate([pw[c], tinv[c]], axis=1)) for c in chunks]
        pw = [b[:, :P] for b in both]
        tinv = [tinv[c] + both[c][:, P:] for c in chunks]
    tinv = [tinv[c] + _dot(pw[c], tinv[c]) for c in chunks]

    avd = [_dot(jnp.concatenate([a_ak[c], a_rk[c], blkl_t[c][P:]], axis=0), vd[c]) for c in chunks]
    hat = [_dot(tinv[c], jnp.concatenate([at[c], avd[c][:P].astype(BF16)], axis=1)) for c in chunks]
    mix = [_dot(jnp.concatenate([a_rb[c], blkl_t[c][:P]], axis=0), hat[c]) for c in chunks]
    rhat = [rt[c].astype(F32) + mix[c][:P, :P] for c in chunks]
    ohat = [mix[c][:P, P:] + avd[c][P:2 * P] for c in chunks]
    mmat = [eye * jnp.exp(tot[c]) + mix[c][P:, :P] for c in chunks]
    cmat = [mix[c][P:, P:] + avd[c][2 * P:] for c in chunks]
    lhs = [jnp.concatenate([rhat[c], mmat[c]], axis=0).astype(BF16) for c in chunks]

    outs = []
    st = st_ref[...]
    for c in chunks:
        upd = _dot(lhs[c], st)
        od = upd[:P] + ohat[c]
        st = upd[P:] + cmat[c]
        outs.append(od[:L] + od[L:])
    st_ref[...] = st
    wkv = jnp.concatenate(outs, axis=0)

    inv_n = 1.0 / N
    mu = _dot_exact_rhs(wkv, ones_bd, 2) * inv_n
    xc = wkv - mu
    var = _dot_exact_rhs(xc * xc, ones_bd, 2) * inv_n
    o = xc * lax.rsqrt(var + LNX_EPS) * lw_ref[...] + lb_ref[...]
    bonus = _dot_exact_rhs(r * kf * rk_ref[...], ones_bd, 2) * v
    o_ref[...] = ((o + bonus) * g).astype(o_ref.dtype)


def _rwkv(cols, w0, a0, k_k, k_a, r_k, lnx_w, lnx_b, w2p, a2p, g2p, *, batch, seq, n_pairs, rows):
    T = cols.shape[0]
    nt = seq // rows
    lora_blk = (cols.shape[1] - 3 * n_pairs * LANES) // LANES
    row_map = lambda off: (lambda b, p, t: (b * nt + t, off + p))
    vec = pl.BlockSpec((1, LANES), lambda b, p, t: (0, p))
    return pl.pallas_call(
        functools.partial(_rwkv_kernel, rows=rows),
        grid=(batch, n_pairs, nt),
        in_specs=[
            pl.BlockSpec((rows, LANES), row_map(0)),
            pl.BlockSpec((rows, LANES), row_map(n_pairs)),
            pl.BlockSpec((rows, LANES), row_map(2 * n_pairs)),
            pl.BlockSpec((rows, lora_blk * LANES), lambda b, p, t: (b * nt + t, 3 * n_pairs // lora_blk)),
            vec, vec, vec, vec, vec, vec, vec,
            pl.BlockSpec((LANES, LANES), lambda b, p, t: (0, p)),
            pl.BlockSpec((LANES, LANES), lambda b, p, t: (0, p)),
            pl.BlockSpec((2 * LANES, LANES), lambda b, p, t: (0, p)),
        ],
        out_specs=pl.BlockSpec((rows, LANES), row_map(0)),
        out_shape=jax.ShapeDtypeStruct((T, n_pairs * LANES), BF16),
        scratch_shapes=[pltpu.VMEM((LANES, LANES), F32)],
        compiler_params=_cparams("arbitrary", "arbitrary", "arbitrary"),
        name="rwkv7",
    )(cols, cols, cols, cols, w0, a0, k_k, k_a, r_k, lnx_w, lnx_b, w2p, a2p, g2p)


def _out_proj_kernel(x_ref, oa_ref, or_ref, wa_ref, wr_ref, nw_ref, x1_ref, h2_ref):
    x1 = (x_ref[...] + jnp.dot(oa_ref[...], wa_ref[...], preferred_element_type=F32)
          + jnp.dot(or_ref[...], wr_ref[...], preferred_element_type=F32))
    x1_ref[...] = x1
    h2_ref[...] = _rms(x1, nw_ref[...]).astype(BF16)


def _out_proj(x2, oa, orw, wa, wr, nw, *, tm):
    T, D = x2.shape
    W = oa.shape[1]
    return pl.pallas_call(
        _out_proj_kernel,
        grid=(T // tm,),
        in_specs=[
            pl.BlockSpec((tm, D), lambda i: (i, 0)),
            pl.BlockSpec((tm, W), lambda i: (i, 0)),
            pl.BlockSpec((tm, W), lambda i: (i, 0)),
            pl.BlockSpec((W, D), lambda i: (0, 0)),
            pl.BlockSpec((W, D), lambda i: (0, 0)),
            pl.BlockSpec((1, D), lambda i: (0, 0)),
        ],
        out_specs=[pl.BlockSpec((tm, D), lambda i: (i, 0)), pl.BlockSpec((tm, D), lambda i: (i, 0))],
        out_shape=[jax.ShapeDtypeStruct((T, D), F32), jax.ShapeDtypeStruct((T, D), BF16)],
        compiler_params=_cparams("arbitrary"),
        name="out_proj",
    )(x2, oa, orw, wa, wr, nw)


def _topk_rows(cur, k):
    rows = []
    for _ in range(k):
        m = jnp.max(cur, axis=0, keepdims=True)
        rows.append(m)
        cur = jnp.where(cur == m, -jnp.inf, cur)
    return rows


_CANDS = [(a, b) for a in range(PEER_TOPK) for b in range(PEER_TOPK) if (a + 1) * (b + 1) <= PEER_TOPK]
_NCAND = -(-len(_CANDS) // 8) * 8


def _route_kernel(h2_ref, wq_ref, keys_ref, s1_ref, e1_ref, s2_ref, e2_ref, tau_ref, sc_ref, cand_ref,
                  *, n_heads):
    q = jnp.dot(h2_ref[...], wq_ref[...], preferred_element_type=F32).astype(BF16)
    for g in range(2 * n_heads):
        sc_ref[g] = lax.dot_general(keys_ref[g], q[:, g * LANES:(g + 1) * LANES], _NT,
                                    preferred_element_type=F32)
    cand_ref[...] = jnp.full(cand_ref.shape, -jnp.inf, F32)

    def per_head(h, _):
        s1 = sc_ref[2 * h]
        s2 = sc_ref[2 * h + 1]
        v1 = _topk_rows(s1, PEER_TOPK)
        v2 = _topk_rows(s2, PEER_TOPK)
        for n, (ia, ib) in enumerate(_CANDS):
            cand_ref[n:n + 1, :] = v1[ia] + v2[ib]
        cand = cand_ref[...]
        tau = _topk_rows(cand, PEER_TOPK)[-1]
        z = jnp.sum(jnp.where(cand >= tau, jnp.exp(cand - (v1[0] + v2[0])), 0.0), axis=0, keepdims=True)
        s1_ref[h] = s1
        e1_ref[h] = jnp.exp(s1 - v1[0])
        s2_ref[h] = s2
        e2_ref[h] = jnp.exp(s2 - v2[0]) / z
        tau_ref[pl.ds(h, 1), :] = tau
        return 0

    lax.fori_loop(0, n_heads, per_head, 0)


def _route(h2, wq, keys, *, tt, n_heads):
    T, D = h2.shape
    nk = keys.shape[1]
    big = jax.ShapeDtypeStruct((n_heads, nk, T), F32)
    blk = pl.BlockSpec((n_heads, nk, tt), lambda i: (0, 0, i))
    return pl.pallas_call(
        functools.partial(_route_kernel, n_heads=n_heads),
        grid=(T // tt,),
        in_specs=[
            pl.BlockSpec((tt, D), lambda i: (i, 0)),
            pl.BlockSpec(wq.shape, lambda i: (0, 0)),
            pl.BlockSpec(keys.shape, lambda i: (0, 0, 0)),
        ],
        out_specs=[blk, blk, blk, blk, pl.BlockSpec((n_heads, tt), lambda i: (0, i))],
        out_shape=[big, big, big, big, jax.ShapeDtypeStruct((n_heads, T), F32)],
        scratch_shapes=[pltpu.VMEM((2 * n_heads, nk, tt), F32), pltpu.VMEM((_NCAND, tt), F32)],
        compiler_params=_cparams("arbitrary"),
        name="peer_route",
    )(h2, wq, keys)


def _peer_kernel(h2_ref, down_ref, upt_ref, s1_ref, e1_ref, s2_ref, e2_ref, tau_ref, o_ref, act_ref,
                 *, n_heads, nk):
    @pl.when(pl.program_id(1) == 0)
    def _():
        o_ref[...] = jnp.zeros(o_ref.shape, F32)

    z = lax.dot_general(down_ref[...], h2_ref[...], _NT, preferred_element_type=F32)
    for ii in range(down_ref.shape[0] // nk):
        rs = slice(ii * nk, (ii + 1) * nk)
        gate = None
        for h in range(n_heads):
            total = s2_ref[h] + s1_ref[h, ii:ii + 1, :]
            val = jnp.where(total >= tau_ref[h:h + 1, :], e2_ref[h] * e1_ref[h, ii:ii + 1, :], 0.0)
            gate = val if gate is None else gate + val
        zz = z[rs]
        gelu = 0.5 * zz * (1.0 + lax.erf(zz * math.sqrt(0.5)))
        act_ref[rs, :] = (gelu * gate).astype(BF16)
    o_ref[...] += jnp.dot(upt_ref[...], act_ref[...], preferred_element_type=F32)


def _peer(h2, down, upt, s1, e1, s2, e2, tau, *, tt, et, n_heads, nk):
    T, D = h2.shape
    E = down.shape[0]
    i1_blk = et // nk
    row_blk = pl.BlockSpec((n_heads, i1_blk, tt), lambda i, j: (0, j, i))
    full_blk = pl.BlockSpec((n_heads, nk, tt), lambda i, j: (0, 0, i))
    return pl.pallas_call(
        functools.partial(_peer_kernel, n_heads=n_heads, nk=nk),
        grid=(T // tt, E // et),
        in_specs=[
            pl.BlockSpec((tt, D), lambda i, j: (i, 0)),
            pl.BlockSpec((et, D), lambda i, j: (j, 0)),
            pl.BlockSpec((D, et), lambda i, j: (0, j)),
            row_blk, row_blk, full_blk, full_blk,
            pl.BlockSpec((n_heads, tt), lambda i, j: (0, i)),
        ],
        out_specs=pl.BlockSpec((D, tt), lambda i, j: (0, i)),
        out_shape=jax.ShapeDtypeStruct((D, T), F32),
        scratch_shapes=[pltpu.VMEM((et, tt), BF16)],
        compiler_params=_cparams("arbitrary", "arbitrary"),
        name="peer_experts",
    )(h2, down, upt, s1, e1, s2, e2, tau)


def _final_kernel(x1_ref, pt_ref, nw_ref, o_ref):
    o_ref[...] = _rms(x1_ref[...] + pt_ref[...].T, nw_ref[...])


def _final(x1, peer_t, nw, *, tm):
    T, D = x1.shape
    return pl.pallas_call(
        _final_kernel,
        grid=(T // tm,),
        in_specs=[
            pl.BlockSpec((tm, D), lambda i: (i, 0)),
            pl.BlockSpec((D, tm), lambda i: (0, i)),
            pl.BlockSpec((1, D), lambda i: (0, 0)),
        ],
        out_specs=pl.BlockSpec((tm, D), lambda i: (i, 0)),
        out_shape=jax.ShapeDtypeStruct((T, D), F32),
        compiler_params=_cparams("arbitrary"),
        name="final_norm",
    )(x1, peer_t, nw)


def _tiles(seq, tokens):
    return dict(
        proj_tm=min(512, seq),
        attn_tq=min(512, seq),
        rwkv_rows=min(512, seq),
        out_tm=min(512, tokens),
        route_tt=min(256, tokens),
        peer_tt=min(512, tokens),
        peer_et=1024,
        final_tm=min(256, tokens),
    )


def _pad_rows(w, rows_before, rows_total):
    return jnp.pad(w, ((rows_before, rows_total - rows_before - w.shape[0]), (0, 0)))


def kernel(x, norm1_w, w_in, tshift_mu, w0, w2, a0, a2, g2, k_k, k_a, r_k, lnx_w, lnx_b, lambda_q1, lambda_k1,
           lambda_q2, lambda_k2, subln_w, w_out, norm2_w, peer_w_query, peer_sub_keys, peer_down, peer_up,
           norm_f_w):
    B, S, D = x.shape
    T = B * S
    depth = norm1_w.shape[0]
    dh = lambda_q1.shape[-1]
    n_rheads, rhead = r_k.shape[1], r_k.shape[2]
    rw = n_rheads * rhead
    n_pairs = rw // LANES
    w_lora, a_lora, g_lora = w2.shape[1], a2.shape[1], g2.shape[1]
    da_cols = w_in.shape[2] - (3 * rw + w_lora + a_lora + g_lora)
    n_aheads = da_cols // (3 * 2 * dh)
    p_heads, nk = peer_sub_keys.shape[1], peer_sub_keys.shape[3]
    assert 2 * dh == LANES and 2 * rhead == LANES and rhead == RWKV_CHUNK and nk == LANES
    assert w_lora + a_lora == LANES and g_lora <= 2 * LANES
    t = _tiles(S, T)

    xt = x.reshape(T, D)
    for l in range(depth):
        row = lambda p: p[l].reshape(1, -1)
        lam_init = 0.8 - 0.6 * math.exp(-0.3 * l)

        w_attn = w_in[l][:, :da_cols].astype(BF16)
        rcols = w_in[l].shape[1] - da_cols
        rpad = 3 * rw + 3 * LANES - rcols
        w_rwkv = jnp.pad(w_in[l][:, da_cols:], ((0, 0), (0, rpad))).astype(BF16)
        mu = jnp.pad(row(tshift_mu), ((0, 0), (0, rpad)))
        w2p = _pad_rows(w2[l], 0, LANES).astype(BF16)
        a2p = _pad_rows(a2[l], w_lora, LANES).astype(BF16)
        g2p = _pad_rows(g2[l], 0, 2 * LANES).astype(BF16)

        qkv = _in_proj(xt, row(norm1_w), w_attn, None, seq=S, tm=t["proj_tm"], tn=da_cols // 3)
        rcol = _in_proj(xt, row(norm1_w), w_rwkv, mu, seq=S, tm=t["proj_tm"], tn=w_rwkv.shape[1] // 3)

        o_attn = _diff_attention(qkv, row(lambda_q1), row(lambda_k1), row(lambda_q2), row(lambda_k2),
                                 row(subln_w), batch=B, seq=S, n_heads=n_aheads, tq=t["attn_tq"],
                                 lam_init=lam_init)
        o_rwkv = _rwkv(rcol, row(w0), row(a0), row(k_k), row(k_a), row(r_k), row(lnx_w), row(lnx_b),
                       w2p, a2p, g2p, batch=B, seq=S, n_pairs=n_pairs, rows=t["rwkv_rows"])

        aw = o_attn.shape[1]
        x1, h2 = _out_proj(xt, o_attn, o_rwkv, w_out[l][:aw].astype(BF16), w_out[l][aw:].astype(BF16),
                           row(norm2_w), tm=t["out_tm"])

        keys = peer_sub_keys[l].reshape(2 * p_heads, nk, -1).astype(BF16)
        s1, e1, s2, e2, tau = _route(h2, peer_w_query[l].astype(BF16), keys, tt=t["route_tt"], n_heads=p_heads)
        peer_t = _peer(h2, peer_down[l].astype(BF16), peer_up[l].T.astype(BF16), s1, e1, s2, e2, tau,
                       tt=t["peer_tt"], et=t["peer_et"], n_heads=p_heads, nk=nk)
        if l + 1 < depth:
            xt = x1 + peer_t.T
        else:
            return _final(x1, peer_t, norm_f_w.reshape(1, -1), tm=t["final_tm"]).reshape(B, S, D)
```

```python
import functools
import math

import jax
import jax.numpy as jnp
from jax import lax
from jax.experimental import pallas as pl
from jax.experimental.pallas import tpu as pltpu

F32 = jnp.float32
BF16 = jnp.bfloat16

LANES = 128
RMS_EPS = 1e-6
LNX_EPS = 64e-5
PEER_TOPK = 16
RWKV_CHUNK = 64
VMEM_LIMIT = 56 * 1024 * 1024
NEG = -1e30
SOFTMAX_UNDERFLOW = 110.0

_NT = (((1,), (1,)), ((), ()))


def _cparams(*sem):
    return pltpu.CompilerParams(dimension_semantics=sem, vmem_limit_bytes=VMEM_LIMIT)


def _rms(x, w):
    ms = jnp.mean(x * x, axis=-1, keepdims=True)
    return x * lax.rsqrt(ms + RMS_EPS) * w


def _proj_kernel(x_ref, nw_ref, w_ref, o_ref, h_ref):
    @pl.when(pl.program_id(1) == 0)
    def _():
        h_ref[...] = _rms(x_ref[...], nw_ref[...]).astype(BF16)

    o_ref[...] = jnp.dot(h_ref[...], w_ref[...], preferred_element_type=F32).astype(o_ref.dtype)


def _proj_shift_kernel(x_ref, nw_ref, w_ref, mu_ref, o_ref, h_ref, carry_ref, *, tiles_per_seq):
    i = pl.program_id(0)
    j = pl.program_id(1)

    @pl.when(j == 0)
    def _():
        h_ref[...] = _rms(x_ref[...], nw_ref[...]).astype(BF16)

    @pl.when(i % tiles_per_seq == 0)
    def _():
        carry_ref[j] = jnp.zeros(carry_ref.shape[1:], F32)

    p = jnp.dot(h_ref[...], w_ref[...], preferred_element_type=F32)
    tm = p.shape[0]
    rolled = pltpu.roll(p, 1, 0)
    row = lax.broadcasted_iota(jnp.int32, p.shape, 0)
    prev = jnp.where(row == 0, carry_ref[j][7:8, :], rolled)
    o_ref[...] = (p + (prev - p) * mu_ref[...]).astype(o_ref.dtype)
    carry_ref[j] = p[tm - 8:tm, :]


def _in_proj(x2, nw, w, mu, *, seq, tm, tn):
    T, D = x2.shape
    N = w.shape[1]
    grid = (T // tm, N // tn)
    in_specs = [
        pl.BlockSpec((tm, D), lambda i, j: (i, 0)),
        pl.BlockSpec((1, D), lambda i, j: (0, 0)),
        pl.BlockSpec((D, tn), lambda i, j: (0, j)),
    ]
    args = [x2, nw, w]
    scratch = [pltpu.VMEM((tm, D), BF16)]
    if mu is None:
        body = _proj_kernel
    else:
        body = functools.partial(_proj_shift_kernel, tiles_per_seq=seq // tm)
        in_specs.append(pl.BlockSpec((1, tn), lambda i, j: (0, j)))
        args.append(mu)
        scratch.append(pltpu.VMEM((N // tn, 8, tn), F32))
    return pl.pallas_call(
        body,
        grid=grid,
        in_specs=in_specs,
        out_specs=pl.BlockSpec((tm, tn), lambda i, j: (i, j)),
        out_shape=jax.ShapeDtypeStruct((T, N), BF16),
        scratch_shapes=scratch,
        compiler_params=_cparams("arbitrary", "arbitrary"),
        name="in_proj" if mu is None else "in_proj_shift",
    )(*args)


def _max_half_sqnorm(x, lane2, dh):
    sq = x * x
    lo = jnp.sum(jnp.where(lane2 < dh, sq, 0.0), axis=-1, keepdims=True)
    hi = jnp.sum(jnp.where(lane2 >= dh, sq, 0.0), axis=-1, keepdims=True)
    return jnp.max(jnp.maximum(lo, hi), axis=0, keepdims=True)


def _attn_kernel(q_ref, k_ref, v_ref, lq1_ref, lk1_ref, lq2_ref, lk2_ref, sw_ref, o_ref, kx1_ref, kx2_ref, knorm_ref,
                 *, tq, seq, n_heads, lam_init):
    h = pl.program_id(1)
    qi = pl.program_id(2)
    tk = tq
    dh = LANES // 2
    lane2 = lax.broadcasted_iota(jnp.int32, (tq, LANES), 1)

    @pl.when(qi == 0)
    def _():
        def build(c, knorm):
            rows = pl.ds(pl.multiple_of(c * tk, tk), tk)
            kb = k_ref[rows, :].astype(F32)
            pos = c * tk + lax.broadcasted_iota(jnp.int32, (tk, LANES), 0)
            hi = (pos // LANES).astype(F32)
            lo = (pos % LANES).astype(F32)
            kx1_ref[rows, :] = jnp.where(lane2 < dh, kb, jnp.where(lane2 == dh, hi, jnp.where(lane2 == dh + 1, lo, 0.0))
                                         ).astype(BF16)
            kx2_ref[rows, :] = jnp.where(lane2 >= dh, kb, jnp.where(lane2 == 0, hi, jnp.where(lane2 == 1, lo, 0.0))
                                         ).astype(BF16)
            return jnp.maximum(knorm, _max_half_sqnorm(kb, lane2, dh))

        knorm_ref[...] = lax.fori_loop(0, seq // tk, build, jnp.zeros((1, 1), F32))

    q = q_ref[...].astype(F32) * dh ** -0.5
    hv = jnp.full((tq, LANES), h + 1, jnp.int32).astype(F32)
    slope = jnp.exp2(hv * (-8.0 / n_heads))
    q1 = jnp.where(lane2 < dh, q, jnp.where(lane2 == dh, slope * LANES, jnp.where(lane2 == dh + 1, slope, 0.0))
                   ).astype(BF16)
    q2 = jnp.where(lane2 >= dh, q, jnp.where(lane2 == 0, slope * LANES, jnp.where(lane2 == 1, slope, 0.0))
                   ).astype(BF16)

    col = lax.broadcasted_iota(jnp.int32, (1, tk), 1)
    row = lax.broadcasted_iota(jnp.int32, (tq, 1), 0)

    def step(kc, carry, masked):
        rows = pl.ds(pl.multiple_of(kc * tk, tk), tk)
        vb = v_ref[rows, :]
        out = []
        for c, (qc, kx_ref) in enumerate(((q1, kx1_ref), (q2, kx2_ref))):
            m, l, a = carry[3 * c:3 * c + 3]
            s = lax.dot_general(qc, kx_ref[rows, :], _NT, preferred_element_type=F32)
            if masked:
                s = jnp.where(col <= row, s, NEG)
            mn = jnp.maximum(m, jnp.max(s, axis=-1, keepdims=True))
            p = jnp.exp(s - mn)
            al = jnp.exp(m - mn)
            l = al * l + jnp.sum(p, axis=-1, keepdims=True)
            a = al * a + jnp.dot(p.astype(BF16), vb, preferred_element_type=F32)
            out += [mn, l, a]
        return tuple(out)

    init = (jnp.full((tq, 1), NEG, F32), jnp.zeros((tq, 1), F32), jnp.zeros((tq, LANES), F32)) * 2
    bound = jnp.sqrt(_max_half_sqnorm(q, lane2, dh) * knorm_ref[...])
    reach = (SOFTMAX_UNDERFLOW + 2.0 * bound) / slope[:1, :1]
    q0 = jnp.full((1, 1), qi * tq, jnp.int32).astype(F32)
    first = jnp.clip(jnp.floor((q0 + 1.0 - reach) / tk), 0.0, q0 / tq)
    carry = lax.fori_loop(jnp.max(first.astype(jnp.int32)), qi, lambda kc, c: step(kc, c, False), init)
    m1, l1, a1, m2, l2, a2 = step(qi, carry, True)

    lam = (jnp.exp(jnp.sum(lq1_ref[...] * lk1_ref[...], axis=-1, keepdims=True))
           - jnp.exp(jnp.sum(lq2_ref[...] * lk2_ref[...], axis=-1, keepdims=True)) + lam_init)
    o = a1 / l1 - lam * (a2 / l2)
    o_ref[...] = (_rms(o, sw_ref[...]) * (1.0 - lam_init)).astype(o_ref.dtype)


def _diff_attention(qkv, lq1, lk1, lq2, lk2, subln_w, *, batch, seq, n_heads, tq, lam_init):
    T = qkv.shape[0]
    nq = seq // tq
    assert 8 % n_heads == 0 and seq <= 256 * LANES
    vec = lambda n: pl.BlockSpec((1, n), lambda b, h, i: (0, 0))
    return pl.pallas_call(
        functools.partial(_attn_kernel, tq=tq, seq=seq, n_heads=n_heads, lam_init=lam_init),
        grid=(batch, n_heads, nq),
        in_specs=[
            pl.BlockSpec((tq, LANES), lambda b, h, i: (b * nq + i, h)),
            pl.BlockSpec((seq, LANES), lambda b, h, i: (b, n_heads + h)),
            pl.BlockSpec((seq, LANES), lambda b, h, i: (b, 2 * n_heads + h)),
            vec(lq1.shape[1]), vec(lq1.shape[1]), vec(lq1.shape[1]), vec(lq1.shape[1]),
            vec(LANES),
        ],
        out_specs=pl.BlockSpec((tq, LANES), lambda b, h, i: (b * nq + i, h)),
        out_shape=jax.ShapeDtypeStruct((T, n_heads * LANES), BF16),
        scratch_shapes=[pltpu.VMEM((seq, LANES), BF16), pltpu.VMEM((seq, LANES), BF16), pltpu.VMEM((1, 1), F32)],
        compiler_params=_cparams("arbitrary", "arbitrary", "arbitrary"),
        name="diff_attn",
    )(qkv, qkv, qkv, lq1, lk1, lq2, lk2, subln_w)


def _split_bf16(x, n):
    parts = []
    for _ in range(n):
        p = x.astype(BF16)
        parts.append(p)
        x = x - p.astype(F32)
    return parts


def _dot(a, b):
    return jnp.dot(a.astype(BF16), b.astype(BF16), preferred_element_type=F32)


def _dot_exact_rhs(x, m_bf16, n):
    out = None
    for p in _split_bf16(x, n):
        t = jnp.dot(p, m_bf16, preferred_element_type=F32)
        out = t if out is None else out + t
    return out


def _dot_exact_lhs(m_bf16, x, n):
    out = None
    for p in _split_bf16(x, n):
        t = jnp.dot(m_bf16, p, preferred_element_type=F32)
        out = t if out is None else out + t
    return out


def _rwkv_kernel(r_ref, k_ref, v_ref, lo_ref, w0_ref, a0_ref, kk_ref, ka_ref, rk_ref, lw_ref, lb_ref,
                 w2_ref, a2_ref, g2_ref, o_ref, st_ref, *, rows):
    L = RWKV_CHUNK
    N = LANES // 2
    P = LANES

    @pl.when(pl.program_id(2) == 0)
    def _():
        st_ref[...] = jnp.zeros(st_ref.shape, F32)

    r = r_ref[...].astype(F32)
    k = k_ref[...].astype(F32)
    v = v_ref[...].astype(F32)
    lo = lo_ref[...]
    lo_wa = lo[:, :P]
    wl = jnp.dot(jnp.tanh(lo_wa.astype(F32)).astype(BF16), w2_ref[...], preferred_element_type=F32)
    al = jnp.dot(lo_wa, a2_ref[...], preferred_element_type=F32)
    g = jnp.dot(jax.nn.sigmoid(lo[:, P:].astype(F32)).astype(BF16), g2_ref[...], preferred_element_type=F32)

    wx = -(w0_ref[...] + wl)
    softplus = jnp.maximum(wx, 0.0) + jnp.log1p(jnp.exp(-jnp.abs(wx)))
    logdecay = -jnp.exp(-softplus - 0.5)
    a = jax.nn.sigmoid(a0_ref[...] + al)

    ri = lax.broadcasted_iota(jnp.int32, (P, P), 0)
    ci = lax.broadcasted_iota(jnp.int32, (P, P), 1)
    ones_bd = ((ri // N) == (ci // N)).astype(BF16)
    eye = (ri == ci).astype(F32)
    strict = ((ri % L) > (ci % L)).astype(F32)
    lower = ((ri % L) >= (ci % L)).astype(F32)
    ti = lax.broadcasted_iota(jnp.int32, (rows, rows), 0)
    tj = lax.broadcasted_iota(jnp.int32, (rows, rows), 1)
    tri_all = ((ti >= tj) & ((ti // L) == (tj // L))).astype(BF16)
    head0 = lax.broadcasted_iota(jnp.int32, (1, P), 1) < N

    kkr = k * kk_ref[...]
    norm = jnp.sqrt(_dot_exact_rhs(kkr * kkr, ones_bd, 2))
    kk = kkr / jnp.maximum(norm, 1e-12)
    kf = k * (1.0 + (a - 1.0) * ka_ref[...])
    avec = -kk
    bvec = kk * a

    def bd(x):
        x = x.astype(BF16)
        z = jnp.zeros_like(x)
        return jnp.concatenate([jnp.where(head0, x, z), jnp.where(head0, z, x)], axis=0)

    nc = rows // L
    chunks = range(nc)
    sls = [slice(c * L, (c + 1) * L) for c in chunks]
    cum_all = _dot_exact_lhs(tri_all, logdecay, 3)
    cum = [cum_all[s] for s in sls]
    tot = [cm[L - 1:L, :] for cm in cum]
    at = [bd(avec[s] * jnp.exp(cm - logdecay[s])) for s, cm in zip(sls, cum)]
    rt = [bd(r[s] * jnp.exp(cm)) for s, cm in zip(sls, cum)]
    inv = [jnp.exp(-cm) for cm in cum]
    bt = [bd(bvec[s] * iv) for s, iv in zip(sls, inv)]
    kt = [bd(kf[s] * iv) for s, iv in zip(sls, inv)]
    rest = [jnp.exp(t - cm) for t, cm in zip(tot, cum)]
    blkl_t = [jnp.concatenate([bd(bvec[s] * rs), bd(kf[s] * rs)], axis=1).astype(F32).T.astype(BF16)
              for s, rs in zip(sls, rest)]
    vd = [bd(v[s]) for s in sls]

    amat = [lax.dot_general(jnp.concatenate([at[c], rt[c]], axis=0), jnp.concatenate([bt[c], kt[c]], axis=0), _NT,
                            preferred_element_type=F32) for c in chunks]
    a_ab = [m[:P, :P] * strict for m in amat]
    a_ak = [(m[:P, P:] * strict).astype(BF16) for m in amat]
    a_rb = [(m[P:, :P] * lower).astype(BF16) for m in amat]
    a_rk = [(m[P:, P:] * lower).astype(BF16) for m in amat]

    tinv = [eye + n for n in a_ab]
    pw = [_dot(n, n) for n in a_ab]
    for _ in range(int(math.log2(L)) - 2):
        both = [_dot(pw[c], jnp.concatenate([pw[c], tinv[c]], axis=1)) for c in chunks]
        pw = [b[:, :P] for b in both]
        tinv = [tinv[c] + both[c][:, P:] for c in chunks]
    tinv = [tinv[c] + _dot(pw[c], tinv[c]) for c in chunks]

    avd = [_dot(jnp.concatenate([a_ak[c], a_rk[c], blkl_t[c][P:]], axis=0), vd[c]) for c in chunks]
    hat = [_dot(tinv[c], jnp.concatenate([at[c], avd[c][:P].astype(BF16)], axis=1)) for c in chunks]
    mix = [_dot(jnp.concatenate([a_rb[c], blkl_t[c][:P]], axis=0), hat[c]) for c in chunks]
    rhat = [rt[c].astype(F32) + mix[c][:P, :P] for c in chunks]
    ohat = [mix[c][:P, P:] + avd[c][P:2 * P] for c in chunks]
    mmat = [eye * jnp.exp(tot[c]) + mix[c][P:, :P] for c in chunks]
    cmat = [mix[c][P:, P:] + avd[c][2 * P:] for c in chunks]
    lhs = [jnp.concatenate([rhat[c], mmat[c]], axis=0).astype(BF16) for c in chunks]

    outs = []
    st = st_ref[...]
    for c in chunks:
        upd = _dot(lhs[c], st)
        od = upd[:P] + ohat[c]
        st = upd[P:] + cmat[c]
        outs.append(od[:L] + od[L:])
    st_ref[...] = st
    wkv = jnp.concatenate(outs, axis=0)

    inv_n = 1.0 / N
    mu = _dot_exact_rhs(wkv, ones_bd, 2) * inv_n
    xc = wkv - mu
    var = _dot_exact_rhs(xc * xc, ones_bd, 2) * inv_n
    o = xc * lax.rsqrt(var + LNX_EPS) * lw_ref[...] + lb_ref[...]
    bonus = _dot_exact_rhs(r * kf * rk_ref[...], ones_bd, 2) * v
    o_ref[...] = ((o + bonus) * g).astype(o_ref.dtype)


def _rwkv(cols, w0, a0, k_k, k_a, r_k, lnx_w, lnx_b, w2p, a2p, g2p, *, batch, seq, n_pairs, rows):
    T = cols.shape[0]
    nt = seq // rows
    lora_blk = (cols.shape[1] - 3 * n_pairs * LANES) // LANES
    row_map = lambda off: (lambda b, p, t: (b * nt + t, off + p))
    vec = pl.BlockSpec((1, LANES), lambda b, p, t: (0, p))
    return pl.pallas_call(
        functools.partial(_rwkv_kernel, rows=rows),
        grid=(batch, n_pairs, nt),
        in_specs=[
            pl.BlockSpec((rows, LANES), row_map(0)),
            pl.BlockSpec((rows, LANES), row_map(n_pairs)),
            pl.BlockSpec((rows, LANES), row_map(2 * n_pairs)),
            pl.BlockSpec((rows, lora_blk * LANES), lambda b, p, t: (b * nt + t, 3 * n_pairs // lora_blk)),
            vec, vec, vec, vec, vec, vec, vec,
            pl.BlockSpec((LANES, LANES), lambda b, p, t: (0, p)),
            pl.BlockSpec((LANES, LANES), lambda b, p, t: (0, p)),
            pl.BlockSpec((2 * LANES, LANES), lambda b, p, t: (0, p)),
        ],
        out_specs=pl.BlockSpec((rows, LANES), row_map(0)),
        out_shape=jax.ShapeDtypeStruct((T, n_pairs * LANES), BF16),
        scratch_shapes=[pltpu.VMEM((LANES, LANES), F32)],
        compiler_params=_cparams("arbitrary", "arbitrary", "arbitrary"),
        name="rwkv7",
    )(cols, cols, cols, cols, w0, a0, k_k, k_a, r_k, lnx_w, lnx_b, w2p, a2p, g2p)


def _out_proj_kernel(x_ref, oa_ref, or_ref, wa_ref, wr_ref, nw_ref, x1_ref, h2_ref):
    x1 = (x_ref[...] + jnp.dot(oa_ref[...], wa_ref[...], preferred_element_type=F32)
          + jnp.dot(or_ref[...], wr_ref[...], preferred_element_type=F32))
    x1_ref[...] = x1
    h2_ref[...] = _rms(x1, nw_ref[...]).astype(BF16)


def _out_proj(x2, oa, orw, wa, wr, nw, *, tm):
    T, D = x2.shape
    W = oa.shape[1]
    return pl.pallas_call(
        _out_proj_kernel,
        grid=(T // tm,),
        in_specs=[
            pl.BlockSpec((tm, D), lambda i: (i, 0)),
            pl.BlockSpec((tm, W), lambda i: (i, 0)),
            pl.BlockSpec((tm, W), lambda i: (i, 0)),
            pl.BlockSpec((W, D), lambda i: (0, 0)),
            pl.BlockSpec((W, D), lambda i: (0, 0)),
            pl.BlockSpec((1, D), lambda i: (0, 0)),
        ],
        out_specs=[pl.BlockSpec((tm, D), lambda i: (i, 0)), pl.BlockSpec((tm, D), lambda i: (i, 0))],
        out_shape=[jax.ShapeDtypeStruct((T, D), F32), jax.ShapeDtypeStruct((T, D), BF16)],
        compiler_params=_cparams("arbitrary"),
        name="out_proj",
    )(x2, oa, orw, wa, wr, nw)


def _topk_rows(cur, k):
    rows = []
    for _ in range(k):
        m = jnp.max(cur, axis=0, keepdims=True)
        rows.append(m)
        cur = jnp.where(cur == m, -jnp.inf, cur)
    return rows


_CANDS = [(a, b) for a in range(PEER_TOPK) for b in range(PEER_TOPK) if (a + 1) * (b + 1) <= PEER_TOPK]
_NCAND = -(-len(_CANDS) // 8) * 8


def _route_kernel(h2_ref, wq_ref, keys_ref, s1_ref, e1_ref, s2_ref, e2_ref, tau_ref, sc_ref, cand_ref,
                  *, n_heads):
    q = jnp.dot(h2_ref[...], wq_ref[...], preferred_element_type=F32).astype(BF16)
    for g in range(2 * n_heads):
        sc_ref[g] = lax.dot_general(keys_ref[g], q[:, g * LANES:(g + 1) * LANES], _NT,
                                    preferred_element_type=F32)
    cand_ref[...] = jnp.full(cand_ref.shape, -jnp.inf, F32)

    def per_head(h, _):
        s1 = sc_ref[2 * h]
        s2 = sc_ref[2 * h + 1]
        v1 = _topk_rows(s1, PEER_TOPK)
        v2 = _topk_rows(s2, PEER_TOPK)
        for n, (ia, ib) in enumerate(_CANDS):
            cand_ref[n:n + 1, :] = v1[ia] + v2[ib]
        cand = cand_ref[...]
        tau = _topk_rows(cand, PEER_TOPK)[-1]
        z = jnp.sum(jnp.where(cand >= tau, jnp.exp(cand - (v1[0] + v2[0])), 0.0), axis=0, keepdims=True)
        s1_ref[h] = s1
        e1_ref[h] = jnp.exp(s1 - v1[0])
        s2_ref[h] = s2
        e2_ref[h] = jnp.exp(s2 - v2[0]) / z
        tau_ref[pl.ds(h, 1), :] = tau
        return 0

    lax.fori_loop(0, n_heads, per_head, 0)


def _route(h2, wq, keys, *, tt, n_heads):
    T, D = h2.shape
    nk = keys.shape[1]
    big = jax.ShapeDtypeStruct((n_heads, nk, T), F32)
    blk = pl.BlockSpec((n_heads, nk, tt), lambda i: (0, 0, i))
    return pl.pallas_call(
        functools.partial(_route_kernel, n_heads=n_heads),
        grid=(T // tt,),
        in_specs=[
            pl.BlockSpec((tt, D), lambda i: (i, 0)),
            pl.BlockSpec(wq.shape, lambda i: (0, 0)),
            pl.BlockSpec(keys.shape, lambda i: (0, 0, 0)),
        ],
        out_specs=[blk, blk, blk, blk, pl.BlockSpec((n_heads, tt), lambda i: (0, i))],
        out_shape=[big, big, big, big, jax.ShapeDtypeStruct((n_heads, T), F32)],
        scratch_shapes=[pltpu.VMEM((2 * n_heads, nk, tt), F32), pltpu.VMEM((_NCAND, tt), F32)],
        compiler_params=_cparams("arbitrary"),
        name="peer_route",
    )(h2, wq, keys)


def _peer_kernel(h2_ref, down_ref, upt_ref, s1_ref, e1_ref, s2_ref, e2_ref, tau_ref, o_ref,
                 z_a, z_b, act_ref, *, n_heads, nk):
    j = pl.program_id(1)

    @pl.when(j == 0)
    def _():
        o_ref[...] = jnp.zeros(o_ref.shape, F32)
        z_b[...] = jnp.zeros(z_b.shape, F32)

    def body(z_new, z_old):
        z_new[...] = lax.dot_general(down_ref[...], h2_ref[...], _NT, preferred_element_type=F32)
        for ii in range(down_ref.shape[0] // nk):
            rs = slice(ii * nk, (ii + 1) * nk)
            gate = None
            for h in range(n_heads):
                total = s2_ref[h] + s1_ref[h, ii:ii + 1, :]
                val = jnp.where(total >= tau_ref[h:h + 1, :], e2_ref[h] * e1_ref[h, ii:ii + 1, :], 0.0)
                gate = val if gate is None else gate + val
            zz = z_old[rs, :]
            gelu = 0.5 * zz * (1.0 + lax.erf(zz * math.sqrt(0.5)))
            act_ref[rs, :] = (gelu * gate).astype(BF16)
        o_ref[...] += jnp.dot(upt_ref[...], act_ref[...], preferred_element_type=F32)

    @pl.when(j % 2 == 0)
    def _():
        body(z_a, z_b)

    @pl.when(j % 2 == 1)
    def _():
        body(z_b, z_a)


def _peer(h2, down, upt, s1, e1, s2, e2, tau, *, tt, et, n_heads, nk):
    T, D = h2.shape
    E = down.shape[0]
    n_tiles = E // et
    assert n_tiles % 2 == 0
    i1_blk = et // nk
    cur = lambda j: jnp.minimum(j, n_tiles - 1)
    prev = lambda j: jnp.maximum(j - 1, 0)
    row_blk = pl.BlockSpec((n_heads, i1_blk, tt), lambda i, j: (0, prev(j), i))
    full_blk = pl.BlockSpec((n_heads, nk, tt), lambda i, j: (0, 0, i))
    return pl.pallas_call(
        functools.partial(_peer_kernel, n_heads=n_heads, nk=nk),
        grid=(T // tt, n_tiles + 1),
        in_specs=[
            pl.BlockSpec((tt, D), lambda i, j: (i, 0)),
            pl.BlockSpec((et, D), lambda i, j: (cur(j), 0)),
            pl.BlockSpec((D, et), lambda i, j: (0, prev(j))),
            row_blk, row_blk, full_blk, full_blk,
            pl.BlockSpec((n_heads, tt), lambda i, j: (0, i)),
        ],
        out_specs=pl.BlockSpec((D, tt), lambda i, j: (0, i)),
        out_shape=jax.ShapeDtypeStruct((D, T), F32),
        scratch_shapes=[pltpu.VMEM((et, tt), F32), pltpu.VMEM((et, tt), F32), pltpu.VMEM((et, tt), BF16)],
        compiler_params=_cparams("arbitrary", "arbitrary"),
        name="peer_experts",
    )(h2, down, upt, s1, e1, s2, e2, tau)


def _final_kernel(x1_ref, pt_ref, nw_ref, o_ref):
    o_ref[...] = _rms(x1_ref[...] + pt_ref[...].T, nw_ref[...])


def _final(x1, peer_t, nw, *, tm):
    T, D = x1.shape
    return pl.pallas_call(
        _final_kernel,
        grid=(T // tm,),
        in_specs=[
            pl.BlockSpec((tm, D), lambda i: (i, 0)),
            pl.BlockSpec((D, tm), lambda i: (0, i)),
            pl.BlockSpec((1, D), lambda i: (0, 0)),
        ],
        out_specs=pl.BlockSpec((tm, D), lambda i: (i, 0)),
        out_shape=jax.ShapeDtypeStruct((T, D), F32),
        compiler_params=_cparams("arbitrary"),
        name="final_norm",
    )(x1, peer_t, nw)


def _tiles(seq, tokens):
    return dict(
        proj_tm=min(512, seq),
        attn_tq=min(512, seq),
        rwkv_rows=min(512, seq),
        out_tm=min(512, tokens),
        route_tt=min(256, tokens),
        peer_tt=min(512, tokens),
        peer_et=1024,
        final_tm=min(256, tokens),
    )


def _pad_rows(w, rows_before, rows_total):
    return jnp.pad(w, ((rows_before, rows_total - rows_before - w.shape[0]), (0, 0)))


def kernel(x, norm1_w, w_in, tshift_mu, w0, w2, a0, a2, g2, k_k, k_a, r_k, lnx_w, lnx_b, lambda_q1, lambda_k1,
           lambda_q2, lambda_k2, subln_w, w_out, norm2_w, peer_w_query, peer_sub_keys, peer_down, peer_up,
           norm_f_w):
    B, S, D = x.shape
    T = B * S
    depth = norm1_w.shape[0]
    dh = lambda_q1.shape[-1]
    n_rheads, rhead = r_k.shape[1], r_k.shape[2]
    rw = n_rheads * rhead
    n_pairs = rw // LANES
    w_lora, a_lora, g_lora = w2.shape[1], a2.shape[1], g2.shape[1]
    da_cols = w_in.shape[2] - (3 * rw + w_lora + a_lora + g_lora)
    n_aheads = da_cols // (3 * 2 * dh)
    p_heads, nk = peer_sub_keys.shape[1], peer_sub_keys.shape[3]
    assert 2 * dh == LANES and 2 * rhead == LANES and rhead == RWKV_CHUNK and nk == LANES
    assert w_lora + a_lora == LANES and g_lora <= 2 * LANES
    t = _tiles(S, T)

    xt = x.reshape(T, D)
    for l in range(depth):
        row = lambda p: p[l].reshape(1, -1)
        lam_init = 0.8 - 0.6 * math.exp(-0.3 * l)

        w_attn = w_in[l][:, :da_cols].astype(BF16)
        rcols = w_in[l].shape[1] - da_cols
        rpad = 3 * rw + 3 * LANES - rcols
        w_rwkv = jnp.pad(w_in[l][:, da_cols:], ((0, 0), (0, rpad))).astype(BF16)
        mu = jnp.pad(row(tshift_mu), ((0, 0), (0, rpad)))
        w2p = _pad_rows(w2[l], 0, LANES).astype(BF16)
        a2p = _pad_rows(a2[l], w_lora, LANES).astype(BF16)
        g2p = _pad_rows(g2[l], 0, 2 * LANES).astype(BF16)

        qkv = _in_proj(xt, row(norm1_w), w_attn, None, seq=S, tm=t["proj_tm"], tn=da_cols // 3)
        rcol = _in_proj(xt, row(norm1_w), w_rwkv, mu, seq=S, tm=t["proj_tm"], tn=w_rwkv.shape[1] // 3)

        o_attn = _diff_attention(qkv, row(lambda_q1), row(lambda_k1), row(lambda_q2), row(lambda_k2),
                                 row(subln_w), batch=B, seq=S, n_heads=n_aheads, tq=t["attn_tq"],
                                 lam_init=lam_init)
        o_rwkv = _rwkv(rcol, row(w0), row(a0), row(k_k), row(k_a), row(r_k), row(lnx_w), row(lnx_b),
                       w2p, a2p, g2p, batch=B, seq=S, n_pairs=n_pairs, rows=t["rwkv_rows"])

        aw = o_attn.shape[1]
        x1, h2 = _out_proj(xt, o_attn, o_rwkv, w_out[l][:aw].astype(BF16), w_out[l][aw:].astype(BF16),
                           row(norm2_w), tm=t["out_tm"])

        keys = peer_sub_keys[l].reshape(2 * p_heads, nk, -1).astype(BF16)
        s1, e1, s2, e2, tau = _route(h2, peer_w_query[l].astype(BF16), keys, tt=t["route_tt"], n_heads=p_heads)
        peer_t = _peer(h2, peer_down[l].astype(BF16), peer_up[l].T.astype(BF16), s1, e1, s2, e2, tau,
                       tt=t["peer_tt"], et=t["peer_et"], n_heads=p_heads, nk=nk)
        if l + 1 < depth:
            xt = x1 + peer_t.T
        else:
            return _final(x1, peer_t, norm_f_w.reshape(1, -1), tm=t["final_tm"]).reshape(B, S, D)
```

```python
import functools
import math

import jax
import jax.numpy as jnp
from jax import lax
from jax.experimental import pallas as pl
from jax.experimental.pallas import tpu as pltpu

F32 = jnp.float32
BF16 = jnp.bfloat16

LANES = 128
RMS_EPS = 1e-6
LNX_EPS = 64e-5
PEER_TOPK = 16
RWKV_CHUNK = 64
PEER_Z_CHUNKS = 8
VMEM_LIMIT = 56 * 1024 * 1024
NEG = -1e30
SOFTMAX_UNDERFLOW = 110.0

_NT = (((1,), (1,)), ((), ()))


def _cparams(*sem):
    return pltpu.CompilerParams(dimension_semantics=sem, vmem_limit_bytes=VMEM_LIMIT)


def _rms(x, w):
    ms = jnp.mean(x * x, axis=-1, keepdims=True)
    return x * lax.rsqrt(ms + RMS_EPS) * w


def _proj_kernel(x_ref, nw_ref, w_ref, o_ref, h_ref):
    @pl.when(pl.program_id(1) == 0)
    def _():
        h_ref[...] = _rms(x_ref[...], nw_ref[...]).astype(BF16)

    o_ref[...] = jnp.dot(h_ref[...], w_ref[...], preferred_element_type=F32).astype(o_ref.dtype)


def _proj_shift_kernel(x_ref, nw_ref, w_ref, mu_ref, o_ref, h_ref, carry_ref, *, tiles_per_seq):
    i = pl.program_id(0)
    j = pl.program_id(1)

    @pl.when(j == 0)
    def _():
        h_ref[...] = _rms(x_ref[...], nw_ref[...]).astype(BF16)

    @pl.when(i % tiles_per_seq == 0)
    def _():
        carry_ref[j] = jnp.zeros(carry_ref.shape[1:], F32)

    p = jnp.dot(h_ref[...], w_ref[...], preferred_element_type=F32)
    tm = p.shape[0]
    rolled = pltpu.roll(p, 1, 0)
    row = lax.broadcasted_iota(jnp.int32, p.shape, 0)
    prev = jnp.where(row == 0, carry_ref[j][7:8, :], rolled)
    o_ref[...] = (p + (prev - p) * mu_ref[...]).astype(o_ref.dtype)
    carry_ref[j] = p[tm - 8:tm, :]


def _in_proj(x2, nw, w, mu, *, seq, tm, tn):
    T, D = x2.shape
    N = w.shape[1]
    grid = (T // tm, N // tn)
    in_specs = [
        pl.BlockSpec((tm, D), lambda i, j: (i, 0)),
        pl.BlockSpec((1, D), lambda i, j: (0, 0)),
        pl.BlockSpec((D, tn), lambda i, j: (0, j)),
    ]
    args = [x2, nw, w]
    scratch = [pltpu.VMEM((tm, D), BF16)]
    if mu is None:
        body = _proj_kernel
    else:
        body = functools.partial(_proj_shift_kernel, tiles_per_seq=seq // tm)
        in_specs.append(pl.BlockSpec((1, tn), lambda i, j: (0, j)))
        args.append(mu)
        scratch.append(pltpu.VMEM((N // tn, 8, tn), F32))
    return pl.pallas_call(
        body,
        grid=grid,
        in_specs=in_specs,
        out_specs=pl.BlockSpec((tm, tn), lambda i, j: (i, j)),
        out_shape=jax.ShapeDtypeStruct((T, N), BF16),
        scratch_shapes=scratch,
        compiler_params=_cparams("arbitrary", "arbitrary"),
        name="in_proj" if mu is None else "in_proj_shift",
    )(*args)


def _max_half_sqnorm(x, lane2, dh):
    sq = x * x
    lo = jnp.sum(jnp.where(lane2 < dh, sq, 0.0), axis=-1, keepdims=True)
    hi = jnp.sum(jnp.where(lane2 >= dh, sq, 0.0), axis=-1, keepdims=True)
    return jnp.max(jnp.maximum(lo, hi), axis=0, keepdims=True)


def _attn_kernel(q_ref, k_ref, v_ref, lq1_ref, lk1_ref, lq2_ref, lk2_ref, sw_ref, o_ref, kx1_ref, kx2_ref, knorm_ref,
                 *, tq, seq, n_heads, lam_init):
    h = pl.program_id(1)
    qi = pl.program_id(2)
    tk = tq
    dh = LANES // 2
    lane2 = lax.broadcasted_iota(jnp.int32, (tq, LANES), 1)

    @pl.when(qi == 0)
    def _():
        def build(c, knorm):
            rows = pl.ds(pl.multiple_of(c * tk, tk), tk)
            kb = k_ref[rows, :].astype(F32)
            pos = c * tk + lax.broadcasted_iota(jnp.int32, (tk, LANES), 0)
            hi = (pos // LANES).astype(F32)
            lo = (pos % LANES).astype(F32)
            kx1_ref[rows, :] = jnp.where(lane2 < dh, kb, jnp.where(lane2 == dh, hi, jnp.where(lane2 == dh + 1, lo, 0.0))
                                         ).astype(BF16)
            kx2_ref[rows, :] = jnp.where(lane2 >= dh, kb, jnp.where(lane2 == 0, hi, jnp.where(lane2 == 1, lo, 0.0))
                                         ).astype(BF16)
            return jnp.maximum(knorm, _max_half_sqnorm(kb, lane2, dh))

        knorm_ref[...] = lax.fori_loop(0, seq // tk, build, jnp.zeros((1, 1), F32))

    q = q_ref[...].astype(F32) * dh ** -0.5
    hv = jnp.full((tq, LANES), h + 1, jnp.int32).astype(F32)
    slope = jnp.exp2(hv * (-8.0 / n_heads))
    q1 = jnp.where(lane2 < dh, q, jnp.where(lane2 == dh, slope * LANES, jnp.where(lane2 == dh + 1, slope, 0.0))
                   ).astype(BF16)
    q2 = jnp.where(lane2 >= dh, q, jnp.where(lane2 == 0, slope * LANES, jnp.where(lane2 == 1, slope, 0.0))
                   ).astype(BF16)

    col = lax.broadcasted_iota(jnp.int32, (1, tk), 1)
    row = lax.broadcasted_iota(jnp.int32, (tq, 1), 0)

    def step(kc, carry, masked):
        rows = pl.ds(pl.multiple_of(kc * tk, tk), tk)
        vb = v_ref[rows, :]
        maps = (0, 1)
        ss = [lax.dot_general(qc, kx_ref[rows, :], _NT, preferred_element_type=F32)
              for qc, kx_ref in ((q1, kx1_ref), (q2, kx2_ref))]
        if masked:
            ss = [jnp.where(col <= row, s, NEG) for s in ss]
        ms, ls, accs = carry[0::3], carry[1::3], carry[2::3]
        mns = [jnp.maximum(ms[c], jnp.max(ss[c], axis=-1, keepdims=True)) for c in maps]
        ps = [jnp.exp(ss[c] - mns[c]) for c in maps]
        als = [jnp.exp(ms[c] - mns[c]) for c in maps]
        ls = [als[c] * ls[c] + jnp.sum(ps[c], axis=-1, keepdims=True) for c in maps]
        pvs = [jnp.dot(ps[c].astype(BF16), vb, preferred_element_type=F32) for c in maps]
        accs = [als[c] * accs[c] + pvs[c] for c in maps]
        return (mns[0], ls[0], accs[0], mns[1], ls[1], accs[1])

    init = (jnp.full((tq, 1), NEG, F32), jnp.zeros((tq, 1), F32), jnp.zeros((tq, LANES), F32)) * 2
    bound = jnp.sqrt(_max_half_sqnorm(q, lane2, dh) * knorm_ref[...])
    reach = (SOFTMAX_UNDERFLOW + 2.0 * bound) / slope[:1, :1]
    q0 = jnp.full((1, 1), qi * tq, jnp.int32).astype(F32)
    first = jnp.clip(jnp.floor((q0 + 1.0 - reach) / tk), 0.0, q0 / tq)
    carry = lax.fori_loop(jnp.max(first.astype(jnp.int32)), qi, lambda kc, c: step(kc, c, False), init)
    m1, l1, a1, m2, l2, a2 = step(qi, carry, True)

    lam = (jnp.exp(jnp.sum(lq1_ref[...] * lk1_ref[...], axis=-1, keepdims=True))
           - jnp.exp(jnp.sum(lq2_ref[...] * lk2_ref[...], axis=-1, keepdims=True)) + lam_init)
    o = a1 / l1 - lam * (a2 / l2)
    o_ref[...] = (_rms(o, sw_ref[...]) * (1.0 - lam_init)).astype(o_ref.dtype)


def _diff_attention(qkv, lq1, lk1, lq2, lk2, subln_w, *, batch, seq, n_heads, tq, lam_init):
    T = qkv.shape[0]
    nq = seq // tq
    assert 8 % n_heads == 0 and seq <= 256 * LANES
    vec = lambda n: pl.BlockSpec((1, n), lambda b, h, i: (0, 0))
    return pl.pallas_call(
        functools.partial(_attn_kernel, tq=tq, seq=seq, n_heads=n_heads, lam_init=lam_init),
        grid=(batch, n_heads, nq),
        in_specs=[
            pl.BlockSpec((tq, LANES), lambda b, h, i: (b * nq + i, h)),
            pl.BlockSpec((seq, LANES), lambda b, h, i: (b, n_heads + h)),
            pl.BlockSpec((seq, LANES), lambda b, h, i: (b, 2 * n_heads + h)),
            vec(lq1.shape[1]), vec(lq1.shape[1]), vec(lq1.shape[1]), vec(lq1.shape[1]),
            vec(LANES),
        ],
        out_specs=pl.BlockSpec((tq, LANES), lambda b, h, i: (b * nq + i, h)),
        out_shape=jax.ShapeDtypeStruct((T, n_heads * LANES), BF16),
        scratch_shapes=[pltpu.VMEM((seq, LANES), BF16), pltpu.VMEM((seq, LANES), BF16), pltpu.VMEM((1, 1), F32)],
        compiler_params=_cparams("arbitrary", "arbitrary", "arbitrary"),
        name="diff_attn",
    )(qkv, qkv, qkv, lq1, lk1, lq2, lk2, subln_w)


def _split_bf16(x, n):
    parts = []
    for _ in range(n):
        p = x.astype(BF16)
        parts.append(p)
        x = x - p.astype(F32)
    return parts


def _dot(a, b):
    return jnp.dot(a.astype(BF16), b.astype(BF16), preferred_element_type=F32)


def _dot_exact_rhs(x, m_bf16, n):
    out = None
    for p in _split_bf16(x, n):
        t = jnp.dot(p, m_bf16, preferred_element_type=F32)
        out = t if out is None else out + t
    return out


def _dot_exact_lhs(m_bf16, x, n):
    out = None
    for p in _split_bf16(x, n):
        t = jnp.dot(m_bf16, p, preferred_element_type=F32)
        out = t if out is None else out + t
    return out


def _rwkv_kernel(r_ref, k_ref, v_ref, lo_ref, w0_ref, a0_ref, kk_ref, ka_ref, rk_ref, lw_ref, lb_ref,
                 w2_ref, a2_ref, g2_ref, o_ref, st_ref, *, rows):
    L = RWKV_CHUNK
    N = LANES // 2
    P = LANES

    @pl.when(pl.program_id(2) == 0)
    def _():
        st_ref[...] = jnp.zeros(st_ref.shape, F32)

    r = r_ref[...].astype(F32)
    k = k_ref[...].astype(F32)
    v = v_ref[...].astype(F32)
    lo = lo_ref[...]
    lo_wa = lo[:, :P]
    wl = jnp.dot(jnp.tanh(lo_wa.astype(F32)).astype(BF16), w2_ref[...], preferred_element_type=F32)
    al = jnp.dot(lo_wa, a2_ref[...], preferred_element_type=F32)
    g = jnp.dot(jax.nn.sigmoid(lo[:, P:].astype(F32)).astype(BF16), g2_ref[...], preferred_element_type=F32)

    wx = -(w0_ref[...] + wl)
    softplus = jnp.maximum(wx, 0.0) + jnp.log1p(jnp.exp(-jnp.abs(wx)))
    logdecay = -jnp.exp(-softplus - 0.5)
    a = jax.nn.sigmoid(a0_ref[...] + al)

    ri = lax.broadcasted_iota(jnp.int32, (P, P), 0)
    ci = lax.broadcasted_iota(jnp.int32, (P, P), 1)
    ones_bd = ((ri // N) == (ci // N)).astype(BF16)
    eye = (ri == ci).astype(F32)
    strict = ((ri % L) > (ci % L)).astype(F32)
    lower = ((ri % L) >= (ci % L)).astype(F32)
    ti = lax.broadcasted_iota(jnp.int32, (rows, rows), 0)
    tj = lax.broadcasted_iota(jnp.int32, (rows, rows), 1)
    tri_all = ((ti >= tj) & ((ti // L) == (tj // L))).astype(BF16)
    head0 = lax.broadcasted_iota(jnp.int32, (1, P), 1) < N

    kkr = k * kk_ref[...]
    norm = jnp.sqrt(_dot_exact_rhs(kkr * kkr, ones_bd, 2))
    kk = kkr / jnp.maximum(norm, 1e-12)
    kf = k * (1.0 + (a - 1.0) * ka_ref[...])
    avec = -kk
    bvec = kk * a

    def bd(x):
        x = x.astype(BF16)
        z = jnp.zeros_like(x)
        return jnp.concatenate([jnp.where(head0, x, z), jnp.where(head0, z, x)], axis=0)

    nc = rows // L
    chunks = range(nc)
    sls = [slice(c * L, (c + 1) * L) for c in chunks]
    cum_all = _dot_exact_lhs(tri_all, logdecay, 3)
    cum = [cum_all[s] for s in sls]
    tot = [cm[L - 1:L, :] for cm in cum]
    at = [bd(avec[s] * jnp.exp(cm - logdecay[s])) for s, cm in zip(sls, cum)]
    rt = [bd(r[s] * jnp.exp(cm)) for s, cm in zip(sls, cum)]
    inv = [jnp.exp(-cm) for cm in cum]
    bt = [bd(bvec[s] * iv) for s, iv in zip(sls, inv)]
    kt = [bd(kf[s] * iv) for s, iv in zip(sls, inv)]
    rest = [jnp.exp(t - cm) for t, cm in zip(tot, cum)]
    blkl_t = [jnp.concatenate([bd(bvec[s] * rs), bd(kf[s] * rs)], axis=1).astype(F32).T.astype(BF16)
              for s, rs in zip(sls, rest)]
    vd = [bd(v[s]) for s in sls]

    amat = [lax.dot_general(jnp.concatenate([at[c], rt[c]], axis=0), jnp.concatenate([bt[c], kt[c]], axis=0), _NT,
                            preferred_element_type=F32) for c in chunks]
    a_ab = [m[:P, :P] * strict for m in amat]
    a_ak = [(m[:P, P:] * strict).astype(BF16) for m in amat]
    a_rb = [(m[P:, :P] * lower).astype(BF16) for m in amat]
    a_rk = [(m[P:, P:] * lower).astype(BF16) for m in amat]

    tinv = [eye + n for n in a_ab]
    pw = [_dot(n, n) for n in a_ab]
    for _ in range(int(math.log2(L)) - 2):
        both = [_dot(pw[c], jnp.concatenate([pw[c], tinv[c]], axis=1)) for c in chunks]
        pw = [b[:, :P] for b in both]
        tinv = [tinv[c] + both[c][:, P:] for c in chunks]
    tinv = [tinv[c] + _dot(pw[c], tinv[c]) for c in chunks]

    avd = [_dot(jnp.concatenate([a_ak[c], a_rk[c], blkl_t[c][P:]], axis=0), vd[c]) for c in chunks]
    hat = [_dot(tinv[c], jnp.concatenate([at[c], avd[c][:P].astype(BF16)], axis=1)) for c in chunks]
    mix = [_dot(jnp.concatenate([a_rb[c], blkl_t[c][:P]], axis=0), hat[c]) for c in chunks]
    rhat = [rt[c].astype(F32) + mix[c][:P, :P] for c in chunks]
    ohat = [mix[c][:P, P:] + avd[c][P:2 * P] for c in chunks]
    mmat = [eye * jnp.exp(tot[c]) + mix[c][P:, :P] for c in chunks]
    cmat = [mix[c][P:, P:] + avd[c][2 * P:] for c in chunks]
    lhs = [jnp.concatenate([rhat[c], mmat[c]], axis=0).astype(BF16) for c in chunks]

    outs = []
    st = st_ref[...]
    for c in chunks:
        upd = _dot(lhs[c], st)
        od = upd[:P] + ohat[c]
        st = upd[P:] + cmat[c]
        outs.append(od[:L] + od[L:])
    st_ref[...] = st
    wkv = jnp.concatenate(outs, axis=0)

    inv_n = 1.0 / N
    mu = _dot_exact_rhs(wkv, ones_bd, 2) * inv_n
    xc = wkv - mu
    var = _dot_exact_rhs(xc * xc, ones_bd, 2) * inv_n
    o = xc * lax.rsqrt(var + LNX_EPS) * lw_ref[...] + lb_ref[...]
    bonus = _dot_exact_rhs(r * kf * rk_ref[...], ones_bd, 2) * v
    o_ref[...] = ((o + bonus) * g).astype(o_ref.dtype)


def _rwkv(cols, w0, a0, k_k, k_a, r_k, lnx_w, lnx_b, w2p, a2p, g2p, *, batch, seq, n_pairs, rows):
    T = cols.shape[0]
    nt = seq // rows
    lora_blk = (cols.shape[1] - 3 * n_pairs * LANES) // LANES
    row_map = lambda off: (lambda b, p, t: (b * nt + t, off + p))
    vec = pl.BlockSpec((1, LANES), lambda b, p, t: (0, p))
    return pl.pallas_call(
        functools.partial(_rwkv_kernel, rows=rows),
        grid=(batch, n_pairs, nt),
        in_specs=[
            pl.BlockSpec((rows, LANES), row_map(0)),
            pl.BlockSpec((rows, LANES), row_map(n_pairs)),
            pl.BlockSpec((rows, LANES), row_map(2 * n_pairs)),
            pl.BlockSpec((rows, lora_blk * LANES), lambda b, p, t: (b * nt + t, 3 * n_pairs // lora_blk)),
            vec, vec, vec, vec, vec, vec, vec,
            pl.BlockSpec((LANES, LANES), lambda b, p, t: (0, p)),
            pl.BlockSpec((LANES, LANES), lambda b, p, t: (0, p)),
            pl.BlockSpec((2 * LANES, LANES), lambda b, p, t: (0, p)),
        ],
        out_specs=pl.BlockSpec((rows, LANES), row_map(0)),
        out_shape=jax.ShapeDtypeStruct((T, n_pairs * LANES), BF16),
        scratch_shapes=[pltpu.VMEM((LANES, LANES), F32)],
        compiler_params=_cparams("arbitrary", "arbitrary", "arbitrary"),
        name="rwkv7",
    )(cols, cols, cols, cols, w0, a0, k_k, k_a, r_k, lnx_w, lnx_b, w2p, a2p, g2p)


def _out_proj_kernel(x_ref, oa_ref, or_ref, wa_ref, wr_ref, nw_ref, x1_ref, h2_ref):
    x1 = (x_ref[...] + jnp.dot(oa_ref[...], wa_ref[...], preferred_element_type=F32)
          + jnp.dot(or_ref[...], wr_ref[...], preferred_element_type=F32))
    x1_ref[...] = x1
    h2_ref[...] = _rms(x1, nw_ref[...]).astype(BF16)


def _out_proj(x2, oa, orw, wa, wr, nw, *, tm):
    T, D = x2.shape
    W = oa.shape[1]
    return pl.pallas_call(
        _out_proj_kernel,
        grid=(T // tm,),
        in_specs=[
            pl.BlockSpec((tm, D), lambda i: (i, 0)),
            pl.BlockSpec((tm, W), lambda i: (i, 0)),
            pl.BlockSpec((tm, W), lambda i: (i, 0)),
            pl.BlockSpec((W, D), lambda i: (0, 0)),
            pl.BlockSpec((W, D), lambda i: (0, 0)),
            pl.BlockSpec((1, D), lambda i: (0, 0)),
        ],
        out_specs=[pl.BlockSpec((tm, D), lambda i: (i, 0)), pl.BlockSpec((tm, D), lambda i: (i, 0))],
        out_shape=[jax.ShapeDtypeStruct((T, D), F32), jax.ShapeDtypeStruct((T, D), BF16)],
        compiler_params=_cparams("arbitrary"),
        name="out_proj",
    )(x2, oa, orw, wa, wr, nw)


def _topk_rows(cur, k):
    rows = []
    for _ in range(k):
        m = jnp.max(cur, axis=0, keepdims=True)
        rows.append(m)
        cur = jnp.where(cur == m, -jnp.inf, cur)
    return rows


_CANDS = [(a, b) for a in range(PEER_TOPK) for b in range(PEER_TOPK) if (a + 1) * (b + 1) <= PEER_TOPK]
_NCAND = -(-len(_CANDS) // 8) * 8


def _route_kernel(h2_ref, wq_ref, keys_ref, s1_ref, e1_ref, s2_ref, e2_ref, tau_ref, sc_ref, cand_ref,
                  *, n_heads):
    q = jnp.dot(h2_ref[...], wq_ref[...], preferred_element_type=F32).astype(BF16)
    for g in range(2 * n_heads):
        sc_ref[g] = lax.dot_general(keys_ref[g], q[:, g * LANES:(g + 1) * LANES], _NT,
                                    preferred_element_type=F32)
    cand_ref[...] = jnp.full(cand_ref.shape, -jnp.inf, F32)

    def per_head(h, _):
        s1 = sc_ref[2 * h]
        s2 = sc_ref[2 * h + 1]
        v1 = _topk_rows(s1, PEER_TOPK)
        v2 = _topk_rows(s2, PEER_TOPK)
        for n, (ia, ib) in enumerate(_CANDS):
            cand_ref[n:n + 1, :] = v1[ia] + v2[ib]
        cand = cand_ref[...]
        tau = _topk_rows(cand, PEER_TOPK)[-1]
        z = jnp.sum(jnp.where(cand >= tau, jnp.exp(cand - (v1[0] + v2[0])), 0.0), axis=0, keepdims=True)
        s1_ref[h] = s1
        e1_ref[h] = jnp.exp(s1 - v1[0])
        s2_ref[h] = s2
        e2_ref[h] = jnp.exp(s2 - v2[0]) / z
        tau_ref[pl.ds(h, 1), :] = tau
        return 0

    lax.fori_loop(0, n_heads, per_head, 0)


def _route(h2, wq, keys, *, tt, n_heads):
    T, D = h2.shape
    nk = keys.shape[1]
    big = jax.ShapeDtypeStruct((n_heads, nk, T), F32)
    blk = pl.BlockSpec((n_heads, nk, tt), lambda i: (0, 0, i))
    return pl.pallas_call(
        functools.partial(_route_kernel, n_heads=n_heads),
        grid=(T // tt,),
        in_specs=[
            pl.BlockSpec((tt, D), lambda i: (i, 0)),
            pl.BlockSpec(wq.shape, lambda i: (0, 0)),
            pl.BlockSpec(keys.shape, lambda i: (0, 0, 0)),
        ],
        out_specs=[blk, blk, blk, blk, pl.BlockSpec((n_heads, tt), lambda i: (0, i))],
        out_shape=[big, big, big, big, jax.ShapeDtypeStruct((n_heads, T), F32)],
        scratch_shapes=[pltpu.VMEM((2 * n_heads, nk, tt), F32), pltpu.VMEM((_NCAND, tt), F32)],
        compiler_params=_cparams("arbitrary"),
        name="peer_route",
    )(h2, wq, keys)


def _peer_kernel(h2_ref, down_ref, upt_ref, s1_ref, e1_ref, s2_ref, e2_ref, tau_ref, o_ref, act_ref,
                 *, n_heads, nk):
    @pl.when(pl.program_id(1) == 0)
    def _():
        o_ref[...] = jnp.zeros(o_ref.shape, F32)

    n_i1 = down_ref.shape[0] // nk
    per_chunk = n_i1 // PEER_Z_CHUNKS
    for q in range(PEER_Z_CHUNKS):
        zr = slice(q * per_chunk * nk, (q + 1) * per_chunk * nk)
        z = lax.dot_general(down_ref[zr, :], h2_ref[...], _NT, preferred_element_type=F32)
        for ic in range(per_chunk):
            ii = q * per_chunk + ic
            rs = slice(ii * nk, (ii + 1) * nk)
            gate = None
            for h in range(n_heads):
                total = s2_ref[h] + s1_ref[h, ii:ii + 1, :]
                val = jnp.where(total >= tau_ref[h:h + 1, :], e2_ref[h] * e1_ref[h, ii:ii + 1, :], 0.0)
                gate = val if gate is None else gate + val
            zz = z[ic * nk:(ic + 1) * nk]
            gelu = 0.5 * zz * (1.0 + lax.erf(zz * math.sqrt(0.5)))
            act_ref[rs, :] = (gelu * gate).astype(BF16)
    o_ref[...] += jnp.dot(upt_ref[...], act_ref[...], preferred_element_type=F32)


def _peer(h2, down, upt, s1, e1, s2, e2, tau, *, tt, et, n_heads, nk):
    T, D = h2.shape
    E = down.shape[0]
    i1_blk = et // nk
    row_blk = pl.BlockSpec((n_heads, i1_blk, tt), lambda i, j: (0, j, i))
    once = pl.Buffered(1)
    full_blk = pl.BlockSpec((n_heads, nk, tt), lambda i, j: (0, 0, i), pipeline_mode=once)
    return pl.pallas_call(
        functools.partial(_peer_kernel, n_heads=n_heads, nk=nk),
        grid=(T // tt, E // et),
        in_specs=[
            pl.BlockSpec((tt, D), lambda i, j: (i, 0), pipeline_mode=once),
            pl.BlockSpec((et, D), lambda i, j: (j, 0)),
            pl.BlockSpec((D, et), lambda i, j: (0, j)),
            row_blk, row_blk, full_blk, full_blk,
            pl.BlockSpec((n_heads, tt), lambda i, j: (0, i)),
        ],
        out_specs=pl.BlockSpec((D, tt), lambda i, j: (0, i)),
        out_shape=jax.ShapeDtypeStruct((D, T), F32),
        scratch_shapes=[pltpu.VMEM((et, tt), BF16)],
        compiler_params=_cparams("arbitrary", "arbitrary"),
        name="peer_experts",
    )(h2, down, upt, s1, e1, s2, e2, tau)


def _final_kernel(x1_ref, pt_ref, nw_ref, o_ref):
    o_ref[...] = _rms(x1_ref[...] + pt_ref[...].T, nw_ref[...])


def _final(x1, peer_t, nw, *, tm):
    T, D = x1.shape
    return pl.pallas_call(
        _final_kernel,
        grid=(T // tm,),
        in_specs=[
            pl.BlockSpec((tm, D), lambda i: (i, 0)),
            pl.BlockSpec((D, tm), lambda i: (0, i)),
            pl.BlockSpec((1, D), lambda i: (0, 0)),
        ],
        out_specs=pl.BlockSpec((tm, D), lambda i: (i, 0)),
        out_shape=jax.ShapeDtypeStruct((T, D), F32),
        compiler_params=_cparams("arbitrary"),
        name="final_norm",
    )(x1, peer_t, nw)


def _tiles(seq, tokens):
    return dict(
        proj_tm=min(512, seq),
        attn_tq=min(512, seq),
        rwkv_rows=min(512, seq),
        out_tm=min(512, tokens),
        route_tt=min(256, tokens),
        peer_tt=min(512, tokens),
        peer_et=2048,
        final_tm=min(256, tokens),
    )


def _pad_rows(w, rows_before, rows_total):
    return jnp.pad(w, ((rows_before, rows_total - rows_before - w.shape[0]), (0, 0)))


def kernel(x, norm1_w, w_in, tshift_mu, w0, w2, a0, a2, g2, k_k, k_a, r_k, lnx_w, lnx_b, lambda_q1, lambda_k1,
           lambda_q2, lambda_k2, subln_w, w_out, norm2_w, peer_w_query, peer_sub_keys, peer_down, peer_up,
           norm_f_w):
    B, S, D = x.shape
    T = B * S
    depth = norm1_w.shape[0]
    dh = lambda_q1.shape[-1]
    n_rheads, rhead = r_k.shape[1], r_k.shape[2]
    rw = n_rheads * rhead
    n_pairs = rw // LANES
    w_lora, a_lora, g_lora = w2.shape[1], a2.shape[1], g2.shape[1]
    da_cols = w_in.shape[2] - (3 * rw + w_lora + a_lora + g_lora)
    n_aheads = da_cols // (3 * 2 * dh)
    p_heads, nk = peer_sub_keys.shape[1], peer_sub_keys.shape[3]
    assert 2 * dh == LANES and 2 * rhead == LANES and rhead == RWKV_CHUNK and nk == LANES
    assert w_lora + a_lora == LANES and g_lora <= 2 * LANES
    t = _tiles(S, T)

    xt = x.reshape(T, D)
    for l in range(depth):
        row = lambda p: p[l].reshape(1, -1)
        lam_init = 0.8 - 0.6 * math.exp(-0.3 * l)

        w_attn = w_in[l][:, :da_cols].astype(BF16)
        rcols = w_in[l].shape[1] - da_cols
        rpad = 3 * rw + 3 * LANES - rcols
        w_rwkv = jnp.pad(w_in[l][:, da_cols:], ((0, 0), (0, rpad))).astype(BF16)
        mu = jnp.pad(row(tshift_mu), ((0, 0), (0, rpad)))
        w2p = _pad_rows(w2[l], 0, LANES).astype(BF16)
        a2p = _pad_rows(a2[l], w_lora, LANES).astype(BF16)
        g2p = _pad_rows(g2[l], 0, 2 * LANES).astype(BF16)

        qkv = _in_proj(xt, row(norm1_w), w_attn, None, seq=S, tm=t["proj_tm"], tn=da_cols // 3)
        rcol = _in_proj(xt, row(norm1_w), w_rwkv, mu, seq=S, tm=t["proj_tm"], tn=w_rwkv.shape[1] // 3)

        o_attn = _diff_attention(qkv, row(lambda_q1), row(lambda_k1), row(lambda_q2), row(lambda_k2),
                                 row(subln_w), batch=B, seq=S, n_heads=n_aheads, tq=t["attn_tq"],
                                 lam_init=lam_init)
        o_rwkv = _rwkv(rcol, row(w0), row(a0), row(k_k), row(k_a), row(r_k), row(lnx_w), row(lnx_b),
                       w2p, a2p, g2p, batch=B, seq=S, n_pairs=n_pairs, rows=t["rwkv_rows"])

        aw = o_attn.shape[1]
        x1, h2 = _out_proj(xt, o_attn, o_rwkv, w_out[l][:aw].astype(BF16), w_out[l][aw:].astype(BF16),
                           row(norm2_w), tm=t["out_tm"])

        keys = peer_sub_keys[l].reshape(2 * p_heads, nk, -1).astype(BF16)
        s1, e1, s2, e2, tau = _route(h2, peer_w_query[l].astype(BF16), keys, tt=t["route_tt"], n_heads=p_heads)
        peer_t = _peer(h2, peer_down[l].astype(BF16), peer_up[l].T.astype(BF16), s1, e1, s2, e2, tau,
                       tt=t["peer_tt"], et=t["peer_et"], n_heads=p_heads, nk=nk)
        if l + 1 < depth:
            xt = x1 + peer_t.T
        else:
            return _final(x1, peer_t, norm_f_w.reshape(1, -1), tm=t["final_tm"]).reshape(B, S, D)
```

```python
import functools
import math

import jax
import jax.numpy as jnp
from jax import lax
from jax.experimental import pallas as pl
from jax.experimental.pallas import tpu as pltpu

F32 = jnp.float32
BF16 = jnp.bfloat16

LANES = 128
RMS_EPS = 1e-6
LNX_EPS = 64e-5
PEER_TOPK = 16
RWKV_CHUNK = 64
PEER_Z_CHUNKS = 8
VMEM_LIMIT = 56 * 1024 * 1024
NEG = -1e30
SOFTMAX_UNDERFLOW = 110.0

_NT = (((1,), (1,)), ((), ()))


def _cparams(*sem):
    return pltpu.CompilerParams(dimension_semantics=sem, vmem_limit_bytes=VMEM_LIMIT)


def _rms(x, w):
    ms = jnp.mean(x * x, axis=-1, keepdims=True)
    return x * lax.rsqrt(ms + RMS_EPS) * w


def _proj_kernel(x_ref, nw_ref, w_ref, o_ref, h_ref):
    @pl.when(pl.program_id(1) == 0)
    def _():
        h_ref[...] = _rms(x_ref[...], nw_ref[...]).astype(BF16)

    o_ref[...] = jnp.dot(h_ref[...], w_ref[...], preferred_element_type=F32).astype(o_ref.dtype)


def _proj_shift_kernel(x_ref, nw_ref, w_ref, mu_ref, o_ref, h_ref, carry_ref, *, tiles_per_seq):
    i = pl.program_id(0)
    j = pl.program_id(1)

    @pl.when(j == 0)
    def _():
        h_ref[...] = _rms(x_ref[...], nw_ref[...]).astype(BF16)

    @pl.when(i % tiles_per_seq == 0)
    def _():
        carry_ref[j] = jnp.zeros(carry_ref.shape[1:], F32)

    p = jnp.dot(h_ref[...], w_ref[...], preferred_element_type=F32)
    tm = p.shape[0]
    rolled = pltpu.roll(p, 1, 0)
    row = lax.broadcasted_iota(jnp.int32, p.shape, 0)
    prev = jnp.where(row == 0, carry_ref[j][7:8, :], rolled)
    o_ref[...] = (p + (prev - p) * mu_ref[...]).astype(o_ref.dtype)
    carry_ref[j] = p[tm - 8:tm, :]


def _in_proj(x2, nw, w, mu, *, seq, tm, tn):
    T, D = x2.shape
    N = w.shape[1]
    grid = (T // tm, N // tn)
    in_specs = [
        pl.BlockSpec((tm, D), lambda i, j: (i, 0)),
        pl.BlockSpec((1, D), lambda i, j: (0, 0)),
        pl.BlockSpec((D, tn), lambda i, j: (0, j)),
    ]
    args = [x2, nw, w]
    scratch = [pltpu.VMEM((tm, D), BF16)]
    if mu is None:
        body = _proj_kernel
    else:
        body = functools.partial(_proj_shift_kernel, tiles_per_seq=seq // tm)
        in_specs.append(pl.BlockSpec((1, tn), lambda i, j: (0, j)))
        args.append(mu)
        scratch.append(pltpu.VMEM((N // tn, 8, tn), F32))
    return pl.pallas_call(
        body,
        grid=grid,
        in_specs=in_specs,
        out_specs=pl.BlockSpec((tm, tn), lambda i, j: (i, j)),
        out_shape=jax.ShapeDtypeStruct((T, N), BF16),
        scratch_shapes=scratch,
        compiler_params=_cparams("arbitrary", "arbitrary"),
        name="in_proj" if mu is None else "in_proj_shift",
    )(*args)


def _max_half_sqnorm(x, lane2, dh):
    sq = x * x
    lo = jnp.sum(jnp.where(lane2 < dh, sq, 0.0), axis=-1, keepdims=True)
    hi = jnp.sum(jnp.where(lane2 >= dh, sq, 0.0), axis=-1, keepdims=True)
    return jnp.max(jnp.maximum(lo, hi), axis=0, keepdims=True)


def _attn_kernel(q_ref, k_ref, v_ref, lq1_ref, lk1_ref, lq2_ref, lk2_ref, sw_ref, o_ref, kx1_ref, kx2_ref, knorm_ref,
                 *, tq, seq, n_heads, lam_init):
    h = pl.program_id(1)
    qi = pl.program_id(2)
    tk = tq
    dh = LANES // 2
    lane2 = lax.broadcasted_iota(jnp.int32, (tq, LANES), 1)

    @pl.when(qi == 0)
    def _():
        def build(c, knorm):
            rows = pl.ds(pl.multiple_of(c * tk, tk), tk)
            kb = k_ref[rows, :].astype(F32)
            pos = c * tk + lax.broadcasted_iota(jnp.int32, (tk, LANES), 0)
            hi = (pos // LANES).astype(F32)
            lo = (pos % LANES).astype(F32)
            kx1_ref[rows, :] = jnp.where(lane2 < dh, kb, jnp.where(lane2 == dh, hi, jnp.where(lane2 == dh + 1, lo, 0.0))
                                         ).astype(BF16)
            kx2_ref[rows, :] = jnp.where(lane2 >= dh, kb, jnp.where(lane2 == 0, hi, jnp.where(lane2 == 1, lo, 0.0))
                                         ).astype(BF16)
            return jnp.maximum(knorm, _max_half_sqnorm(kb, lane2, dh))

        knorm_ref[...] = lax.fori_loop(0, seq // tk, build, jnp.zeros((1, 1), F32))

    q = q_ref[...].astype(F32) * dh ** -0.5
    hv = jnp.full((tq, LANES), h + 1, jnp.int32).astype(F32)
    slope = jnp.exp2(hv * (-8.0 / n_heads))
    q1 = jnp.where(lane2 < dh, q, jnp.where(lane2 == dh, slope * LANES, jnp.where(lane2 == dh + 1, slope, 0.0))
                   ).astype(BF16)
    q2 = jnp.where(lane2 >= dh, q, jnp.where(lane2 == 0, slope * LANES, jnp.where(lane2 == 1, slope, 0.0))
                   ).astype(BF16)

    col = lax.broadcasted_iota(jnp.int32, (1, tk), 1)
    row = lax.broadcasted_iota(jnp.int32, (tq, 1), 0)

    def step(kc, carry, masked):
        rows = pl.ds(pl.multiple_of(kc * tk, tk), tk)
        vb = v_ref[rows, :]
        maps = (0, 1)
        ss = [lax.dot_general(qc, kx_ref[rows, :], _NT, preferred_element_type=F32)
              for qc, kx_ref in ((q1, kx1_ref), (q2, kx2_ref))]
        if masked:
            ss = [jnp.where(col <= row, s, NEG) for s in ss]
        ms, ls, accs = carry[0::3], carry[1::3], carry[2::3]
        mns = [jnp.maximum(ms[c], jnp.max(ss[c], axis=-1, keepdims=True)) for c in maps]
        ps = [jnp.exp(ss[c] - mns[c]) for c in maps]
        als = [jnp.exp(ms[c] - mns[c]) for c in maps]
        ls = [als[c] * ls[c] + jnp.sum(ps[c], axis=-1, keepdims=True) for c in maps]
        pvs = [jnp.dot(ps[c].astype(BF16), vb, preferred_element_type=F32) for c in maps]
        accs = [als[c] * accs[c] + pvs[c] for c in maps]
        return (mns[0], ls[0], accs[0], mns[1], ls[1], accs[1])

    init = (jnp.full((tq, 1), NEG, F32), jnp.zeros((tq, 1), F32), jnp.zeros((tq, LANES), F32)) * 2
    bound = jnp.sqrt(_max_half_sqnorm(q, lane2, dh) * knorm_ref[...])
    reach = (SOFTMAX_UNDERFLOW + 2.0 * bound) / slope[:1, :1]
    q0 = jnp.full((1, 1), qi * tq, jnp.int32).astype(F32)
    first = jnp.clip(jnp.floor((q0 + 1.0 - reach) / tk), 0.0, q0 / tq)
    carry = lax.fori_loop(jnp.max(first.astype(jnp.int32)), qi, lambda kc, c: step(kc, c, False), init)
    m1, l1, a1, m2, l2, a2 = step(qi, carry, True)

    lam = (jnp.exp(jnp.sum(lq1_ref[...] * lk1_ref[...], axis=-1, keepdims=True))
           - jnp.exp(jnp.sum(lq2_ref[...] * lk2_ref[...], axis=-1, keepdims=True)) + lam_init)
    o = a1 / l1 - lam * (a2 / l2)
    o_ref[...] = (_rms(o, sw_ref[...]) * (1.0 - lam_init)).astype(o_ref.dtype)


def _diff_attention(qkv, lq1, lk1, lq2, lk2, subln_w, *, batch, seq, n_heads, tq, lam_init):
    T = qkv.shape[0]
    nq = seq // tq
    assert 8 % n_heads == 0 and seq <= 256 * LANES
    vec = lambda n: pl.BlockSpec((1, n), lambda b, h, i: (0, 0))
    return pl.pallas_call(
        functools.partial(_attn_kernel, tq=tq, seq=seq, n_heads=n_heads, lam_init=lam_init),
        grid=(batch, n_heads, nq),
        in_specs=[
            pl.BlockSpec((tq, LANES), lambda b, h, i: (b * nq + i, h)),
            pl.BlockSpec((seq, LANES), lambda b, h, i: (b, n_heads + h)),
            pl.BlockSpec((seq, LANES), lambda b, h, i: (b, 2 * n_heads + h)),
            vec(lq1.shape[1]), vec(lq1.shape[1]), vec(lq1.shape[1]), vec(lq1.shape[1]),
            vec(LANES),
        ],
        out_specs=pl.BlockSpec((tq, LANES), lambda b, h, i: (b * nq + i, h)),
        out_shape=jax.ShapeDtypeStruct((T, n_heads * LANES), BF16),
        scratch_shapes=[pltpu.VMEM((seq, LANES), BF16), pltpu.VMEM((seq, LANES), BF16), pltpu.VMEM((1, 1), F32)],
        compiler_params=_cparams("arbitrary", "arbitrary", "arbitrary"),
        name="diff_attn",
    )(qkv, qkv, qkv, lq1, lk1, lq2, lk2, subln_w)


def _split_bf16(x, n):
    parts = []
    for _ in range(n):
        p = x.astype(BF16)
        parts.append(p)
        x = x - p.astype(F32)
    return parts


def _dot(a, b):
    return jnp.dot(a.astype(BF16), b.astype(BF16), preferred_element_type=F32)


def _dot_exact_rhs(x, m_bf16, n):
    out = None
    for p in _split_bf16(x, n):
        t = jnp.dot(p, m_bf16, preferred_element_type=F32)
        out = t if out is None else out + t
    return out


def _dot_exact_lhs(m_bf16, x, n):
    out = None
    for p in _split_bf16(x, n):
        t = jnp.dot(m_bf16, p, preferred_element_type=F32)
        out = t if out is None else out + t
    return out


def _rwkv_kernel(r_ref, k_ref, v_ref, lo_ref, w0_ref, a0_ref, kk_ref, ka_ref, rk_ref, lw_ref, lb_ref,
                 w2_ref, a2_ref, g2_ref, o_ref, st_ref, *, rows):
    L = RWKV_CHUNK
    N = LANES // 2
    P = LANES

    @pl.when(pl.program_id(2) == 0)
    def _():
        st_ref[...] = jnp.zeros(st_ref.shape, F32)

    r = r_ref[...].astype(F32)
    k = k_ref[...].astype(F32)
    v = v_ref[...].astype(F32)
    lo = lo_ref[...]
    lo_wa = lo[:, :P]
    wl = jnp.dot(jnp.tanh(lo_wa.astype(F32)).astype(BF16), w2_ref[...], preferred_element_type=F32)
    al = jnp.dot(lo_wa, a2_ref[...], preferred_element_type=F32)
    g = jnp.dot(jax.nn.sigmoid(lo[:, P:].astype(F32)).astype(BF16), g2_ref[...], preferred_element_type=F32)

    wx = -(w0_ref[...] + wl)
    softplus = jnp.maximum(wx, 0.0) + jnp.log1p(jnp.exp(-jnp.abs(wx)))
    logdecay = -jnp.exp(-softplus - 0.5)
    a = jax.nn.sigmoid(a0_ref[...] + al)

    ri = lax.broadcasted_iota(jnp.int32, (P, P), 0)
    ci = lax.broadcasted_iota(jnp.int32, (P, P), 1)
    ones_bd = ((ri // N) == (ci // N)).astype(BF16)
    eye = (ri == ci).astype(F32)
    strict = ((ri % L) > (ci % L)).astype(F32)
    lower = ((ri % L) >= (ci % L)).astype(F32)
    ti = lax.broadcasted_iota(jnp.int32, (rows, rows), 0)
    tj = lax.broadcasted_iota(jnp.int32, (rows, rows), 1)
    tri_all = ((ti >= tj) & ((ti // L) == (tj // L))).astype(BF16)
    head0 = lax.broadcasted_iota(jnp.int32, (1, P), 1) < N

    kkr = k * kk_ref[...]
    norm = jnp.sqrt(_dot_exact_rhs(kkr * kkr, ones_bd, 2))
    kk = kkr / jnp.maximum(norm, 1e-12)
    kf = k * (1.0 + (a - 1.0) * ka_ref[...])
    avec = -kk
    bvec = kk * a

    def bd(x):
        x = x.astype(BF16)
        z = jnp.zeros_like(x)
        return jnp.concatenate([jnp.where(head0, x, z), jnp.where(head0, z, x)], axis=0)

    nc = rows // L
    chunks = range(nc)
    sls = [slice(c * L, (c + 1) * L) for c in chunks]
    cum_all = _dot_exact_lhs(tri_all, logdecay, 3)
    cum = [cum_all[s] for s in sls]
    tot = [cm[L - 1:L, :] for cm in cum]
    at = [bd(avec[s] * jnp.exp(cm - logdecay[s])) for s, cm in zip(sls, cum)]
    rt = [bd(r[s] * jnp.exp(cm)) for s, cm in zip(sls, cum)]
    inv = [jnp.exp(-cm) for cm in cum]
    bt = [bd(bvec[s] * iv) for s, iv in zip(sls, inv)]
    kt = [bd(kf[s] * iv) for s, iv in zip(sls, inv)]
    rest = [jnp.exp(t - cm) for t, cm in zip(tot, cum)]
    blkl_t = [jnp.concatenate([bd(bvec[s] * rs), bd(kf[s] * rs)], axis=1).astype(F32).T.astype(BF16)
              for s, rs in zip(sls, rest)]
    vd = [bd(v[s]) for s in sls]

    amat = [lax.dot_general(jnp.concatenate([at[c], rt[c]], axis=0), jnp.concatenate([bt[c], kt[c]], axis=0), _NT,
                            preferred_element_type=F32) for c in chunks]
    a_ab = [m[:P, :P] * strict for m in amat]
    a_ak = [(m[:P, P:] * strict).astype(BF16) for m in amat]
    a_rb = [(m[P:, :P] * lower).astype(BF16) for m in amat]
    a_rk = [(m[P:, P:] * lower).astype(BF16) for m in amat]

    tinv = [eye + n for n in a_ab]
    pw = [_dot(n, n) for n in a_ab]
    for _ in range(int(math.log2(L)) - 2):
        both = [_dot(pw[c], jnp.concatenate([pw[c], tinv[c]], axis=1)) for c in chunks]
        pw = [b[:, :P] for b in both]
        tinv = [tinv[c] + both[c][:, P:] for c in chunks]
    tinv = [tinv[c] + _dot(pw[c], tinv[c]) for c in chunks]

    avd = [_dot(jnp.concatenate([a_ak[c], a_rk[c], blkl_t[c][P:]], axis=0), vd[c]) for c in chunks]
    hat = [_dot(tinv[c], jnp.concatenate([at[c], avd[c][:P].astype(BF16)], axis=1)) for c in chunks]
    mix = [_dot(jnp.concatenate([a_rb[c], blkl_t[c][:P]], axis=0), hat[c]) for c in chunks]
    rhat = [rt[c].astype(F32) + mix[c][:P, :P] for c in chunks]
    ohat = [mix[c][:P, P:] + avd[c][P:2 * P] for c in chunks]
    mmat = [eye * jnp.exp(tot[c]) + mix[c][P:, :P] for c in chunks]
    cmat = [mix[c][P:, P:] + avd[c][2 * P:] for c in chunks]
    lhs = [jnp.concatenate([rhat[c], mmat[c]], axis=0).astype(BF16) for c in chunks]

    outs = []
    st = st_ref[...]
    for c in chunks:
        upd = _dot(lhs[c], st)
        od = upd[:P] + ohat[c]
        st = upd[P:] + cmat[c]
        outs.append(od[:L] + od[L:])
    st_ref[...] = st
    wkv = jnp.concatenate(outs, axis=0)

    inv_n = 1.0 / N
    mu = _dot_exact_rhs(wkv, ones_bd, 2) * inv_n
    xc = wkv - mu
    var = _dot_exact_rhs(xc * xc, ones_bd, 2) * inv_n
    o = xc * lax.rsqrt(var + LNX_EPS) * lw_ref[...] + lb_ref[...]
    bonus = _dot_exact_rhs(r * kf * rk_ref[...], ones_bd, 2) * v
    o_ref[...] = ((o + bonus) * g).astype(o_ref.dtype)


def _rwkv(cols, w0, a0, k_k, k_a, r_k, lnx_w, lnx_b, w2p, a2p, g2p, *, batch, seq, n_pairs, rows):
    T = cols.shape[0]
    nt = seq // rows
    lora_blk = (cols.shape[1] - 3 * n_pairs * LANES) // LANES
    row_map = lambda off: (lambda b, p, t: (b * nt + t, off + p))
    vec = pl.BlockSpec((1, LANES), lambda b, p, t: (0, p))
    return pl.pallas_call(
        functools.partial(_rwkv_kernel, rows=rows),
        grid=(batch, n_pairs, nt),
        in_specs=[
            pl.BlockSpec((rows, LANES), row_map(0)),
            pl.BlockSpec((rows, LANES), row_map(n_pairs)),
            pl.BlockSpec((rows, LANES), row_map(2 * n_pairs)),
            pl.BlockSpec((rows, lora_blk * LANES), lambda b, p, t: (b * nt + t, 3 * n_pairs // lora_blk)),
            vec, vec, vec, vec, vec, vec, vec,
            pl.BlockSpec((LANES, LANES), lambda b, p, t: (0, p)),
            pl.BlockSpec((LANES, LANES), lambda b, p, t: (0, p)),
            pl.BlockSpec((2 * LANES, LANES), lambda b, p, t: (0, p)),
        ],
        out_specs=pl.BlockSpec((rows, LANES), row_map(0)),
        out_shape=jax.ShapeDtypeStruct((T, n_pairs * LANES), BF16),
        scratch_shapes=[pltpu.VMEM((LANES, LANES), F32)],
        compiler_params=_cparams("arbitrary", "arbitrary", "arbitrary"),
        name="rwkv7",
    )(cols, cols, cols, cols, w0, a0, k_k, k_a, r_k, lnx_w, lnx_b, w2p, a2p, g2p)


def _out_proj_kernel(x_ref, oa_ref, or_ref, wa_ref, wr_ref, nw_ref, x1_ref, h2_ref):
    x1 = (x_ref[...] + jnp.dot(oa_ref[...], wa_ref[...], preferred_element_type=F32)
          + jnp.dot(or_ref[...], wr_ref[...], preferred_element_type=F32))
    x1_ref[...] = x1
    h2_ref[...] = _rms(x1, nw_ref[...]).astype(BF16)


def _out_proj(x2, oa, orw, wa, wr, nw, *, tm):
    T, D = x2.shape
    W = oa.shape[1]
    return pl.pallas_call(
        _out_proj_kernel,
        grid=(T // tm,),
        in_specs=[
            pl.BlockSpec((tm, D), lambda i: (i, 0)),
            pl.BlockSpec((tm, W), lambda i: (i, 0)),
            pl.BlockSpec((tm, W), lambda i: (i, 0)),
            pl.BlockSpec((W, D), lambda i: (0, 0)),
            pl.BlockSpec((W, D), lambda i: (0, 0)),
            pl.BlockSpec((1, D), lambda i: (0, 0)),
        ],
        out_specs=[pl.BlockSpec((tm, D), lambda i: (i, 0)), pl.BlockSpec((tm, D), lambda i: (i, 0))],
        out_shape=[jax.ShapeDtypeStruct((T, D), F32), jax.ShapeDtypeStruct((T, D), BF16)],
        compiler_params=_cparams("arbitrary"),
        name="out_proj",
    )(x2, oa, orw, wa, wr, nw)


def _topk_rows(cur, k):
    rows = []
    for _ in range(k):
        m = jnp.max(cur, axis=0, keepdims=True)
        rows.append(m)
        cur = jnp.where(cur == m, -jnp.inf, cur)
    return rows


_NLIST = PEER_TOPK + 1
_CANDS = [(a, b) for a in range(_NLIST) for b in range(_NLIST) if (a + 1) * (b + 1) <= _NLIST]
_NCAND = -(-len(_CANDS) // 8) * 8


def _route_kernel(h2_ref, wq_ref, keys_ref, thr_ref, e1_ref, e2_ref, sc_ref, cand_ref, *, n_heads):
    q = jnp.dot(h2_ref[...], wq_ref[...], preferred_element_type=F32).astype(BF16)
    for g in range(2 * n_heads):
        sc_ref[g] = lax.dot_general(keys_ref[g], q[:, g * LANES:(g + 1) * LANES], _NT,
                                    preferred_element_type=F32)
    cand_ref[...] = jnp.full(cand_ref.shape, -jnp.inf, F32)

    def per_head(h, _):
        s1 = sc_ref[2 * h]
        s2 = sc_ref[2 * h + 1]
        v1 = _topk_rows(s1, _NLIST)
        v2 = _topk_rows(s2, _NLIST)
        for n, (ia, ib) in enumerate(_CANDS):
            cand_ref[n:n + 1, :] = v1[ia] + v2[ib]
        cand = cand_ref[...]
        top = _topk_rows(cand, _NLIST)
        tau = 0.5 * (top[PEER_TOPK - 1] + top[PEER_TOPK])
        inv_z = 1.0 / jnp.sum(jnp.where(cand > tau, jnp.exp(cand - (v1[0] + v2[0])), 0.0), axis=0, keepdims=True)
        e1_ref[h] = jnp.exp(s1 - v1[0])
        e2_ref[h] = jnp.exp(s2 - v2[0]) * inv_z
        thr_ref[h] = jnp.exp((tau - s1) - v2[0]) * inv_z
        return 0

    lax.fori_loop(0, n_heads, per_head, 0)


def _route(h2, wq, keys, *, tt, n_heads):
    T, D = h2.shape
    nk = keys.shape[1]
    big = jax.ShapeDtypeStruct((n_heads, nk, T), F32)
    blk = pl.BlockSpec((n_heads, nk, tt), lambda i: (0, 0, i))
    return pl.pallas_call(
        functools.partial(_route_kernel, n_heads=n_heads),
        grid=(T // tt,),
        in_specs=[
            pl.BlockSpec((tt, D), lambda i: (i, 0)),
            pl.BlockSpec(wq.shape, lambda i: (0, 0)),
            pl.BlockSpec(keys.shape, lambda i: (0, 0, 0)),
        ],
        out_specs=[blk, blk, blk],
        out_shape=[big, big, big],
        scratch_shapes=[pltpu.VMEM((2 * n_heads, nk, tt), F32), pltpu.VMEM((_NCAND, tt), F32)],
        compiler_params=_cparams("arbitrary"),
        name="peer_route",
    )(h2, wq, keys)


def _peer_kernel(h2_ref, down_ref, upt_ref, thr_ref, e1_ref, e2_ref, o_ref, act_ref, *, n_heads, nk):
    @pl.when(pl.program_id(1) == 0)
    def _():
        o_ref[...] = jnp.zeros(o_ref.shape, F32)

    n_i1 = down_ref.shape[0] // nk
    per_chunk = n_i1 // PEER_Z_CHUNKS
    for q in range(PEER_Z_CHUNKS):
        zr = slice(q * per_chunk * nk, (q + 1) * per_chunk * nk)
        z = lax.dot_general(down_ref[zr, :], h2_ref[...], _NT, preferred_element_type=F32)
        for ic in range(per_chunk):
            ii = q * per_chunk + ic
            rs = slice(ii * nk, (ii + 1) * nk)
            gate = None
            for h in range(n_heads):
                e2 = e2_ref[h]
                val = jnp.where(e2 >= thr_ref[h, ii:ii + 1, :], e2, 0.0) * e1_ref[h, ii:ii + 1, :]
                gate = val if gate is None else gate + val
            zz = z[ic * nk:(ic + 1) * nk]
            gelu = 0.5 * zz * (1.0 + lax.erf(zz * math.sqrt(0.5)))
            act_ref[rs, :] = (gelu * gate).astype(BF16)
    o_ref[...] += jnp.dot(upt_ref[...], act_ref[...], preferred_element_type=F32)


def _peer(h2, down, upt, thr, e1, e2, *, tt, et, n_heads, nk):
    T, D = h2.shape
    E = down.shape[0]
    i1_blk = et // nk
    row_blk = pl.BlockSpec((n_heads, i1_blk, tt), lambda i, j: (0, j, i))
    once = pl.Buffered(1)
    return pl.pallas_call(
        functools.partial(_peer_kernel, n_heads=n_heads, nk=nk),
        grid=(T // tt, E // et),
        in_specs=[
            pl.BlockSpec((tt, D), lambda i, j: (i, 0), pipeline_mode=once),
            pl.BlockSpec((et, D), lambda i, j: (j, 0)),
            pl.BlockSpec((D, et), lambda i, j: (0, j)),
            row_blk, row_blk,
            pl.BlockSpec((n_heads, nk, tt), lambda i, j: (0, 0, i), pipeline_mode=once),
        ],
        out_specs=pl.BlockSpec((D, tt), lambda i, j: (0, i)),
        out_shape=jax.ShapeDtypeStruct((D, T), F32),
        scratch_shapes=[pltpu.VMEM((et, tt), BF16)],
        compiler_params=_cparams("arbitrary", "arbitrary"),
        name="peer_experts",
    )(h2, down, upt, thr, e1, e2)


def _final_kernel(x1_ref, pt_ref, nw_ref, o_ref):
    o_ref[...] = _rms(x1_ref[...] + pt_ref[...].T, nw_ref[...])


def _final(x1, peer_t, nw, *, tm):
    T, D = x1.shape
    return pl.pallas_call(
        _final_kernel,
        grid=(T // tm,),
        in_specs=[
            pl.BlockSpec((tm, D), lambda i: (i, 0)),
            pl.BlockSpec((D, tm), lambda i: (0, i)),
            pl.BlockSpec((1, D), lambda i: (0, 0)),
        ],
        out_specs=pl.BlockSpec((tm, D), lambda i: (i, 0)),
        out_shape=jax.ShapeDtypeStruct((T, D), F32),
        compiler_params=_cparams("arbitrary"),
        name="final_norm",
    )(x1, peer_t, nw)


def _tiles(seq, tokens):
    return dict(
        proj_tm=min(512, seq),
        attn_tq=min(512, seq),
        rwkv_rows=min(512, seq),
        out_tm=min(512, tokens),
        route_tt=min(256, tokens),
        peer_tt=min(512, tokens),
        peer_et=2048,
        final_tm=min(256, tokens),
    )


def _pad_rows(w, rows_before, rows_total):
    return jnp.pad(w, ((rows_before, rows_total - rows_before - w.shape[0]), (0, 0)))


def kernel(x, norm1_w, w_in, tshift_mu, w0, w2, a0, a2, g2, k_k, k_a, r_k, lnx_w, lnx_b, lambda_q1, lambda_k1,
           lambda_q2, lambda_k2, subln_w, w_out, norm2_w, peer_w_query, peer_sub_keys, peer_down, peer_up,
           norm_f_w):
    B, S, D = x.shape
    T = B * S
    depth = norm1_w.shape[0]
    dh = lambda_q1.shape[-1]
    n_rheads, rhead = r_k.shape[1], r_k.shape[2]
    rw = n_rheads * rhead
    n_pairs = rw // LANES
    w_lora, a_lora, g_lora = w2.shape[1], a2.shape[1], g2.shape[1]
    da_cols = w_in.shape[2] - (3 * rw + w_lora + a_lora + g_lora)
    n_aheads = da_cols // (3 * 2 * dh)
    p_heads, nk = peer_sub_keys.shape[1], peer_sub_keys.shape[3]
    assert 2 * dh == LANES and 2 * rhead == LANES and rhead == RWKV_CHUNK and nk == LANES
    assert w_lora + a_lora == LANES and g_lora <= 2 * LANES
    t = _tiles(S, T)

    xt = x.reshape(T, D)
    for l in range(depth):
        row = lambda p: p[l].reshape(1, -1)
        lam_init = 0.8 - 0.6 * math.exp(-0.3 * l)

        w_attn = w_in[l][:, :da_cols].astype(BF16)
        rcols = w_in[l].shape[1] - da_cols
        rpad = 3 * rw + 3 * LANES - rcols
        w_rwkv = jnp.pad(w_in[l][:, da_cols:], ((0, 0), (0, rpad))).astype(BF16)
        mu = jnp.pad(row(tshift_mu), ((0, 0), (0, rpad)))
        w2p = _pad_rows(w2[l], 0, LANES).astype(BF16)
        a2p = _pad_rows(a2[l], w_lora, LANES).astype(BF16)
        g2p = _pad_rows(g2[l], 0, 2 * LANES).astype(BF16)

        qkv = _in_proj(xt, row(norm1_w), w_attn, None, seq=S, tm=t["proj_tm"], tn=da_cols // 3)
        rcol = _in_proj(xt, row(norm1_w), w_rwkv, mu, seq=S, tm=t["proj_tm"], tn=w_rwkv.shape[1] // 3)

        o_attn = _diff_attention(qkv, row(lambda_q1), row(lambda_k1), row(lambda_q2), row(lambda_k2),
                                 row(subln_w), batch=B, seq=S, n_heads=n_aheads, tq=t["attn_tq"],
                                 lam_init=lam_init)
        o_rwkv = _rwkv(rcol, row(w0), row(a0), row(k_k), row(k_a), row(r_k), row(lnx_w), row(lnx_b),
                       w2p, a2p, g2p, batch=B, seq=S, n_pairs=n_pairs, rows=t["rwkv_rows"])

        aw = o_attn.shape[1]
        x1, h2 = _out_proj(xt, o_attn, o_rwkv, w_out[l][:aw].astype(BF16), w_out[l][aw:].astype(BF16),
                           row(norm2_w), tm=t["out_tm"])

        keys = peer_sub_keys[l].reshape(2 * p_heads, nk, -1).astype(BF16)
        thr, e1, e2 = _route(h2, peer_w_query[l].astype(BF16), keys, tt=t["route_tt"], n_heads=p_heads)
        peer_t = _peer(h2, peer_down[l].astype(BF16), peer_up[l].T.astype(BF16), thr, e1, e2,
                       tt=t["peer_tt"], et=t["peer_et"], n_heads=p_heads, nk=nk)
        if l + 1 < depth:
            xt = x1 + peer_t.T
        else:
            return _final(x1, peer_t, norm_f_w.reshape(1, -1), tm=t["final_tm"]).reshape(B, S, D)
```

```python
import functools
import math

import jax
import jax.numpy as jnp
from jax import lax
from jax.experimental import pallas as pl
from jax.experimental.pallas import tpu as pltpu

F32 = jnp.float32
BF16 = jnp.bfloat16
FP8 = jnp.float8_e4m3fn

LANES = 128
RMS_EPS = 1e-6
LNX_EPS = 64e-5
PEER_TOPK = 16
RWKV_CHUNK = 64
PEER_DOWN_SCALE = 32.0
PEER_ACT_SCALE = 2.0 * PEER_DOWN_SCALE
PEER_Z_CHUNKS = 2
VMEM_LIMIT = 56 * 1024 * 1024
NEG = -1e30
SOFTMAX_UNDERFLOW = 110.0

_NT = (((1,), (1,)), ((), ()))


def _cparams(*sem):
    return pltpu.CompilerParams(dimension_semantics=sem, vmem_limit_bytes=VMEM_LIMIT)


def _rms(x, w):
    ms = jnp.mean(x * x, axis=-1, keepdims=True)
    return x * lax.rsqrt(ms + RMS_EPS) * w


def _proj_kernel(x_ref, nw_ref, w_ref, o_ref, h_ref):
    @pl.when(pl.program_id(1) == 0)
    def _():
        h_ref[...] = _rms(x_ref[...], nw_ref[...]).astype(BF16)

    o_ref[...] = jnp.dot(h_ref[...], w_ref[...], preferred_element_type=F32).astype(o_ref.dtype)


def _proj_shift_kernel(x_ref, nw_ref, w_ref, mu_ref, o_ref, h_ref, carry_ref, *, tiles_per_seq):
    i = pl.program_id(0)
    j = pl.program_id(1)

    @pl.when(j == 0)
    def _():
        h_ref[...] = _rms(x_ref[...], nw_ref[...]).astype(BF16)

    @pl.when(i % tiles_per_seq == 0)
    def _():
        carry_ref[j] = jnp.zeros(carry_ref.shape[1:], F32)

    p = jnp.dot(h_ref[...], w_ref[...], preferred_element_type=F32)
    tm = p.shape[0]
    rolled = pltpu.roll(p, 1, 0)
    row = lax.broadcasted_iota(jnp.int32, p.shape, 0)
    prev = jnp.where(row == 0, carry_ref[j][7:8, :], rolled)
    o_ref[...] = (p + (prev - p) * mu_ref[...]).astype(o_ref.dtype)
    carry_ref[j] = p[tm - 8:tm, :]


def _in_proj(x2, nw, w, mu, *, seq, tm, tn):
    T, D = x2.shape
    N = w.shape[1]
    grid = (T // tm, N // tn)
    in_specs = [
        pl.BlockSpec((tm, D), lambda i, j: (i, 0)),
        pl.BlockSpec((1, D), lambda i, j: (0, 0)),
        pl.BlockSpec((D, tn), lambda i, j: (0, j)),
    ]
    args = [x2, nw, w]
    scratch = [pltpu.VMEM((tm, D), BF16)]
    if mu is None:
        body = _proj_kernel
    else:
        body = functools.partial(_proj_shift_kernel, tiles_per_seq=seq // tm)
        in_specs.append(pl.BlockSpec((1, tn), lambda i, j: (0, j)))
        args.append(mu)
        scratch.append(pltpu.VMEM((N // tn, 8, tn), F32))
    return pl.pallas_call(
        body,
        grid=grid,
        in_specs=in_specs,
        out_specs=pl.BlockSpec((tm, tn), lambda i, j: (i, j)),
        out_shape=jax.ShapeDtypeStruct((T, N), BF16),
        scratch_shapes=scratch,
        compiler_params=_cparams("arbitrary", "arbitrary"),
        name="in_proj" if mu is None else "in_proj_shift",
    )(*args)


def _max_half_sqnorm(x, lane2, dh):
    sq = x * x
    lo = jnp.sum(jnp.where(lane2 < dh, sq, 0.0), axis=-1, keepdims=True)
    hi = jnp.sum(jnp.where(lane2 >= dh, sq, 0.0), axis=-1, keepdims=True)
    return jnp.max(jnp.maximum(lo, hi), axis=0, keepdims=True)


def _attn_kernel(q_ref, k_ref, v_ref, lq1_ref, lk1_ref, lq2_ref, lk2_ref, sw_ref, o_ref, kx1_ref, kx2_ref, knorm_ref,
                 *, tq, seq, n_heads, lam_init):
    h = pl.program_id(1)
    qi = pl.program_id(2)
    tk = tq
    dh = LANES // 2
    lane2 = lax.broadcasted_iota(jnp.int32, (tq, LANES), 1)

    @pl.when(qi == 0)
    def _():
        def build(c, knorm):
            rows = pl.ds(pl.multiple_of(c * tk, tk), tk)
            kb = k_ref[rows, :].astype(F32)
            pos = c * tk + lax.broadcasted_iota(jnp.int32, (tk, LANES), 0)
            hi = (pos // LANES).astype(F32)
            lo = (pos % LANES).astype(F32)
            kx1_ref[rows, :] = jnp.where(lane2 < dh, kb, jnp.where(lane2 == dh, hi, jnp.where(lane2 == dh + 1, lo, 0.0))
                                         ).astype(BF16)
            kx2_ref[rows, :] = jnp.where(lane2 >= dh, kb, jnp.where(lane2 == 0, hi, jnp.where(lane2 == 1, lo, 0.0))
                                         ).astype(BF16)
            return jnp.maximum(knorm, _max_half_sqnorm(kb, lane2, dh))

        knorm_ref[...] = lax.fori_loop(0, seq // tk, build, jnp.zeros((1, 1), F32))

    q = q_ref[...].astype(F32) * dh ** -0.5
    hv = jnp.full((tq, LANES), h + 1, jnp.int32).astype(F32)
    slope = jnp.exp2(hv * (-8.0 / n_heads))
    q1 = jnp.where(lane2 < dh, q, jnp.where(lane2 == dh, slope * LANES, jnp.where(lane2 == dh + 1, slope, 0.0))
                   ).astype(BF16)
    q2 = jnp.where(lane2 >= dh, q, jnp.where(lane2 == 0, slope * LANES, jnp.where(lane2 == 1, slope, 0.0))
                   ).astype(BF16)

    col = lax.broadcasted_iota(jnp.int32, (1, tk), 1)
    row = lax.broadcasted_iota(jnp.int32, (tq, 1), 0)

    def step(kc, carry, masked):
        rows = pl.ds(pl.multiple_of(kc * tk, tk), tk)
        vb = v_ref[rows, :]
        maps = (0, 1)
        ss = [lax.dot_general(qc, kx_ref[rows, :], _NT, preferred_element_type=F32)
              for qc, kx_ref in ((q1, kx1_ref), (q2, kx2_ref))]
        if masked:
            ss = [jnp.where(col <= row, s, NEG) for s in ss]
        ms, ls, accs = carry[0::3], carry[1::3], carry[2::3]
        mns = [jnp.maximum(ms[c], jnp.max(ss[c], axis=-1, keepdims=True)) for c in maps]
        ps = [jnp.exp(ss[c] - mns[c]) for c in maps]
        als = [jnp.exp(ms[c] - mns[c]) for c in maps]
        ls = [als[c] * ls[c] + jnp.sum(ps[c], axis=-1, keepdims=True) for c in maps]
        pvs = [jnp.dot(ps[c].astype(BF16), vb, preferred_element_type=F32) for c in maps]
        accs = [als[c] * accs[c] + pvs[c] for c in maps]
        return (mns[0], ls[0], accs[0], mns[1], ls[1], accs[1])

    init = (jnp.full((tq, 1), NEG, F32), jnp.zeros((tq, 1), F32), jnp.zeros((tq, LANES), F32)) * 2
    bound = jnp.sqrt(_max_half_sqnorm(q, lane2, dh) * knorm_ref[...])
    reach = (SOFTMAX_UNDERFLOW + 2.0 * bound) / slope[:1, :1]
    q0 = jnp.full((1, 1), qi * tq, jnp.int32).astype(F32)
    first = jnp.clip(jnp.floor((q0 + 1.0 - reach) / tk), 0.0, q0 / tq)
    carry = lax.fori_loop(jnp.max(first.astype(jnp.int32)), qi, lambda kc, c: step(kc, c, False), init)
    m1, l1, a1, m2, l2, a2 = step(qi, carry, True)

    lam = (jnp.exp(jnp.sum(lq1_ref[...] * lk1_ref[...], axis=-1, keepdims=True))
           - jnp.exp(jnp.sum(lq2_ref[...] * lk2_ref[...], axis=-1, keepdims=True)) + lam_init)
    o = a1 / l1 - lam * (a2 / l2)
    o_ref[...] = (_rms(o, sw_ref[...]) * (1.0 - lam_init)).astype(o_ref.dtype)


def _diff_attention(qkv, lq1, lk1, lq2, lk2, subln_w, *, batch, seq, n_heads, tq, lam_init):
    T = qkv.shape[0]
    nq = seq // tq
    assert 8 % n_heads == 0 and seq <= 256 * LANES
    vec = lambda n: pl.BlockSpec((1, n), lambda b, h, i: (0, 0))
    return pl.pallas_call(
        functools.partial(_attn_kernel, tq=tq, seq=seq, n_heads=n_heads, lam_init=lam_init),
        grid=(batch, n_heads, nq),
        in_specs=[
            pl.BlockSpec((tq, LANES), lambda b, h, i: (b * nq + i, h)),
            pl.BlockSpec((seq, LANES), lambda b, h, i: (b, n_heads + h)),
            pl.BlockSpec((seq, LANES), lambda b, h, i: (b, 2 * n_heads + h)),
            vec(lq1.shape[1]), vec(lq1.shape[1]), vec(lq1.shape[1]), vec(lq1.shape[1]),
            vec(LANES),
        ],
        out_specs=pl.BlockSpec((tq, LANES), lambda b, h, i: (b * nq + i, h)),
        out_shape=jax.ShapeDtypeStruct((T, n_heads * LANES), BF16),
        scratch_shapes=[pltpu.VMEM((seq, LANES), BF16), pltpu.VMEM((seq, LANES), BF16), pltpu.VMEM((1, 1), F32)],
        compiler_params=_cparams("arbitrary", "arbitrary", "arbitrary"),
        name="diff_attn",
    )(qkv, qkv, qkv, lq1, lk1, lq2, lk2, subln_w)


def _split_bf16(x, n):
    parts = []
    for _ in range(n):
        p = x.astype(BF16)
        parts.append(p)
        x = x - p.astype(F32)
    return parts


def _dot(a, b):
    return jnp.dot(a.astype(BF16), b.astype(BF16), preferred_element_type=F32)


def _dot_exact_rhs(x, m_bf16, n):
    out = None
    for p in _split_bf16(x, n):
        t = jnp.dot(p, m_bf16, preferred_element_type=F32)
        out = t if out is None else out + t
    return out


def _dot_exact_lhs(m_bf16, x, n):
    out = None
    for p in _split_bf16(x, n):
        t = jnp.dot(m_bf16, p, preferred_element_type=F32)
        out = t if out is None else out + t
    return out


def _rwkv_kernel(r_ref, k_ref, v_ref, lo_ref, w0_ref, a0_ref, kk_ref, ka_ref, rk_ref, lw_ref, lb_ref,
                 w2_ref, a2_ref, g2_ref, o_ref, st_ref, *, rows):
    L = RWKV_CHUNK
    N = LANES // 2
    P = LANES

    @pl.when(pl.program_id(2) == 0)
    def _():
        st_ref[...] = jnp.zeros(st_ref.shape, F32)

    r = r_ref[...].astype(F32)
    k = k_ref[...].astype(F32)
    v = v_ref[...].astype(F32)
    lo = lo_ref[...]
    lo_wa = lo[:, :P]
    wl = jnp.dot(jnp.tanh(lo_wa.astype(F32)).astype(BF16), w2_ref[...], preferred_element_type=F32)
    al = jnp.dot(lo_wa, a2_ref[...], preferred_element_type=F32)
    g = jnp.dot(jax.nn.sigmoid(lo[:, P:].astype(F32)).astype(BF16), g2_ref[...], preferred_element_type=F32)

    wx = -(w0_ref[...] + wl)
    softplus = jnp.maximum(wx, 0.0) + jnp.log1p(jnp.exp(-jnp.abs(wx)))
    logdecay = -jnp.exp(-softplus - 0.5)
    a = jax.nn.sigmoid(a0_ref[...] + al)

    ri = lax.broadcasted_iota(jnp.int32, (P, P), 0)
    ci = lax.broadcasted_iota(jnp.int32, (P, P), 1)
    ones_bd = ((ri // N) == (ci // N)).astype(BF16)
    eye = (ri == ci).astype(F32)
    strict = ((ri % L) > (ci % L)).astype(F32)
    lower = ((ri % L) >= (ci % L)).astype(F32)
    ti = lax.broadcasted_iota(jnp.int32, (rows, rows), 0)
    tj = lax.broadcasted_iota(jnp.int32, (rows, rows), 1)
    tri_all = ((ti >= tj) & ((ti // L) == (tj // L))).astype(BF16)
    head0 = lax.broadcasted_iota(jnp.int32, (1, P), 1) < N

    kkr = k * kk_ref[...]
    norm = jnp.sqrt(_dot_exact_rhs(kkr * kkr, ones_bd, 2))
    kk = kkr / jnp.maximum(norm, 1e-12)
    kf = k * (1.0 + (a - 1.0) * ka_ref[...])
    avec = -kk
    bvec = kk * a

    def bd(x):
        x = x.astype(BF16)
        z = jnp.zeros_like(x)
        return jnp.concatenate([jnp.where(head0, x, z), jnp.where(head0, z, x)], axis=0)

    nc = rows // L
    chunks = range(nc)
    sls = [slice(c * L, (c + 1) * L) for c in chunks]
    cum_all = _dot_exact_lhs(tri_all, logdecay, 3)
    cum = [cum_all[s] for s in sls]
    tot = [cm[L - 1:L, :] for cm in cum]
    at = [bd(avec[s] * jnp.exp(cm - logdecay[s])) for s, cm in zip(sls, cum)]
    rt = [bd(r[s] * jnp.exp(cm)) for s, cm in zip(sls, cum)]
    inv = [jnp.exp(-cm) for cm in cum]
    bt = [bd(bvec[s] * iv) for s, iv in zip(sls, inv)]
    kt = [bd(kf[s] * iv) for s, iv in zip(sls, inv)]
    rest = [jnp.exp(t - cm) for t, cm in zip(tot, cum)]
    blkl_t = [jnp.concatenate([bd(bvec[s] * rs), bd(kf[s] * rs)], axis=1).astype(F32).T.astype(BF16)
              for s, rs in zip(sls, rest)]
    vd = [bd(v[s]) for s in sls]

    amat = [lax.dot_general(jnp.concatenate([at[c], rt[c]], axis=0), jnp.concatenate([bt[c], kt[c]], axis=0), _NT,
                            preferred_element_type=F32) for c in chunks]
    a_ab = [m[:P, :P] * strict for m in amat]
    a_ak = [(m[:P, P:] * strict).astype(BF16) for m in amat]
    a_rb = [(m[P:, :P] * lower).astype(BF16) for m in amat]
    a_rk = [(m[P:, P:] * lower).astype(BF16) for m in amat]

    tinv = [eye + n for n in a_ab]
    pw = [_dot(n, n) for n in a_ab]
    for _ in range(int(math.log2(L)) - 2):
        both = [_dot(pw[c], jnp.concatenate([pw[c], tinv[c]], axis=1)) for c in chunks]
        pw = [b[:, :P] for b in both]
        tinv = [tinv[c] + both[c][:, P:] for c in chunks]
    tinv = [tinv[c] + _dot(pw[c], tinv[c]) for c in chunks]

    avd = [_dot(jnp.concatenate([a_ak[c], a_rk[c], blkl_t[c][P:]], axis=0), vd[c]) for c in chunks]
    hat = [_dot(tinv[c], jnp.concatenate([at[c], avd[c][:P].astype(BF16)], axis=1)) for c in chunks]
    mix = [_dot(jnp.concatenate([a_rb[c], blkl_t[c][:P]], axis=0), hat[c]) for c in chunks]
    rhat = [rt[c].astype(F32) + mix[c][:P, :P] for c in chunks]
    ohat = [mix[c][:P, P:] + avd[c][P:2 * P] for c in chunks]
    mmat = [eye * jnp.exp(tot[c]) + mix[c][P:, :P] for c in chunks]
    cmat = [mix[c][P:, P:] + avd[c][2 * P:] for c in chunks]
    lhs = [jnp.concatenate([rhat[c], mmat[c]], axis=0).astype(BF16) for c in chunks]

    outs = []
    st = st_ref[...]
    for c in chunks:
        upd = _dot(lhs[c], st)
        od = upd[:P] + ohat[c]
        st = upd[P:] + cmat[c]
        outs.append(od[:L] + od[L:])
    st_ref[...] = st
    wkv = jnp.concatenate(outs, axis=0)

    inv_n = 1.0 / N
    mu = _dot_exact_rhs(wkv, ones_bd, 2) * inv_n
    xc = wkv - mu
    var = _dot_exact_rhs(xc * xc, ones_bd, 2) * inv_n
    o = xc * lax.rsqrt(var + LNX_EPS) * lw_ref[...] + lb_ref[...]
    bonus = _dot_exact_rhs(r * kf * rk_ref[...], ones_bd, 2) * v
    o_ref[...] = ((o + bonus) * g).astype(o_ref.dtype)


def _rwkv(cols, w0, a0, k_k, k_a, r_k, lnx_w, lnx_b, w2p, a2p, g2p, *, batch, seq, n_pairs, rows):
    T = cols.shape[0]
    nt = seq // rows
    lora_blk = (cols.shape[1] - 3 * n_pairs * LANES) // LANES
    row_map = lambda off: (lambda b, p, t: (b * nt + t, off + p))
    vec = pl.BlockSpec((1, LANES), lambda b, p, t: (0, p))
    return pl.pallas_call(
        functools.partial(_rwkv_kernel, rows=rows),
        grid=(batch, n_pairs, nt),
        in_specs=[
            pl.BlockSpec((rows, LANES), row_map(0)),
            pl.BlockSpec((rows, LANES), row_map(n_pairs)),
            pl.BlockSpec((rows, LANES), row_map(2 * n_pairs)),
            pl.BlockSpec((rows, lora_blk * LANES), lambda b, p, t: (b * nt + t, 3 * n_pairs // lora_blk)),
            vec, vec, vec, vec, vec, vec, vec,
            pl.BlockSpec((LANES, LANES), lambda b, p, t: (0, p)),
            pl.BlockSpec((LANES, LANES), lambda b, p, t: (0, p)),
            pl.BlockSpec((2 * LANES, LANES), lambda b, p, t: (0, p)),
        ],
        out_specs=pl.BlockSpec((rows, LANES), row_map(0)),
        out_shape=jax.ShapeDtypeStruct((T, n_pairs * LANES), BF16),
        scratch_shapes=[pltpu.VMEM((LANES, LANES), F32)],
        compiler_params=_cparams("arbitrary", "arbitrary", "arbitrary"),
        name="rwkv7",
    )(cols, cols, cols, cols, w0, a0, k_k, k_a, r_k, lnx_w, lnx_b, w2p, a2p, g2p)


def _out_proj_kernel(x_ref, oa_ref, or_ref, wa_ref, wr_ref, nw_ref, x1_ref, h2_ref):
    x1 = (x_ref[...] + jnp.dot(oa_ref[...], wa_ref[...], preferred_element_type=F32)
          + jnp.dot(or_ref[...], wr_ref[...], preferred_element_type=F32))
    x1_ref[...] = x1
    h2_ref[...] = _rms(x1, nw_ref[...]).astype(BF16)


def _out_proj(x2, oa, orw, wa, wr, nw, *, tm):
    T, D = x2.shape
    W = oa.shape[1]
    return pl.pallas_call(
        _out_proj_kernel,
        grid=(T // tm,),
        in_specs=[
            pl.BlockSpec((tm, D), lambda i: (i, 0)),
            pl.BlockSpec((tm, W), lambda i: (i, 0)),
            pl.BlockSpec((tm, W), lambda i: (i, 0)),
            pl.BlockSpec((W, D), lambda i: (0, 0)),
            pl.BlockSpec((W, D), lambda i: (0, 0)),
            pl.BlockSpec((1, D), lambda i: (0, 0)),
        ],
        out_specs=[pl.BlockSpec((tm, D), lambda i: (i, 0)), pl.BlockSpec((tm, D), lambda i: (i, 0))],
        out_shape=[jax.ShapeDtypeStruct((T, D), F32), jax.ShapeDtypeStruct((T, D), BF16)],
        compiler_params=_cparams("arbitrary"),
        name="out_proj",
    )(x2, oa, orw, wa, wr, nw)


def _topk_rows(curs, k):
    curs = list(curs)
    rows = [[] for _ in curs]
    for _ in range(k):
        ms = [jnp.max(cur, axis=0, keepdims=True) for cur in curs]
        curs = [jnp.where(cur == m, -jnp.inf, cur) for cur, m in zip(curs, ms)]
        for r, m in zip(rows, ms):
            r.append(m)
    return rows


_NLIST = PEER_TOPK + 1
_CANDS = [(a, b) for a in range(_NLIST) for b in range(_NLIST) if (a + 1) * (b + 1) <= _NLIST]
_NCAND = -(-len(_CANDS) // 8) * 8


def _route_kernel(h2_ref, wq_ref, keys_ref, thr_ref, e1_ref, e2_ref, sc_ref, cand_ref, *, n_heads):
    q = jnp.dot(h2_ref[...], wq_ref[...], preferred_element_type=F32).astype(BF16)
    for g in range(2 * n_heads):
        sc_ref[g] = lax.dot_general(keys_ref[g], q[:, g * LANES:(g + 1) * LANES], _NT,
                                    preferred_element_type=F32)
    cand_ref[...] = jnp.full(cand_ref.shape, -jnp.inf, F32)

    def per_head(h, _):
        s1 = sc_ref[2 * h]
        s2 = sc_ref[2 * h + 1]
        v1, v2 = _topk_rows((s1, s2), _NLIST)
        for n, (ia, ib) in enumerate(_CANDS):
            cand_ref[n:n + 1, :] = v1[ia] + v2[ib]
        cand = cand_ref[...]
        top, = _topk_rows((cand,), _NLIST)
        tau = 0.5 * (top[PEER_TOPK - 1] + top[PEER_TOPK])
        inv_z = 1.0 / jnp.sum(jnp.where(cand > tau, jnp.exp(cand - (v1[0] + v2[0])), 0.0), axis=0, keepdims=True)
        e1_ref[h] = jnp.exp(s1 - v1[0])
        e2_ref[h] = jnp.exp(s2 - v2[0]) * inv_z
        thr_ref[h] = jnp.exp((tau - s1) - v2[0]) * inv_z
        return 0

    lax.fori_loop(0, n_heads, per_head, 0)


def _route(h2, wq, keys, *, tt, n_heads):
    T, D = h2.shape
    nk = keys.shape[1]
    big = jax.ShapeDtypeStruct((n_heads, nk, T), F32)
    blk = pl.BlockSpec((n_heads, nk, tt), lambda i: (0, 0, i))
    return pl.pallas_call(
        functools.partial(_route_kernel, n_heads=n_heads),
        grid=(T // tt,),
        in_specs=[
            pl.BlockSpec((tt, D), lambda i: (i, 0)),
            pl.BlockSpec(wq.shape, lambda i: (0, 0)),
            pl.BlockSpec(keys.shape, lambda i: (0, 0, 0)),
        ],
        out_specs=[blk, blk, blk],
        out_shape=[big, big, big],
        scratch_shapes=[pltpu.VMEM((2 * n_heads, nk, tt), F32), pltpu.VMEM((_NCAND, tt), F32)],
        compiler_params=_cparams("arbitrary"),
        name="peer_route",
    )(h2, wq, keys)


def _peer_kernel(h2_ref, down_ref, upt_ref, thr_ref, e1_ref, e2_ref, o_ref, act_ref, *, n_heads, nk):
    @pl.when(pl.program_id(1) == 0)
    def _():
        o_ref[...] = jnp.zeros(o_ref.shape, F32)

    n_i1 = down_ref.shape[0] // nk
    per_chunk = n_i1 // PEER_Z_CHUNKS
    for q in range(PEER_Z_CHUNKS):
        zr = slice(q * per_chunk * nk, (q + 1) * per_chunk * nk)
        z = lax.dot_general(down_ref[zr, :], h2_ref[...], _NT, preferred_element_type=F32)
        for ic in range(per_chunk):
            ii = q * per_chunk + ic
            rs = slice(ii * nk, (ii + 1) * nk)
            gate = None
            for h in range(n_heads):
                e2 = e2_ref[h]
                val = jnp.where(e2 >= thr_ref[h, ii:ii + 1, :], e2, 0.0) * e1_ref[h, ii:ii + 1, :]
                gate = val if gate is None else gate + val
            zz = z[ic * nk:(ic + 1) * nk]
            gelu = zz * (1.0 + lax.erf(zz * (math.sqrt(0.5) / PEER_DOWN_SCALE)))
            act_ref[rs, :] = (gelu * gate).astype(FP8)
    o_ref[...] += jnp.dot(upt_ref[...], act_ref[...], preferred_element_type=F32)


def _peer(h2, down, upt, thr, e1, e2, *, tt, et, n_heads, nk):
    T, D = h2.shape
    E = down.shape[0]
    i1_blk = et // nk
    row_blk = pl.BlockSpec((n_heads, i1_blk, tt), lambda i, j: (0, j, i))
    once = pl.Buffered(1)
    return pl.pallas_call(
        functools.partial(_peer_kernel, n_heads=n_heads, nk=nk),
        grid=(T // tt, E // et),
        in_specs=[
            pl.BlockSpec((tt, D), lambda i, j: (i, 0), pipeline_mode=once),
            pl.BlockSpec((et, D), lambda i, j: (j, 0)),
            pl.BlockSpec((D, et), lambda i, j: (0, j)),
            row_blk, row_blk,
            pl.BlockSpec((n_heads, nk, tt), lambda i, j: (0, 0, i), pipeline_mode=once),
        ],
        out_specs=pl.BlockSpec((D, tt), lambda i, j: (0, i)),
        out_shape=jax.ShapeDtypeStruct((D, T), F32),
        scratch_shapes=[pltpu.VMEM((et, tt), FP8)],
        compiler_params=_cparams("arbitrary", "arbitrary"),
        name="peer_experts",
    )(h2, down, upt, thr, e1, e2)


def _final_kernel(x1_ref, pt_ref, nw_ref, o_ref):
    o_ref[...] = _rms(x1_ref[...] + pt_ref[...].T * (1.0 / PEER_ACT_SCALE), nw_ref[...])


def _final(x1, peer_t, nw, *, tm):
    T, D = x1.shape
    return pl.pallas_call(
        _final_kernel,
        grid=(T // tm,),
        in_specs=[
            pl.BlockSpec((tm, D), lambda i: (i, 0)),
            pl.BlockSpec((D, tm), lambda i: (0, i)),
            pl.BlockSpec((1, D), lambda i: (0, 0)),
        ],
        out_specs=pl.BlockSpec((tm, D), lambda i: (i, 0)),
        out_shape=jax.ShapeDtypeStruct((T, D), F32),
        compiler_params=_cparams("arbitrary"),
        name="final_norm",
    )(x1, peer_t, nw)


def _tiles(seq, tokens):
    return dict(
        proj_tm=min(512, seq),
        attn_tq=min(512, seq),
        rwkv_rows=min(512, seq),
        out_tm=min(512, tokens),
        route_tt=min(256, tokens),
        peer_tt=min(512, tokens),
        peer_et=2048,
        final_tm=min(256, tokens),
    )


def _pad_rows(w, rows_before, rows_total):
    return jnp.pad(w, ((rows_before, rows_total - rows_before - w.shape[0]), (0, 0)))


def kernel(x, norm1_w, w_in, tshift_mu, w0, w2, a0, a2, g2, k_k, k_a, r_k, lnx_w, lnx_b, lambda_q1, lambda_k1,
           lambda_q2, lambda_k2, subln_w, w_out, norm2_w, peer_w_query, peer_sub_keys, peer_down, peer_up,
           norm_f_w):
    B, S, D = x.shape
    T = B * S
    depth = norm1_w.shape[0]
    dh = lambda_q1.shape[-1]
    n_rheads, rhead = r_k.shape[1], r_k.shape[2]
    rw = n_rheads * rhead
    n_pairs = rw // LANES
    w_lora, a_lora, g_lora = w2.shape[1], a2.shape[1], g2.shape[1]
    da_cols = w_in.shape[2] - (3 * rw + w_lora + a_lora + g_lora)
    n_aheads = da_cols // (3 * 2 * dh)
    p_heads, nk = peer_sub_keys.shape[1], peer_sub_keys.shape[3]
    assert 2 * dh == LANES and 2 * rhead == LANES and rhead == RWKV_CHUNK and nk == LANES
    assert w_lora + a_lora == LANES and g_lora <= 2 * LANES
    t = _tiles(S, T)

    xt = x.reshape(T, D)
    for l in range(depth):
        row = lambda p: p[l].reshape(1, -1)
        lam_init = 0.8 - 0.6 * math.exp(-0.3 * l)

        w_attn = w_in[l][:, :da_cols].astype(BF16)
        rcols = w_in[l].shape[1] - da_cols
        rpad = 3 * rw + 3 * LANES - rcols
        w_rwkv = jnp.pad(w_in[l][:, da_cols:], ((0, 0), (0, rpad))).astype(BF16)
        mu = jnp.pad(row(tshift_mu), ((0, 0), (0, rpad)))
        w2p = _pad_rows(w2[l], 0, LANES).astype(BF16)
        a2p = _pad_rows(a2[l], w_lora, LANES).astype(BF16)
        g2p = _pad_rows(g2[l], 0, 2 * LANES).astype(BF16)

        qkv = _in_proj(xt, row(norm1_w), w_attn, None, seq=S, tm=t["proj_tm"], tn=da_cols // 3)
        rcol = _in_proj(xt, row(norm1_w), w_rwkv, mu, seq=S, tm=t["proj_tm"], tn=w_rwkv.shape[1] // 3)

        o_attn = _diff_attention(qkv, row(lambda_q1), row(lambda_k1), row(lambda_q2), row(lambda_k2),
                                 row(subln_w), batch=B, seq=S, n_heads=n_aheads, tq=t["attn_tq"],
                                 lam_init=lam_init)
        o_rwkv = _rwkv(rcol, row(w0), row(a0), row(k_k), row(k_a), row(r_k), row(lnx_w), row(lnx_b),
                       w2p, a2p, g2p, batch=B, seq=S, n_pairs=n_pairs, rows=t["rwkv_rows"])

        aw = o_attn.shape[1]
        x1, h2 = _out_proj(xt, o_attn, o_rwkv, w_out[l][:aw].astype(BF16), w_out[l][aw:].astype(BF16),
                           row(norm2_w), tm=t["out_tm"])

        keys = peer_sub_keys[l].reshape(2 * p_heads, nk, -1).astype(BF16)
        thr, e1, e2 = _route(h2, peer_w_query[l].astype(BF16), keys, tt=t["route_tt"], n_heads=p_heads)
        peer_t = _peer(h2.astype(FP8), (peer_down[l] * PEER_DOWN_SCALE).astype(FP8), peer_up[l].T.astype(FP8), thr, e1, e2,
                       tt=t["peer_tt"], et=t["peer_et"], n_heads=p_heads, nk=nk)
        if l + 1 < depth:
            xt = x1 + peer_t.T * (1.0 / PEER_ACT_SCALE)
        else:
            return _final(x1, peer_t, norm_f_w.reshape(1, -1), tm=t["final_tm"]).reshape(B, S, D)
```

```python
import functools
import math

import jax
import jax.numpy as jnp
from jax import lax
from jax.experimental import pallas as pl
from jax.experimental.pallas import tpu as pltpu

F32 = jnp.float32
BF16 = jnp.bfloat16
FP8 = jnp.float8_e4m3fn

LANES = 128
RMS_EPS = 1e-6
LNX_EPS = 64e-5
PEER_TOPK = 16
RWKV_CHUNK = 64
PEER_DOWN_SCALE = 32.0
PEER_ACT_SCALE = 2.0 * PEER_DOWN_SCALE
PEER_Z_CHUNKS = 2
PEER_UP_CHUNKS = 4
VMEM_LIMIT = 56 * 1024 * 1024
NEG = -1e30
SOFTMAX_UNDERFLOW = 110.0

_NT = (((1,), (1,)), ((), ()))


def _cparams(*sem):
    return pltpu.CompilerParams(dimension_semantics=sem, vmem_limit_bytes=VMEM_LIMIT)


def _rms(x, w):
    ms = jnp.mean(x * x, axis=-1, keepdims=True)
    return x * lax.rsqrt(ms + RMS_EPS) * w


def _proj_kernel(x_ref, nw_ref, w_ref, o_ref, h_ref):
    @pl.when(pl.program_id(1) == 0)
    def _():
        h_ref[...] = _rms(x_ref[...], nw_ref[...]).astype(BF16)

    o_ref[...] = jnp.dot(h_ref[...], w_ref[...], preferred_element_type=F32).astype(o_ref.dtype)


def _proj_shift_kernel(x_ref, nw_ref, w_ref, mu_ref, o_ref, h_ref, carry_ref, *, tiles_per_seq):
    i = pl.program_id(0)
    j = pl.program_id(1)

    @pl.when(j == 0)
    def _():
        h_ref[...] = _rms(x_ref[...], nw_ref[...]).astype(BF16)

    @pl.when(i % tiles_per_seq == 0)
    def _():
        carry_ref[j] = jnp.zeros(carry_ref.shape[1:], F32)

    p = jnp.dot(h_ref[...], w_ref[...], preferred_element_type=F32)
    tm = p.shape[0]
    rolled = pltpu.roll(p, 1, 0)
    row = lax.broadcasted_iota(jnp.int32, p.shape, 0)
    prev = jnp.where(row == 0, carry_ref[j][7:8, :], rolled)
    o_ref[...] = (p + (prev - p) * mu_ref[...]).astype(o_ref.dtype)
    carry_ref[j] = p[tm - 8:tm, :]


def _in_proj(x2, nw, w, mu, *, seq, tm, tn):
    T, D = x2.shape
    N = w.shape[1]
    grid = (T // tm, N // tn)
    in_specs = [
        pl.BlockSpec((tm, D), lambda i, j: (i, 0)),
        pl.BlockSpec((1, D), lambda i, j: (0, 0)),
        pl.BlockSpec((D, tn), lambda i, j: (0, j)),
    ]
    args = [x2, nw, w]
    scratch = [pltpu.VMEM((tm, D), BF16)]
    if mu is None:
        body = _proj_kernel
    else:
        body = functools.partial(_proj_shift_kernel, tiles_per_seq=seq // tm)
        in_specs.append(pl.BlockSpec((1, tn), lambda i, j: (0, j)))
        args.append(mu)
        scratch.append(pltpu.VMEM((N // tn, 8, tn), F32))
    return pl.pallas_call(
        body,
        grid=grid,
        in_specs=in_specs,
        out_specs=pl.BlockSpec((tm, tn), lambda i, j: (i, j)),
        out_shape=jax.ShapeDtypeStruct((T, N), BF16),
        scratch_shapes=scratch,
        compiler_params=_cparams("arbitrary", "arbitrary"),
        name="in_proj" if mu is None else "in_proj_shift",
    )(*args)


def _max_half_sqnorm(x, lane2, dh):
    sq = x * x
    lo = jnp.sum(jnp.where(lane2 < dh, sq, 0.0), axis=-1, keepdims=True)
    hi = jnp.sum(jnp.where(lane2 >= dh, sq, 0.0), axis=-1, keepdims=True)
    return jnp.max(jnp.maximum(lo, hi), axis=0, keepdims=True)


def _attn_kernel(q_ref, k_ref, v_ref, lq1_ref, lk1_ref, lq2_ref, lk2_ref, sw_ref, o_ref, kx1_ref, kx2_ref, knorm_ref,
                 *, tq, seq, n_heads, lam_init):
    h = pl.program_id(1)
    qi = pl.program_id(2)
    tk = tq
    dh = LANES // 2
    lane2 = lax.broadcasted_iota(jnp.int32, (tq, LANES), 1)

    @pl.when(qi == 0)
    def _():
        def build(c, knorm):
            rows = pl.ds(pl.multiple_of(c * tk, tk), tk)
            kb = k_ref[rows, :].astype(F32)
            pos = c * tk + lax.broadcasted_iota(jnp.int32, (tk, LANES), 0)
            hi = (pos // LANES).astype(F32)
            lo = (pos % LANES).astype(F32)
            kx1_ref[rows, :] = jnp.where(lane2 < dh, kb, jnp.where(lane2 == dh, hi, jnp.where(lane2 == dh + 1, lo, 0.0))
                                         ).astype(BF16)
            kx2_ref[rows, :] = jnp.where(lane2 >= dh, kb, jnp.where(lane2 == 0, hi, jnp.where(lane2 == 1, lo, 0.0))
                                         ).astype(BF16)
            return jnp.maximum(knorm, _max_half_sqnorm(kb, lane2, dh))

        knorm_ref[...] = lax.fori_loop(0, seq // tk, build, jnp.zeros((1, 1), F32))

    q = q_ref[...].astype(F32) * dh ** -0.5
    hv = jnp.full((tq, LANES), h + 1, jnp.int32).astype(F32)
    slope = jnp.exp2(hv * (-8.0 / n_heads))
    q1 = jnp.where(lane2 < dh, q, jnp.where(lane2 == dh, slope * LANES, jnp.where(lane2 == dh + 1, slope, 0.0))
                   ).astype(BF16)
    q2 = jnp.where(lane2 >= dh, q, jnp.where(lane2 == 0, slope * LANES, jnp.where(lane2 == 1, slope, 0.0))
                   ).astype(BF16)

    col = lax.broadcasted_iota(jnp.int32, (1, tk), 1)
    row = lax.broadcasted_iota(jnp.int32, (tq, 1), 0)

    def step(kc, carry, masked):
        rows = pl.ds(pl.multiple_of(kc * tk, tk), tk)
        vb = v_ref[rows, :]
        maps = (0, 1)
        ss = [lax.dot_general(qc, kx_ref[rows, :], _NT, preferred_element_type=F32)
              for qc, kx_ref in ((q1, kx1_ref), (q2, kx2_ref))]
        if masked:
            ss = [jnp.where(col <= row, s, NEG) for s in ss]
        ms, ls, accs = carry[0::3], carry[1::3], carry[2::3]
        mns = [jnp.maximum(ms[c], jnp.max(ss[c], axis=-1, keepdims=True)) for c in maps]
        ps = [jnp.exp(ss[c] - mns[c]) for c in maps]
        als = [jnp.exp(ms[c] - mns[c]) for c in maps]
        ls = [als[c] * ls[c] + jnp.sum(ps[c], axis=-1, keepdims=True) for c in maps]
        pvs = [jnp.dot(ps[c].astype(BF16), vb, preferred_element_type=F32) for c in maps]
        accs = [als[c] * accs[c] + pvs[c] for c in maps]
        return (mns[0], ls[0], accs[0], mns[1], ls[1], accs[1])

    init = (jnp.full((tq, 1), NEG, F32), jnp.zeros((tq, 1), F32), jnp.zeros((tq, LANES), F32)) * 2
    bound = jnp.sqrt(_max_half_sqnorm(q, lane2, dh) * knorm_ref[...])
    reach = (SOFTMAX_UNDERFLOW + 2.0 * bound) / slope[:1, :1]
    q0 = jnp.full((1, 1), qi * tq, jnp.int32).astype(F32)
    first = jnp.clip(jnp.floor((q0 + 1.0 - reach) / tk), 0.0, q0 / tq)
    carry = lax.fori_loop(jnp.max(first.astype(jnp.int32)), qi, lambda kc, c: step(kc, c, False), init)
    m1, l1, a1, m2, l2, a2 = step(qi, carry, True)

    lam = (jnp.exp(jnp.sum(lq1_ref[...] * lk1_ref[...], axis=-1, keepdims=True))
           - jnp.exp(jnp.sum(lq2_ref[...] * lk2_ref[...], axis=-1, keepdims=True)) + lam_init)
    o = a1 / l1 - lam * (a2 / l2)
    o_ref[...] = (_rms(o, sw_ref[...]) * (1.0 - lam_init)).astype(o_ref.dtype)


def _diff_attention(qkv, lq1, lk1, lq2, lk2, subln_w, *, batch, seq, n_heads, tq, lam_init):
    T = qkv.shape[0]
    nq = seq // tq
    assert 8 % n_heads == 0 and seq <= 256 * LANES
    vec = lambda n: pl.BlockSpec((1, n), lambda b, h, i: (0, 0))
    return pl.pallas_call(
        functools.partial(_attn_kernel, tq=tq, seq=seq, n_heads=n_heads, lam_init=lam_init),
        grid=(batch, n_heads, nq),
        in_specs=[
            pl.BlockSpec((tq, LANES), lambda b, h, i: (b * nq + i, h)),
            pl.BlockSpec((seq, LANES), lambda b, h, i: (b, n_heads + h)),
            pl.BlockSpec((seq, LANES), lambda b, h, i: (b, 2 * n_heads + h)),
            vec(lq1.shape[1]), vec(lq1.shape[1]), vec(lq1.shape[1]), vec(lq1.shape[1]),
            vec(LANES),
        ],
        out_specs=pl.BlockSpec((tq, LANES), lambda b, h, i: (b * nq + i, h)),
        out_shape=jax.ShapeDtypeStruct((T, n_heads * LANES), BF16),
        scratch_shapes=[pltpu.VMEM((seq, LANES), BF16), pltpu.VMEM((seq, LANES), BF16), pltpu.VMEM((1, 1), F32)],
        compiler_params=_cparams("arbitrary", "arbitrary", "arbitrary"),
        name="diff_attn",
    )(qkv, qkv, qkv, lq1, lk1, lq2, lk2, subln_w)


def _split_bf16(x, n):
    parts = []
    for _ in range(n):
        p = x.astype(BF16)
        parts.append(p)
        x = x - p.astype(F32)
    return parts


def _dot(a, b):
    return jnp.dot(a.astype(BF16), b.astype(BF16), preferred_element_type=F32)


def _dot_exact_rhs(x, m_bf16, n):
    out = None
    for p in _split_bf16(x, n):
        t = jnp.dot(p, m_bf16, preferred_element_type=F32)
        out = t if out is None else out + t
    return out


def _dot_exact_lhs(m_bf16, x, n):
    out = None
    for p in _split_bf16(x, n):
        t = jnp.dot(m_bf16, p, preferred_element_type=F32)
        out = t if out is None else out + t
    return out


def _rwkv_kernel(r_ref, k_ref, v_ref, lo_ref, w0_ref, a0_ref, kk_ref, ka_ref, rk_ref, lw_ref, lb_ref,
                 w2_ref, a2_ref, g2_ref, o_ref, st_ref, *, rows):
    L = RWKV_CHUNK
    N = LANES // 2
    P = LANES

    @pl.when(pl.program_id(2) == 0)
    def _():
        st_ref[...] = jnp.zeros(st_ref.shape, F32)

    r = r_ref[...].astype(F32)
    k = k_ref[...].astype(F32)
    v = v_ref[...].astype(F32)
    lo = lo_ref[...]
    lo_wa = lo[:, :P]
    wl = jnp.dot(jnp.tanh(lo_wa.astype(F32)).astype(BF16), w2_ref[...], preferred_element_type=F32)
    al = jnp.dot(lo_wa, a2_ref[...], preferred_element_type=F32)
    g = jnp.dot(jax.nn.sigmoid(lo[:, P:].astype(F32)).astype(BF16), g2_ref[...], preferred_element_type=F32)

    wx = -(w0_ref[...] + wl)
    softplus = jnp.maximum(wx, 0.0) + jnp.log1p(jnp.exp(-jnp.abs(wx)))
    logdecay = -jnp.exp(-softplus - 0.5)
    a = jax.nn.sigmoid(a0_ref[...] + al)

    ri = lax.broadcasted_iota(jnp.int32, (P, P), 0)
    ci = lax.broadcasted_iota(jnp.int32, (P, P), 1)
    ones_bd = ((ri // N) == (ci // N)).astype(BF16)
    eye = (ri == ci).astype(F32)
    strict = ((ri % L) > (ci % L)).astype(F32)
    lower = ((ri % L) >= (ci % L)).astype(F32)
    ti = lax.broadcasted_iota(jnp.int32, (rows, rows), 0)
    tj = lax.broadcasted_iota(jnp.int32, (rows, rows), 1)
    tri_all = ((ti >= tj) & ((ti // L) == (tj // L))).astype(BF16)
    head0 = lax.broadcasted_iota(jnp.int32, (1, P), 1) < N

    kkr = k * kk_ref[...]
    norm = jnp.sqrt(_dot_exact_rhs(kkr * kkr, ones_bd, 2))
    kk = kkr / jnp.maximum(norm, 1e-12)
    kf = k * (1.0 + (a - 1.0) * ka_ref[...])
    avec = -kk
    bvec = kk * a

    def bd(x):
        x = x.astype(BF16)
        z = jnp.zeros_like(x)
        return jnp.concatenate([jnp.where(head0, x, z), jnp.where(head0, z, x)], axis=0)

    nc = rows // L
    chunks = range(nc)
    sls = [slice(c * L, (c + 1) * L) for c in chunks]
    cum_all = _dot_exact_lhs(tri_all, logdecay, 3)
    cum = [cum_all[s] for s in sls]
    tot = [cm[L - 1:L, :] for cm in cum]
    at = [bd(avec[s] * jnp.exp(cm - logdecay[s])) for s, cm in zip(sls, cum)]
    rt = [bd(r[s] * jnp.exp(cm)) for s, cm in zip(sls, cum)]
    inv = [jnp.exp(-cm) for cm in cum]
    bt = [bd(bvec[s] * iv) for s, iv in zip(sls, inv)]
    kt = [bd(kf[s] * iv) for s, iv in zip(sls, inv)]
    rest = [jnp.exp(t - cm) for t, cm in zip(tot, cum)]
    blkl_t = [jnp.concatenate([bd(bvec[s] * rs), bd(kf[s] * rs)], axis=1).astype(F32).T.astype(BF16)
              for s, rs in zip(sls, rest)]
    vd = [bd(v[s]) for s in sls]

    amat = [lax.dot_general(jnp.concatenate([at[c], rt[c]], axis=0), jnp.concatenate([bt[c], kt[c]], axis=0), _NT,
                            preferred_element_type=F32) for c in chunks]
    a_ab = [m[:P, :P] * strict for m in amat]
    a_ak = [(m[:P, P:] * strict).astype(BF16) for m in amat]
    a_rb = [(m[P:, :P] * lower).astype(BF16) for m in amat]
    a_rk = [(m[P:, P:] * lower).astype(BF16) for m in amat]

    tinv = [eye + n for n in a_ab]
    pw = [_dot(n, n) for n in a_ab]
    for _ in range(int(math.log2(L)) - 2):
        both = [_dot(pw[c], jnp.concatenate([pw[c], tinv[c]], axis=1)) for c in chunks]
        pw = [b[:, :P] for b in both]
        tinv = [tinv[c] + both[c][:, P:] for c in chunks]
    tinv = [tinv[c] + _dot(pw[c], tinv[c]) for c in chunks]

    avd = [_dot(jnp.concatenate([a_ak[c], a_rk[c], blkl_t[c][P:]], axis=0), vd[c]) for c in chunks]
    hat = [_dot(tinv[c], jnp.concatenate([at[c], avd[c][:P].astype(BF16)], axis=1)) for c in chunks]
    mix = [_dot(jnp.concatenate([a_rb[c], blkl_t[c][:P]], axis=0), hat[c]) for c in chunks]
    rhat = [rt[c].astype(F32) + mix[c][:P, :P] for c in chunks]
    ohat = [mix[c][:P, P:] + avd[c][P:2 * P] for c in chunks]
    mmat = [eye * jnp.exp(tot[c]) + mix[c][P:, :P] for c in chunks]
    cmat = [mix[c][P:, P:] + avd[c][2 * P:] for c in chunks]
    lhs = [jnp.concatenate([rhat[c], mmat[c]], axis=0).astype(BF16) for c in chunks]

    outs = []
    st = st_ref[...]
    for c in chunks:
        upd = _dot(lhs[c], st)
        od = upd[:P] + ohat[c]
        st = upd[P:] + cmat[c]
        outs.append(od[:L] + od[L:])
    st_ref[...] = st
    wkv = jnp.concatenate(outs, axis=0)

    inv_n = 1.0 / N
    mu = _dot_exact_rhs(wkv, ones_bd, 2) * inv_n
    xc = wkv - mu
    var = _dot_exact_rhs(xc * xc, ones_bd, 2) * inv_n
    o = xc * lax.rsqrt(var + LNX_EPS) * lw_ref[...] + lb_ref[...]
    bonus = _dot_exact_rhs(r * kf * rk_ref[...], ones_bd, 2) * v
    o_ref[...] = ((o + bonus) * g).astype(o_ref.dtype)


def _rwkv(cols, w0, a0, k_k, k_a, r_k, lnx_w, lnx_b, w2p, a2p, g2p, *, batch, seq, n_pairs, rows):
    T = cols.shape[0]
    nt = seq // rows
    lora_blk = (cols.shape[1] - 3 * n_pairs * LANES) // LANES
    row_map = lambda off: (lambda b, p, t: (b * nt + t, off + p))
    vec = pl.BlockSpec((1, LANES), lambda b, p, t: (0, p))
    return pl.pallas_call(
        functools.partial(_rwkv_kernel, rows=rows),
        grid=(batch, n_pairs, nt),
        in_specs=[
            pl.BlockSpec((rows, LANES), row_map(0)),
            pl.BlockSpec((rows, LANES), row_map(n_pairs)),
            pl.BlockSpec((rows, LANES), row_map(2 * n_pairs)),
            pl.BlockSpec((rows, lora_blk * LANES), lambda b, p, t: (b * nt + t, 3 * n_pairs // lora_blk)),
            vec, vec, vec, vec, vec, vec, vec,
            pl.BlockSpec((LANES, LANES), lambda b, p, t: (0, p)),
            pl.BlockSpec((LANES, LANES), lambda b, p, t: (0, p)),
            pl.BlockSpec((2 * LANES, LANES), lambda b, p, t: (0, p)),
        ],
        out_specs=pl.BlockSpec((rows, LANES), row_map(0)),
        out_shape=jax.ShapeDtypeStruct((T, n_pairs * LANES), BF16),
        scratch_shapes=[pltpu.VMEM((LANES, LANES), F32)],
        compiler_params=_cparams("arbitrary", "arbitrary", "arbitrary"),
        name="rwkv7",
    )(cols, cols, cols, cols, w0, a0, k_k, k_a, r_k, lnx_w, lnx_b, w2p, a2p, g2p)


def _out_proj_kernel(x_ref, oa_ref, or_ref, wa_ref, wr_ref, nw_ref, x1_ref, h2_ref):
    x1 = (x_ref[...] + jnp.dot(oa_ref[...], wa_ref[...], preferred_element_type=F32)
          + jnp.dot(or_ref[...], wr_ref[...], preferred_element_type=F32))
    x1_ref[...] = x1
    h2_ref[...] = _rms(x1, nw_ref[...]).astype(BF16)


def _out_proj(x2, oa, orw, wa, wr, nw, *, tm):
    T, D = x2.shape
    W = oa.shape[1]
    return pl.pallas_call(
        _out_proj_kernel,
        grid=(T // tm,),
        in_specs=[
            pl.BlockSpec((tm, D), lambda i: (i, 0)),
            pl.BlockSpec((tm, W), lambda i: (i, 0)),
            pl.BlockSpec((tm, W), lambda i: (i, 0)),
            pl.BlockSpec((W, D), lambda i: (0, 0)),
            pl.BlockSpec((W, D), lambda i: (0, 0)),
            pl.BlockSpec((1, D), lambda i: (0, 0)),
        ],
        out_specs=[pl.BlockSpec((tm, D), lambda i: (i, 0)), pl.BlockSpec((tm, D), lambda i: (i, 0))],
        out_shape=[jax.ShapeDtypeStruct((T, D), F32), jax.ShapeDtypeStruct((T, D), BF16)],
        compiler_params=_cparams("arbitrary"),
        name="out_proj",
    )(x2, oa, orw, wa, wr, nw)


def _topk_rows(curs, k):
    curs = list(curs)
    rows = [[] for _ in curs]
    for _ in range(k):
        ms = [jnp.max(cur, axis=0, keepdims=True) for cur in curs]
        curs = [jnp.where(cur == m, -jnp.inf, cur) for cur, m in zip(curs, ms)]
        for r, m in zip(rows, ms):
            r.append(m)
    return rows


_NLIST = PEER_TOPK + 1
_CANDS = [(a, b) for a in range(_NLIST) for b in range(_NLIST) if (a + 1) * (b + 1) <= _NLIST]
_NCAND = -(-len(_CANDS) // 8) * 8


def _route_kernel(h2_ref, wq_ref, keys_ref, thr_ref, e1_ref, e2_ref, sc_ref, cand_ref, *, n_heads):
    q = jnp.dot(h2_ref[...], wq_ref[...], preferred_element_type=F32).astype(BF16)
    for g in range(2 * n_heads):
        sc_ref[g] = lax.dot_general(keys_ref[g], q[:, g * LANES:(g + 1) * LANES], _NT,
                                    preferred_element_type=F32)
    cand_ref[...] = jnp.full(cand_ref.shape, -jnp.inf, F32)

    def per_head(h, _):
        s1 = sc_ref[2 * h]
        s2 = sc_ref[2 * h + 1]
        v1, v2 = _topk_rows((s1, s2), _NLIST)
        for n, (ia, ib) in enumerate(_CANDS):
            cand_ref[n:n + 1, :] = v1[ia] + v2[ib]
        cand = cand_ref[...]
        top, = _topk_rows((cand,), _NLIST)
        tau = 0.5 * (top[PEER_TOPK - 1] + top[PEER_TOPK])
        inv_z = 1.0 / jnp.sum(jnp.where(cand > tau, jnp.exp(cand - (v1[0] + v2[0])), 0.0), axis=0, keepdims=True)
        e1_ref[h] = jnp.exp(s1 - v1[0])
        e2_ref[h] = jnp.exp(s2 - v2[0]) * inv_z
        thr_ref[h] = jnp.exp((tau - s1) - v2[0]) * inv_z
        return 0

    lax.fori_loop(0, n_heads, per_head, 0)


def _route(h2, wq, keys, *, tt, n_heads):
    T, D = h2.shape
    nk = keys.shape[1]
    big = jax.ShapeDtypeStruct((n_heads, nk, T), F32)
    blk = pl.BlockSpec((n_heads, nk, tt), lambda i: (0, 0, i))
    return pl.pallas_call(
        functools.partial(_route_kernel, n_heads=n_heads),
        grid=(T // tt,),
        in_specs=[
            pl.BlockSpec((tt, D), lambda i: (i, 0)),
            pl.BlockSpec(wq.shape, lambda i: (0, 0)),
            pl.BlockSpec(keys.shape, lambda i: (0, 0, 0)),
        ],
        out_specs=[blk, blk, blk],
        out_shape=[big, big, big],
        scratch_shapes=[pltpu.VMEM((2 * n_heads, nk, tt), F32), pltpu.VMEM((_NCAND, tt), F32)],
        compiler_params=_cparams("arbitrary"),
        name="peer_route",
    )(h2, wq, keys)


def _peer_kernel(h2_ref, down_ref, upt_ref, thr_ref, e1_ref, e2_ref, o_ref, act_ref, *, n_heads, nk):
    @pl.when(pl.program_id(1) == 0)
    def _():
        o_ref[...] = jnp.zeros(o_ref.shape, F32)

    n_i1 = down_ref.shape[0] // nk
    per_chunk = n_i1 // PEER_Z_CHUNKS
    for q in range(PEER_Z_CHUNKS):
        zr = slice(q * per_chunk * nk, (q + 1) * per_chunk * nk)
        z = lax.dot_general(down_ref[zr, :], h2_ref[...], _NT, preferred_element_type=F32)
        for ic in range(per_chunk):
            ii = q * per_chunk + ic
            rs = slice(ii * nk, (ii + 1) * nk)
            gate = None
            for h in range(n_heads):
                e2 = e2_ref[h]
                val = jnp.where(e2 >= thr_ref[h, ii:ii + 1, :], e2, 0.0) * e1_ref[h, ii:ii + 1, :]
                gate = val if gate is None else gate + val
            zz = z[ic * nk:(ic + 1) * nk]
            gelu = zz * (1.0 + lax.erf(zz * (math.sqrt(0.5) / PEER_DOWN_SCALE)))
            act_ref[rs, :] = (gelu * gate).astype(FP8)
    d_rows = upt_ref.shape[0] // PEER_UP_CHUNKS
    for u in range(PEER_UP_CHUNKS):
        us = slice(u * d_rows, (u + 1) * d_rows)
        o_ref[us, :] += jnp.dot(upt_ref[us, :], act_ref[...], preferred_element_type=F32)


def _peer(h2, down, upt, thr, e1, e2, *, tt, et, n_heads, nk):
    T, D = h2.shape
    E = down.shape[0]
    i1_blk = et // nk
    row_blk = pl.BlockSpec((n_heads, i1_blk, tt), lambda i, j: (0, j, i))
    once = pl.Buffered(1)
    return pl.pallas_call(
        functools.partial(_peer_kernel, n_heads=n_heads, nk=nk),
        grid=(T // tt, E // et),
        in_specs=[
            pl.BlockSpec((tt, D), lambda i, j: (i, 0), pipeline_mode=once),
            pl.BlockSpec((et, D), lambda i, j: (j, 0)),
            pl.BlockSpec((D, et), lambda i, j: (0, j)),
            row_blk, row_blk,
            pl.BlockSpec((n_heads, nk, tt), lambda i, j: (0, 0, i), pipeline_mode=once),
        ],
        out_specs=pl.BlockSpec((D, tt), lambda i, j: (0, i)),
        out_shape=jax.ShapeDtypeStruct((D, T), F32),
        scratch_shapes=[pltpu.VMEM((et, tt), FP8)],
        compiler_params=_cparams("arbitrary", "arbitrary"),
        name="peer_experts",
    )(h2, down, upt, thr, e1, e2)


def _final_kernel(x1_ref, pt_ref, nw_ref, o_ref):
    o_ref[...] = _rms(x1_ref[...] + pt_ref[...].T * (1.0 / PEER_ACT_SCALE), nw_ref[...])


def _final(x1, peer_t, nw, *, tm):
    T, D = x1.shape
    return pl.pallas_call(
        _final_kernel,
        grid=(T // tm,),
        in_specs=[
            pl.BlockSpec((tm, D), lambda i: (i, 0)),
            pl.BlockSpec((D, tm), lambda i: (0, i)),
            pl.BlockSpec((1, D), lambda i: (0, 0)),
        ],
        out_specs=pl.BlockSpec((tm, D), lambda i: (i, 0)),
        out_shape=jax.ShapeDtypeStruct((T, D), F32),
        compiler_params=_cparams("arbitrary"),
        name="final_norm",
    )(x1, peer_t, nw)


def _tiles(seq, tokens):
    return dict(
        proj_tm=min(1024, seq),
        attn_tq=min(512, seq),
        rwkv_rows=min(512, seq),
        out_tm=min(512, tokens),
        route_tt=min(256, tokens),
        peer_tt=min(512, tokens),
        peer_et=2048,
        final_tm=min(256, tokens),
    )


def _pad_rows(w, rows_before, rows_total):
    return jnp.pad(w, ((rows_before, rows_total - rows_before - w.shape[0]), (0, 0)))


def kernel(x, norm1_w, w_in, tshift_mu, w0, w2, a0, a2, g2, k_k, k_a, r_k, lnx_w, lnx_b, lambda_q1, lambda_k1,
           lambda_q2, lambda_k2, subln_w, w_out, norm2_w, peer_w_query, peer_sub_keys, peer_down, peer_up,
           norm_f_w):
    B, S, D = x.shape
    T = B * S
    depth = norm1_w.shape[0]
    dh = lambda_q1.shape[-1]
    n_rheads, rhead = r_k.shape[1], r_k.shape[2]
    rw = n_rheads * rhead
    n_pairs = rw // LANES
    w_lora, a_lora, g_lora = w2.shape[1], a2.shape[1], g2.shape[1]
    da_cols = w_in.shape[2] - (3 * rw + w_lora + a_lora + g_lora)
    n_aheads = da_cols // (3 * 2 * dh)
    p_heads, nk = peer_sub_keys.shape[1], peer_sub_keys.shape[3]
    assert 2 * dh == LANES and 2 * rhead == LANES and rhead == RWKV_CHUNK and nk == LANES
    assert w_lora + a_lora == LANES and g_lora <= 2 * LANES
    t = _tiles(S, T)

    xt = x.reshape(T, D)
    for l in range(depth):
        row = lambda p: p[l].reshape(1, -1)
        lam_init = 0.8 - 0.6 * math.exp(-0.3 * l)

        w_attn = w_in[l][:, :da_cols].astype(BF16)
        rcols = w_in[l].shape[1] - da_cols
        rpad = 3 * rw + 3 * LANES - rcols
        w_rwkv = jnp.pad(w_in[l][:, da_cols:], ((0, 0), (0, rpad))).astype(BF16)
        mu = jnp.pad(row(tshift_mu), ((0, 0), (0, rpad)))
        w2p = _pad_rows(w2[l], 0, LANES).astype(BF16)
        a2p = _pad_rows(a2[l], w_lora, LANES).astype(BF16)
        g2p = _pad_rows(g2[l], 0, 2 * LANES).astype(BF16)

        qkv = _in_proj(xt, row(norm1_w), w_attn, None, seq=S, tm=t["proj_tm"], tn=da_cols // 3)
        rcol = _in_proj(xt, row(norm1_w), w_rwkv, mu, seq=S, tm=t["proj_tm"], tn=w_rwkv.shape[1] // 3)

        o_attn = _diff_attention(qkv, row(lambda_q1), row(lambda_k1), row(lambda_q2), row(lambda_k2),
                                 row(subln_w), batch=B, seq=S, n_heads=n_aheads, tq=t["attn_tq"],
                                 lam_init=lam_init)
        o_rwkv = _rwkv(rcol, row(w0), row(a0), row(k_k), row(k_a), row(r_k), row(lnx_w), row(lnx_b),
                       w2p, a2p, g2p, batch=B, seq=S, n_pairs=n_pairs, rows=t["rwkv_rows"])

        aw = o_attn.shape[1]
        x1, h2 = _out_proj(xt, o_attn, o_rwkv, w_out[l][:aw].astype(BF16), w_out[l][aw:].astype(BF16),
                           row(norm2_w), tm=t["out_tm"])

        keys = peer_sub_keys[l].reshape(2 * p_heads, nk, -1).astype(BF16)
        thr, e1, e2 = _route(h2, peer_w_query[l].astype(BF16), keys, tt=t["route_tt"], n_heads=p_heads)
        peer_t = _peer(h2.astype(FP8), (peer_down[l] * PEER_DOWN_SCALE).astype(FP8), peer_up[l].T.astype(FP8), thr, e1, e2,
                       tt=t["peer_tt"], et=t["peer_et"], n_heads=p_heads, nk=nk)
        if l + 1 < depth:
            xt = x1 + peer_t.T * (1.0 / PEER_ACT_SCALE)
        else:
            return _final(x1, peer_t, norm_f_w.reshape(1, -1), tm=t["final_tm"]).reshape(B, S, D)
```

```python
import functools
import math

import jax
import jax.numpy as jnp
from jax import lax
from jax.experimental import pallas as pl
from jax.experimental.pallas import tpu as pltpu

F32 = jnp.float32
BF16 = jnp.bfloat16
FP8 = jnp.float8_e4m3fn

LANES = 128
RMS_EPS = 1e-6
LNX_EPS = 64e-5
PEER_TOPK = 16
RWKV_CHUNK = 64
PEER_DOWN_SCALE = 32.0
PEER_ACT_SCALE = 2.0 * PEER_DOWN_SCALE
PEER_Z_CHUNKS = 2
PEER_UP_CHUNKS = 4
VMEM_LIMIT = 56 * 1024 * 1024
NEG = -1e30
SOFTMAX_UNDERFLOW = 110.0

_NT = (((1,), (1,)), ((), ()))


def _cparams(*sem):
    return pltpu.CompilerParams(dimension_semantics=sem, vmem_limit_bytes=VMEM_LIMIT)


def _rms(x, w):
    ms = jnp.mean(x * x, axis=-1, keepdims=True)
    return x * lax.rsqrt(ms + RMS_EPS) * w


def _proj_kernel(x_ref, nw_ref, w_ref, o_ref, h_ref):
    @pl.when(pl.program_id(1) == 0)
    def _():
        h_ref[...] = _rms(x_ref[...], nw_ref[...]).astype(BF16)

    o_ref[...] = jnp.dot(h_ref[...], w_ref[...], preferred_element_type=F32).astype(o_ref.dtype)


def _proj_shift_kernel(x_ref, nw_ref, w_ref, mu_ref, o_ref, h_ref, carry_ref, *, tiles_per_seq):
    i = pl.program_id(0)
    j = pl.program_id(1)

    @pl.when(j == 0)
    def _():
        h_ref[...] = _rms(x_ref[...], nw_ref[...]).astype(BF16)

    @pl.when(i % tiles_per_seq == 0)
    def _():
        carry_ref[j] = jnp.zeros(carry_ref.shape[1:], F32)

    p = jnp.dot(h_ref[...], w_ref[...], preferred_element_type=F32)
    tm = p.shape[0]
    rolled = pltpu.roll(p, 1, 0)
    row = lax.broadcasted_iota(jnp.int32, p.shape, 0)
    prev = jnp.where(row == 0, carry_ref[j][7:8, :], rolled)
    o_ref[...] = (p + (prev - p) * mu_ref[...]).astype(o_ref.dtype)
    carry_ref[j] = p[tm - 8:tm, :]


def _in_proj(x2, nw, w, mu, *, seq, tm, tn):
    T, D = x2.shape
    N = w.shape[1]
    grid = (T // tm, N // tn)
    in_specs = [
        pl.BlockSpec((tm, D), lambda i, j: (i, 0)),
        pl.BlockSpec((1, D), lambda i, j: (0, 0)),
        pl.BlockSpec((D, tn), lambda i, j: (0, j)),
    ]
    args = [x2, nw, w]
    scratch = [pltpu.VMEM((tm, D), BF16)]
    if mu is None:
        body = _proj_kernel
    else:
        body = functools.partial(_proj_shift_kernel, tiles_per_seq=seq // tm)
        in_specs.append(pl.BlockSpec((1, tn), lambda i, j: (0, j)))
        args.append(mu)
        scratch.append(pltpu.VMEM((N // tn, 8, tn), F32))
    return pl.pallas_call(
        body,
        grid=grid,
        in_specs=in_specs,
        out_specs=pl.BlockSpec((tm, tn), lambda i, j: (i, j)),
        out_shape=jax.ShapeDtypeStruct((T, N), BF16),
        scratch_shapes=scratch,
        compiler_params=_cparams("arbitrary", "arbitrary"),
        name="in_proj" if mu is None else "in_proj_shift",
    )(*args)


def _max_half_sqnorm(x, lane2, dh):
    sq = x * x
    lo = jnp.sum(jnp.where(lane2 < dh, sq, 0.0), axis=-1, keepdims=True)
    hi = jnp.sum(jnp.where(lane2 >= dh, sq, 0.0), axis=-1, keepdims=True)
    return jnp.max(jnp.maximum(lo, hi), axis=0, keepdims=True)


def _attn_kernel(q_ref, k_ref, v_ref, lq1_ref, lk1_ref, lq2_ref, lk2_ref, sw_ref, o_ref, kx1_ref, kx2_ref, knorm_ref,
                 *, tq, seq, n_heads, lam_init):
    h = pl.program_id(1)
    qi = pl.program_id(2)
    tk = tq
    dh = LANES // 2
    lane2 = lax.broadcasted_iota(jnp.int32, (tq, LANES), 1)

    @pl.when(qi == 0)
    def _():
        def build(c, knorm):
            rows = pl.ds(pl.multiple_of(c * tk, tk), tk)
            kb = k_ref[rows, :].astype(F32)
            pos = c * tk + lax.broadcasted_iota(jnp.int32, (tk, LANES), 0)
            hi = (pos // LANES).astype(F32)
            lo = (pos % LANES).astype(F32)
            kx1_ref[rows, :] = jnp.where(lane2 < dh, kb, jnp.where(lane2 == dh, hi, jnp.where(lane2 == dh + 1, lo, 0.0))
                                         ).astype(BF16)
            kx2_ref[rows, :] = jnp.where(lane2 >= dh, kb, jnp.where(lane2 == 0, hi, jnp.where(lane2 == 1, lo, 0.0))
                                         ).astype(BF16)
            return jnp.maximum(knorm, _max_half_sqnorm(kb, lane2, dh))

        knorm_ref[...] = lax.fori_loop(0, seq // tk, build, jnp.zeros((1, 1), F32))

    q = q_ref[...].astype(F32) * dh ** -0.5
    hv = jnp.full((tq, LANES), h + 1, jnp.int32).astype(F32)
    slope = jnp.exp2(hv * (-8.0 / n_heads))
    q1 = jnp.where(lane2 < dh, q, jnp.where(lane2 == dh, slope * LANES, jnp.where(lane2 == dh + 1, slope, 0.0))
                   ).astype(BF16)
    q2 = jnp.where(lane2 >= dh, q, jnp.where(lane2 == 0, slope * LANES, jnp.where(lane2 == 1, slope, 0.0))
                   ).astype(BF16)

    col = lax.broadcasted_iota(jnp.int32, (1, tk), 1)
    row = lax.broadcasted_iota(jnp.int32, (tq, 1), 0)

    def step(kc, carry, masked):
        rows = pl.ds(pl.multiple_of(kc * tk, tk), tk)
        vb = v_ref[rows, :]
        maps = (0, 1)
        ss = [lax.dot_general(qc, kx_ref[rows, :], _NT, preferred_element_type=F32)
              for qc, kx_ref in ((q1, kx1_ref), (q2, kx2_ref))]
        if masked:
            ss = [jnp.where(col <= row, s, NEG) for s in ss]
        ms, ls, accs = carry[0::3], carry[1::3], carry[2::3]
        mns = [jnp.maximum(ms[c], jnp.max(ss[c], axis=-1, keepdims=True)) for c in maps]
        ps = [jnp.exp(ss[c] - mns[c]) for c in maps]
        als = [jnp.exp(ms[c] - mns[c]) for c in maps]
        ls = [als[c] * ls[c] + jnp.sum(ps[c], axis=-1, keepdims=True) for c in maps]
        pvs = [jnp.dot(ps[c].astype(BF16), vb, preferred_element_type=F32) for c in maps]
        accs = [als[c] * accs[c] + pvs[c] for c in maps]
        return (mns[0], ls[0], accs[0], mns[1], ls[1], accs[1])

    init = (jnp.full((tq, 1), NEG, F32), jnp.zeros((tq, 1), F32), jnp.zeros((tq, LANES), F32)) * 2
    bound = jnp.sqrt(_max_half_sqnorm(q, lane2, dh) * knorm_ref[...])
    reach = (SOFTMAX_UNDERFLOW + 2.0 * bound) / slope[:1, :1]
    q0 = jnp.full((1, 1), qi * tq, jnp.int32).astype(F32)
    first = jnp.clip(jnp.floor((q0 + 1.0 - reach) / tk), 0.0, q0 / tq)
    carry = lax.fori_loop(jnp.max(first.astype(jnp.int32)), qi, lambda kc, c: step(kc, c, False), init)
    m1, l1, a1, m2, l2, a2 = step(qi, carry, True)

    lam = (jnp.exp(jnp.sum(lq1_ref[...] * lk1_ref[...], axis=-1, keepdims=True))
           - jnp.exp(jnp.sum(lq2_ref[...] * lk2_ref[...], axis=-1, keepdims=True)) + lam_init)
    o = a1 / l1 - lam * (a2 / l2)
    o_ref[...] = (_rms(o, sw_ref[...]) * (1.0 - lam_init)).astype(o_ref.dtype)


def _diff_attention(qkv, lq1, lk1, lq2, lk2, subln_w, *, batch, seq, n_heads, tq, lam_init):
    T = qkv.shape[0]
    nq = seq // tq
    assert 8 % n_heads == 0 and seq <= 256 * LANES
    vec = lambda n: pl.BlockSpec((1, n), lambda b, h, i: (0, 0))
    return pl.pallas_call(
        functools.partial(_attn_kernel, tq=tq, seq=seq, n_heads=n_heads, lam_init=lam_init),
        grid=(batch, n_heads, nq),
        in_specs=[
            pl.BlockSpec((tq, LANES), lambda b, h, i: (b * nq + i, h)),
            pl.BlockSpec((seq, LANES), lambda b, h, i: (b, n_heads + h)),
            pl.BlockSpec((seq, LANES), lambda b, h, i: (b, 2 * n_heads + h)),
            vec(lq1.shape[1]), vec(lq1.shape[1]), vec(lq1.shape[1]), vec(lq1.shape[1]),
            vec(LANES),
        ],
        out_specs=pl.BlockSpec((tq, LANES), lambda b, h, i: (b * nq + i, h)),
        out_shape=jax.ShapeDtypeStruct((T, n_heads * LANES), BF16),
        scratch_shapes=[pltpu.VMEM((seq, LANES), BF16), pltpu.VMEM((seq, LANES), BF16), pltpu.VMEM((1, 1), F32)],
        compiler_params=_cparams("arbitrary", "arbitrary", "arbitrary"),
        name="diff_attn",
    )(qkv, qkv, qkv, lq1, lk1, lq2, lk2, subln_w)


def _split_bf16(x, n):
    parts = []
    for _ in range(n):
        p = x.astype(BF16)
        parts.append(p)
        x = x - p.astype(F32)
    return parts


def _dot(a, b):
    return jnp.dot(a.astype(BF16), b.astype(BF16), preferred_element_type=F32)


def _dot_exact_rhs(x, m_bf16, n):
    out = None
    for p in _split_bf16(x, n):
        t = jnp.dot(p, m_bf16, preferred_element_type=F32)
        out = t if out is None else out + t
    return out


def _dot_exact_lhs(m_bf16, x, n):
    out = None
    for p in _split_bf16(x, n):
        t = jnp.dot(m_bf16, p, preferred_element_type=F32)
        out = t if out is None else out + t
    return out


def _rwkv_kernel(r_ref, k_ref, v_ref, lo_ref, w0_ref, a0_ref, kk_ref, ka_ref, rk_ref, lw_ref, lb_ref,
                 w2_ref, a2_ref, g2_ref, o_ref, st_ref, *, rows):
    L = RWKV_CHUNK
    N = LANES // 2
    P = LANES

    @pl.when(pl.program_id(2) == 0)
    def _():
        st_ref[...] = jnp.zeros(st_ref.shape, F32)

    r = r_ref[...].astype(F32)
    k = k_ref[...].astype(F32)
    v = v_ref[...].astype(F32)
    lo = lo_ref[...]
    lo_wa = lo[:, :P]
    wl = jnp.dot(jnp.tanh(lo_wa.astype(F32)).astype(BF16), w2_ref[...], preferred_element_type=F32)
    al = jnp.dot(lo_wa, a2_ref[...], preferred_element_type=F32)
    g = jnp.dot(jax.nn.sigmoid(lo[:, P:].astype(F32)).astype(BF16), g2_ref[...], preferred_element_type=F32)

    wx = -(w0_ref[...] + wl)
    softplus = jnp.maximum(wx, 0.0) + jnp.log1p(jnp.exp(-jnp.abs(wx)))
    logdecay = -jnp.exp(-softplus - 0.5)
    a = jax.nn.sigmoid(a0_ref[...] + al)

    ri = lax.broadcasted_iota(jnp.int32, (P, P), 0)
    ci = lax.broadcasted_iota(jnp.int32, (P, P), 1)
    ones_bd = ((ri // N) == (ci // N)).astype(BF16)
    eye = (ri == ci).astype(F32)
    strict = ((ri % L) > (ci % L)).astype(F32)
    lower = ((ri % L) >= (ci % L)).astype(F32)
    ti = lax.broadcasted_iota(jnp.int32, (rows, rows), 0)
    tj = lax.broadcasted_iota(jnp.int32, (rows, rows), 1)
    tri_all = ((ti >= tj) & ((ti // L) == (tj // L))).astype(BF16)
    head0 = lax.broadcasted_iota(jnp.int32, (1, P), 1) < N

    kkr = k * kk_ref[...]
    norm = jnp.sqrt(_dot_exact_rhs(kkr * kkr, ones_bd, 2))
    kk = kkr / jnp.maximum(norm, 1e-12)
    kf = k * (1.0 + (a - 1.0) * ka_ref[...])
    avec = -kk
    bvec = kk * a

    def bd(x):
        x = x.astype(BF16)
        z = jnp.zeros_like(x)
        return jnp.concatenate([jnp.where(head0, x, z), jnp.where(head0, z, x)], axis=0)

    nc = rows // L
    chunks = range(nc)
    sls = [slice(c * L, (c + 1) * L) for c in chunks]
    cum_all = _dot_exact_lhs(tri_all, logdecay, 3)
    cum = [cum_all[s] for s in sls]
    tot = [cm[L - 1:L, :] for cm in cum]
    at = [bd(avec[s] * jnp.exp(cm - logdecay[s])) for s, cm in zip(sls, cum)]
    rt = [bd(r[s] * jnp.exp(cm)) for s, cm in zip(sls, cum)]
    inv = [jnp.exp(-cm) for cm in cum]
    bt = [bd(bvec[s] * iv) for s, iv in zip(sls, inv)]
    kt = [bd(kf[s] * iv) for s, iv in zip(sls, inv)]
    rest = [jnp.exp(t - cm) for t, cm in zip(tot, cum)]
    blkl_t = [jnp.concatenate([bd(bvec[s] * rs), bd(kf[s] * rs)], axis=1).astype(F32).T.astype(BF16)
              for s, rs in zip(sls, rest)]
    vd = [bd(v[s]) for s in sls]

    amat = [lax.dot_general(jnp.concatenate([at[c], rt[c]], axis=0), jnp.concatenate([bt[c], kt[c]], axis=0), _NT,
                            preferred_element_type=F32) for c in chunks]
    a_ab = [m[:P, :P] * strict for m in amat]
    a_ak = [(m[:P, P:] * strict).astype(BF16) for m in amat]
    a_rb = [(m[P:, :P] * lower).astype(BF16) for m in amat]
    a_rk = [(m[P:, P:] * lower).astype(BF16) for m in amat]

    tinv = [eye + n for n in a_ab]
    pw = [_dot(n, n) for n in a_ab]
    for _ in range(int(math.log2(L)) - 2):
        both = [_dot(pw[c], jnp.concatenate([pw[c], tinv[c]], axis=1)) for c in chunks]
        pw = [b[:, :P] for b in both]
        tinv = [tinv[c] + both[c][:, P:] for c in chunks]
    tinv = [tinv[c] + _dot(pw[c], tinv[c]) for c in chunks]

    avd = [_dot(jnp.concatenate([a_ak[c], a_rk[c], blkl_t[c][P:]], axis=0), vd[c]) for c in chunks]
    hat = [_dot(tinv[c], jnp.concatenate([at[c], avd[c][:P].astype(BF16)], axis=1)) for c in chunks]
    mix = [_dot(jnp.concatenate([a_rb[c], blkl_t[c][:P]], axis=0), hat[c]) for c in chunks]
    rhat = [rt[c].astype(F32) + mix[c][:P, :P] for c in chunks]
    ohat = [mix[c][:P, P:] + avd[c][P:2 * P] for c in chunks]
    mmat = [eye * jnp.exp(tot[c]) + mix[c][P:, :P] for c in chunks]
    cmat = [mix[c][P:, P:] + avd[c][2 * P:] for c in chunks]
    rhat = [x.astype(BF16) for x in rhat]
    mmat = [x.astype(BF16) for x in mmat]

    states = []
    st = st_ref[...]
    for c in chunks:
        states.append(st.astype(BF16))
        st = _dot(mmat[c], states[c]) + cmat[c]
    st_ref[...] = st
    outs = []
    for c in chunks:
        od = _dot(rhat[c], states[c]) + ohat[c]
        outs.append(od[:L] + od[L:])
    wkv = jnp.concatenate(outs, axis=0)

    inv_n = 1.0 / N
    mu = _dot_exact_rhs(wkv, ones_bd, 2) * inv_n
    xc = wkv - mu
    var = _dot_exact_rhs(xc * xc, ones_bd, 2) * inv_n
    o = xc * lax.rsqrt(var + LNX_EPS) * lw_ref[...] + lb_ref[...]
    bonus = _dot_exact_rhs(r * kf * rk_ref[...], ones_bd, 2) * v
    o_ref[...] = ((o + bonus) * g).astype(o_ref.dtype)


def _rwkv(cols, w0, a0, k_k, k_a, r_k, lnx_w, lnx_b, w2p, a2p, g2p, *, batch, seq, n_pairs, rows):
    T = cols.shape[0]
    nt = seq // rows
    lora_blk = (cols.shape[1] - 3 * n_pairs * LANES) // LANES
    row_map = lambda off: (lambda b, p, t: (b * nt + t, off + p))
    vec = pl.BlockSpec((1, LANES), lambda b, p, t: (0, p))
    return pl.pallas_call(
        functools.partial(_rwkv_kernel, rows=rows),
        grid=(batch, n_pairs, nt),
        in_specs=[
            pl.BlockSpec((rows, LANES), row_map(0)),
            pl.BlockSpec((rows, LANES), row_map(n_pairs)),
            pl.BlockSpec((rows, LANES), row_map(2 * n_pairs)),
            pl.BlockSpec((rows, lora_blk * LANES), lambda b, p, t: (b * nt + t, 3 * n_pairs // lora_blk)),
            vec, vec, vec, vec, vec, vec, vec,
            pl.BlockSpec((LANES, LANES), lambda b, p, t: (0, p)),
            pl.BlockSpec((LANES, LANES), lambda b, p, t: (0, p)),
            pl.BlockSpec((2 * LANES, LANES), lambda b, p, t: (0, p)),
        ],
        out_specs=pl.BlockSpec((rows, LANES), row_map(0)),
        out_shape=jax.ShapeDtypeStruct((T, n_pairs * LANES), BF16),
        scratch_shapes=[pltpu.VMEM((LANES, LANES), F32)],
        compiler_params=_cparams("arbitrary", "arbitrary", "arbitrary"),
        name="rwkv7",
    )(cols, cols, cols, cols, w0, a0, k_k, k_a, r_k, lnx_w, lnx_b, w2p, a2p, g2p)


def _out_proj_kernel(x_ref, oa_ref, or_ref, wa_ref, wr_ref, nw_ref, x1_ref, h2_ref):
    x1 = (x_ref[...] + jnp.dot(oa_ref[...], wa_ref[...], preferred_element_type=F32)
          + jnp.dot(or_ref[...], wr_ref[...], preferred_element_type=F32))
    x1_ref[...] = x1
    h2_ref[...] = _rms(x1, nw_ref[...]).astype(BF16)


def _out_proj(x2, oa, orw, wa, wr, nw, *, tm):
    T, D = x2.shape
    W = oa.shape[1]
    return pl.pallas_call(
        _out_proj_kernel,
        grid=(T // tm,),
        in_specs=[
            pl.BlockSpec((tm, D), lambda i: (i, 0)),
            pl.BlockSpec((tm, W), lambda i: (i, 0)),
            pl.BlockSpec((tm, W), lambda i: (i, 0)),
            pl.BlockSpec((W, D), lambda i: (0, 0)),
            pl.BlockSpec((W, D), lambda i: (0, 0)),
            pl.BlockSpec((1, D), lambda i: (0, 0)),
        ],
        out_specs=[pl.BlockSpec((tm, D), lambda i: (i, 0)), pl.BlockSpec((tm, D), lambda i: (i, 0))],
        out_shape=[jax.ShapeDtypeStruct((T, D), F32), jax.ShapeDtypeStruct((T, D), BF16)],
        compiler_params=_cparams("arbitrary"),
        name="out_proj",
    )(x2, oa, orw, wa, wr, nw)


def _topk_rows(curs, k):
    curs = list(curs)
    rows = [[] for _ in curs]
    for _ in range(k):
        ms = [jnp.max(cur, axis=0, keepdims=True) for cur in curs]
        curs = [jnp.where(cur == m, -jnp.inf, cur) for cur, m in zip(curs, ms)]
        for r, m in zip(rows, ms):
            r.append(m)
    return rows


_NLIST = PEER_TOPK + 1
_CANDS = [(a, b) for a in range(_NLIST) for b in range(_NLIST) if (a + 1) * (b + 1) <= _NLIST]
_NCAND = -(-len(_CANDS) // 8) * 8


def _route_kernel(h2_ref, wq_ref, keys_ref, thr_ref, e1_ref, e2_ref, sc_ref, cand_ref, *, n_heads):
    q = jnp.dot(h2_ref[...], wq_ref[...], preferred_element_type=F32).astype(BF16)
    for g in range(2 * n_heads):
        sc_ref[g] = lax.dot_general(keys_ref[g], q[:, g * LANES:(g + 1) * LANES], _NT,
                                    preferred_element_type=F32)
    cand_ref[...] = jnp.full(cand_ref.shape, -jnp.inf, F32)

    def per_head(h, _):
        s1 = sc_ref[2 * h]
        s2 = sc_ref[2 * h + 1]
        v1, v2 = _topk_rows((s1, s2), _NLIST)
        for n, (ia, ib) in enumerate(_CANDS):
            cand_ref[n:n + 1, :] = v1[ia] + v2[ib]
        cand = cand_ref[...]
        top, = _topk_rows((cand,), _NLIST)
        tau = 0.5 * (top[PEER_TOPK - 1] + top[PEER_TOPK])
        inv_z = 1.0 / jnp.sum(jnp.where(cand > tau, jnp.exp(cand - (v1[0] + v2[0])), 0.0), axis=0, keepdims=True)
        e1_ref[h] = jnp.exp(s1 - v1[0])
        e2_ref[h] = jnp.exp(s2 - v2[0]) * inv_z
        thr_ref[h] = jnp.exp((tau - s1) - v2[0]) * inv_z
        return 0

    lax.fori_loop(0, n_heads, per_head, 0)


def _route(h2, wq, keys, *, tt, n_heads):
    T, D = h2.shape
    nk = keys.shape[1]
    big = jax.ShapeDtypeStruct((n_heads, nk, T), F32)
    blk = pl.BlockSpec((n_heads, nk, tt), lambda i: (0, 0, i))
    return pl.pallas_call(
        functools.partial(_route_kernel, n_heads=n_heads),
        grid=(T // tt,),
        in_specs=[
            pl.BlockSpec((tt, D), lambda i: (i, 0)),
            pl.BlockSpec(wq.shape, lambda i: (0, 0)),
            pl.BlockSpec(keys.shape, lambda i: (0, 0, 0)),
        ],
        out_specs=[blk, blk, blk],
        out_shape=[big, big, big],
        scratch_shapes=[pltpu.VMEM((2 * n_heads, nk, tt), F32), pltpu.VMEM((_NCAND, tt), F32)],
        compiler_params=_cparams("arbitrary"),
        name="peer_route",
    )(h2, wq, keys)


def _peer_kernel(h2_ref, down_ref, upt_ref, thr_ref, e1_ref, e2_ref, o_ref, act_ref, *, n_heads, nk):
    @pl.when(pl.program_id(1) == 0)
    def _():
        o_ref[...] = jnp.zeros(o_ref.shape, F32)

    n_i1 = down_ref.shape[0] // nk
    per_chunk = n_i1 // PEER_Z_CHUNKS
    for q in range(PEER_Z_CHUNKS):
        zr = slice(q * per_chunk * nk, (q + 1) * per_chunk * nk)
        z = lax.dot_general(down_ref[zr, :], h2_ref[...], _NT, preferred_element_type=F32)
        for ic in range(per_chunk):
            ii = q * per_chunk + ic
            rs = slice(ii * nk, (ii + 1) * nk)
            gate = None
            for h in range(n_heads):
                e2 = e2_ref[h]
                val = jnp.where(e2 >= thr_ref[h, ii:ii + 1, :], e2, 0.0) * e1_ref[h, ii:ii + 1, :]
                gate = val if gate is None else gate + val
            zz = z[ic * nk:(ic + 1) * nk]
            gelu = zz * (1.0 + lax.erf(zz * (math.sqrt(0.5) / PEER_DOWN_SCALE)))
            act_ref[rs, :] = (gelu * gate).astype(FP8)
    d_rows = upt_ref.shape[0] // PEER_UP_CHUNKS
    e_rows = down_ref.shape[0] // PEER_Z_CHUNKS
    for q in range(PEER_Z_CHUNKS):
        es = slice(q * e_rows, (q + 1) * e_rows)
        for u in range(PEER_UP_CHUNKS):
            us = slice(u * d_rows, (u + 1) * d_rows)
            o_ref[us, :] += jnp.dot(upt_ref[us, es], act_ref[es, :], preferred_element_type=F32)


def _peer(h2, down, upt, thr, e1, e2, *, tt, et, n_heads, nk):
    T, D = h2.shape
    E = down.shape[0]
    i1_blk = et // nk
    row_blk = pl.BlockSpec((n_heads, i1_blk, tt), lambda i, j: (0, j, i))
    once = pl.Buffered(1)
    return pl.pallas_call(
        functools.partial(_peer_kernel, n_heads=n_heads, nk=nk),
        grid=(T // tt, E // et),
        in_specs=[
            pl.BlockSpec((tt, D), lambda i, j: (i, 0), pipeline_mode=once),
            pl.BlockSpec((et, D), lambda i, j: (j, 0)),
            pl.BlockSpec((D, et), lambda i, j: (0, j)),
            row_blk, row_blk,
            pl.BlockSpec((n_heads, nk, tt), lambda i, j: (0, 0, i), pipeline_mode=once),
        ],
        out_specs=pl.BlockSpec((D, tt), lambda i, j: (0, i)),
        out_shape=jax.ShapeDtypeStruct((D, T), F32),
        scratch_shapes=[pltpu.VMEM((et, tt), FP8)],
        compiler_params=_cparams("arbitrary", "arbitrary"),
        name="peer_experts",
    )(h2, down, upt, thr, e1, e2)


def _final_kernel(x1_ref, pt_ref, nw_ref, o_ref):
    o_ref[...] = _rms(x1_ref[...] + pt_ref[...].T * (1.0 / PEER_ACT_SCALE), nw_ref[...])


def _final(x1, peer_t, nw, *, tm):
    T, D = x1.shape
    return pl.pallas_call(
        _final_kernel,
        grid=(T // tm,),
        in_specs=[
            pl.BlockSpec((tm, D), lambda i: (i, 0)),
            pl.BlockSpec((D, tm), lambda i: (0, i)),
            pl.BlockSpec((1, D), lambda i: (0, 0)),
        ],
        out_specs=pl.BlockSpec((tm, D), lambda i: (i, 0)),
        out_shape=jax.ShapeDtypeStruct((T, D), F32),
        compiler_params=_cparams("arbitrary"),
        name="final_norm",
    )(x1, peer_t, nw)


def _tiles(seq, tokens):
    return dict(
        proj_tm=min(1024, seq),
        attn_tq=min(512, seq),
        rwkv_rows=min(512, seq),
        out_tm=min(512, tokens),
        route_tt=min(256, tokens),
        peer_tt=min(512, tokens),
        peer_et=2048,
        final_tm=min(512, tokens),
    )


def _pad_rows(w, rows_before, rows_total):
    return jnp.pad(w, ((rows_before, rows_total - rows_before - w.shape[0]), (0, 0)))


def kernel(x, norm1_w, w_in, tshift_mu, w0, w2, a0, a2, g2, k_k, k_a, r_k, lnx_w, lnx_b, lambda_q1, lambda_k1,
           lambda_q2, lambda_k2, subln_w, w_out, norm2_w, peer_w_query, peer_sub_keys, peer_down, peer_up,
           norm_f_w):
    B, S, D = x.shape
    T = B * S
    depth = norm1_w.shape[0]
    dh = lambda_q1.shape[-1]
    n_rheads, rhead = r_k.shape[1], r_k.shape[2]
    rw = n_rheads * rhead
    n_pairs = rw // LANES
    w_lora, a_lora, g_lora = w2.shape[1], a2.shape[1], g2.shape[1]
    da_cols = w_in.shape[2] - (3 * rw + w_lora + a_lora + g_lora)
    n_aheads = da_cols // (3 * 2 * dh)
    p_heads, nk = peer_sub_keys.shape[1], peer_sub_keys.shape[3]
    assert 2 * dh == LANES and 2 * rhead == LANES and rhead == RWKV_CHUNK and nk == LANES
    assert w_lora + a_lora == LANES and g_lora <= 2 * LANES
    t = _tiles(S, T)

    xt = x.reshape(T, D)
    for l in range(depth):
        row = lambda p: p[l].reshape(1, -1)
        lam_init = 0.8 - 0.6 * math.exp(-0.3 * l)

        w_attn = w_in[l][:, :da_cols].astype(BF16)
        rcols = w_in[l].shape[1] - da_cols
        rpad = 3 * rw + 3 * LANES - rcols
        w_rwkv = jnp.pad(w_in[l][:, da_cols:], ((0, 0), (0, rpad))).astype(BF16)
        mu = jnp.pad(row(tshift_mu), ((0, 0), (0, rpad)))
        w2p = _pad_rows(w2[l], 0, LANES).astype(BF16)
        a2p = _pad_rows(a2[l], w_lora, LANES).astype(BF16)
        g2p = _pad_rows(g2[l], 0, 2 * LANES).astype(BF16)

        qkv = _in_proj(xt, row(norm1_w), w_attn, None, seq=S, tm=t["proj_tm"], tn=da_cols // 3)
        rcol = _in_proj(xt, row(norm1_w), w_rwkv, mu, seq=S, tm=t["proj_tm"], tn=w_rwkv.shape[1] // 3)

        o_attn = _diff_attention(qkv, row(lambda_q1), row(lambda_k1), row(lambda_q2), row(lambda_k2),
                                 row(subln_w), batch=B, seq=S, n_heads=n_aheads, tq=t["attn_tq"],
                                 lam_init=lam_init)
        o_rwkv = _rwkv(rcol, row(w0), row(a0), row(k_k), row(k_a), row(r_k), row(lnx_w), row(lnx_b),
                       w2p, a2p, g2p, batch=B, seq=S, n_pairs=n_pairs, rows=t["rwkv_rows"])

        aw = o_attn.shape[1]
        x1, h2 = _out_proj(xt, o_attn, o_rwkv, w_out[l][:aw].astype(BF16), w_out[l][aw:].astype(BF16),
                           row(norm2_w), tm=t["out_tm"])

        keys = peer_sub_keys[l].reshape(2 * p_heads, nk, -1).astype(BF16)
        thr, e1, e2 = _route(h2, peer_w_query[l].astype(BF16), keys, tt=t["route_tt"], n_heads=p_heads)
        peer_t = _peer(h2.astype(FP8), (peer_down[l] * PEER_DOWN_SCALE).astype(FP8), peer_up[l].T.astype(FP8), thr, e1, e2,
                       tt=t["peer_tt"], et=t["peer_et"], n_heads=p_heads, nk=nk)
        if l + 1 < depth:
            xt = x1 + peer_t.T * (1.0 / PEER_ACT_SCALE)
        else:
            return _final(x1, peer_t, norm_f_w.reshape(1, -1), tm=t["final_tm"]).reshape(B, S, D)
```

```python
import functools
import math

import jax
import jax.numpy as jnp
from jax import lax
from jax.experimental import pallas as pl
from jax.experimental.pallas import tpu as pltpu

F32 = jnp.float32
BF16 = jnp.bfloat16
FP8 = jnp.float8_e4m3fn

LANES = 128
RMS_EPS = 1e-6
LNX_EPS = 64e-5
PEER_TOPK = 16
RWKV_CHUNK = 64
HEAD_SUM_PIECES = 1
PEER_DOWN_SCALE = 32.0
PEER_ACT_SCALE = 2.0 * PEER_DOWN_SCALE
PEER_Z_CHUNKS = 2
PEER_UP_CHUNKS = 4
VMEM_LIMIT = 56 * 1024 * 1024
NEG = -1e30
SOFTMAX_UNDERFLOW = 110.0

_NT = (((1,), (1,)), ((), ()))


def _cparams(*sem):
    return pltpu.CompilerParams(dimension_semantics=sem, vmem_limit_bytes=VMEM_LIMIT)


def _rms(x, w):
    ms = jnp.mean(x * x, axis=-1, keepdims=True)
    return x * lax.rsqrt(ms + RMS_EPS) * w


def _proj_kernel(x_ref, nw_ref, w_ref, o_ref, h_ref):
    @pl.when(pl.program_id(1) == 0)
    def _():
        h_ref[...] = _rms(x_ref[...], nw_ref[...]).astype(BF16)

    o_ref[...] = jnp.dot(h_ref[...], w_ref[...], preferred_element_type=F32).astype(o_ref.dtype)


def _proj_shift_kernel(x_ref, nw_ref, w_ref, mu_ref, o_ref, h_ref, carry_ref, *, tiles_per_seq):
    i = pl.program_id(0)
    j = pl.program_id(1)

    @pl.when(j == 0)
    def _():
        h_ref[...] = _rms(x_ref[...], nw_ref[...]).astype(BF16)

    @pl.when(i % tiles_per_seq == 0)
    def _():
        carry_ref[j] = jnp.zeros(carry_ref.shape[1:], F32)

    p = jnp.dot(h_ref[...], w_ref[...], preferred_element_type=F32)
    tm = p.shape[0]
    rolled = pltpu.roll(p, 1, 0)
    row = lax.broadcasted_iota(jnp.int32, p.shape, 0)
    prev = jnp.where(row == 0, carry_ref[j][7:8, :], rolled)
    o_ref[...] = (p + (prev - p) * mu_ref[...]).astype(o_ref.dtype)
    carry_ref[j] = p[tm - 8:tm, :]


def _in_proj(x2, nw, w, mu, *, seq, tm, tn):
    T, D = x2.shape
    N = w.shape[1]
    grid = (T // tm, N // tn)
    in_specs = [
        pl.BlockSpec((tm, D), lambda i, j: (i, 0)),
        pl.BlockSpec((1, D), lambda i, j: (0, 0)),
        pl.BlockSpec((D, tn), lambda i, j: (0, j)),
    ]
    args = [x2, nw, w]
    scratch = [pltpu.VMEM((tm, D), BF16)]
    if mu is None:
        body = _proj_kernel
    else:
        body = functools.partial(_proj_shift_kernel, tiles_per_seq=seq // tm)
        in_specs.append(pl.BlockSpec((1, tn), lambda i, j: (0, j)))
        args.append(mu)
        scratch.append(pltpu.VMEM((N // tn, 8, tn), F32))
    return pl.pallas_call(
        body,
        grid=grid,
        in_specs=in_specs,
        out_specs=pl.BlockSpec((tm, tn), lambda i, j: (i, j)),
        out_shape=jax.ShapeDtypeStruct((T, N), BF16),
        scratch_shapes=scratch,
        compiler_params=_cparams("arbitrary", "arbitrary"),
        name="in_proj" if mu is None else "in_proj_shift",
    )(*args)


def _max_half_sqnorm(x, lane2, dh):
    sq = x * x
    lo = jnp.sum(jnp.where(lane2 < dh, sq, 0.0), axis=-1, keepdims=True)
    hi = jnp.sum(jnp.where(lane2 >= dh, sq, 0.0), axis=-1, keepdims=True)
    return jnp.max(jnp.maximum(lo, hi), axis=0, keepdims=True)


def _attn_kernel(q_ref, k_ref, v_ref, lq1_ref, lk1_ref, lq2_ref, lk2_ref, sw_ref, o_ref, kx1_ref, kx2_ref, knorm_ref,
                 *, tq, seq, n_heads, lam_init):
    h = pl.program_id(1)
    qi = pl.program_id(2)
    tk = tq
    dh = LANES // 2
    lane2 = lax.broadcasted_iota(jnp.int32, (tq, LANES), 1)

    @pl.when(qi == 0)
    def _():
        def build(c, knorm):
            rows = pl.ds(pl.multiple_of(c * tk, tk), tk)
            kb = k_ref[rows, :].astype(F32)
            pos = c * tk + lax.broadcasted_iota(jnp.int32, (tk, LANES), 0)
            hi = (pos // LANES).astype(F32)
            lo = (pos % LANES).astype(F32)
            kx1_ref[rows, :] = jnp.where(lane2 < dh, kb, jnp.where(lane2 == dh, hi, jnp.where(lane2 == dh + 1, lo, 0.0))
                                         ).astype(BF16)
            kx2_ref[rows, :] = jnp.where(lane2 >= dh, kb, jnp.where(lane2 == 0, hi, jnp.where(lane2 == 1, lo, 0.0))
                                         ).astype(BF16)
            return jnp.maximum(knorm, _max_half_sqnorm(kb, lane2, dh))

        knorm_ref[...] = lax.fori_loop(0, seq // tk, build, jnp.zeros((1, 1), F32))

    q = q_ref[...].astype(F32) * dh ** -0.5
    hv = jnp.full((tq, LANES), h + 1, jnp.int32).astype(F32)
    slope = jnp.exp2(hv * (-8.0 / n_heads))
    q1 = jnp.where(lane2 < dh, q, jnp.where(lane2 == dh, slope * LANES, jnp.where(lane2 == dh + 1, slope, 0.0))
                   ).astype(BF16)
    q2 = jnp.where(lane2 >= dh, q, jnp.where(lane2 == 0, slope * LANES, jnp.where(lane2 == 1, slope, 0.0))
                   ).astype(BF16)

    col = lax.broadcasted_iota(jnp.int32, (1, tk), 1)
    row = lax.broadcasted_iota(jnp.int32, (tq, 1), 0)

    def step(kc, carry, masked):
        rows = pl.ds(pl.multiple_of(kc * tk, tk), tk)
        vb = v_ref[rows, :]
        maps = (0, 1)
        ss = [lax.dot_general(qc, kx_ref[rows, :], _NT, preferred_element_type=F32)
              for qc, kx_ref in ((q1, kx1_ref), (q2, kx2_ref))]
        if masked:
            ss = [jnp.where(col <= row, s, NEG) for s in ss]
        ms, ls, accs = carry[0::3], carry[1::3], carry[2::3]
        mns = [jnp.maximum(ms[c], jnp.max(ss[c], axis=-1, keepdims=True)) for c in maps]
        ps = [jnp.exp(ss[c] - mns[c]) for c in maps]
        als = [jnp.exp(ms[c] - mns[c]) for c in maps]
        ls = [als[c] * ls[c] + jnp.sum(ps[c], axis=-1, keepdims=True) for c in maps]
        pvs = [jnp.dot(ps[c].astype(BF16), vb, preferred_element_type=F32) for c in maps]
        accs = [als[c] * accs[c] + pvs[c] for c in maps]
        return (mns[0], ls[0], accs[0], mns[1], ls[1], accs[1])

    init = (jnp.full((tq, 1), NEG, F32), jnp.zeros((tq, 1), F32), jnp.zeros((tq, LANES), F32)) * 2
    bound = jnp.sqrt(_max_half_sqnorm(q, lane2, dh) * knorm_ref[...])
    reach = (SOFTMAX_UNDERFLOW + 2.0 * bound) / slope[:1, :1]
    q0 = jnp.full((1, 1), qi * tq, jnp.int32).astype(F32)
    first = jnp.clip(jnp.floor((q0 + 1.0 - reach) / tk), 0.0, q0 / tq)
    carry = lax.fori_loop(jnp.max(first.astype(jnp.int32)), qi, lambda kc, c: step(kc, c, False), init)
    m1, l1, a1, m2, l2, a2 = step(qi, carry, True)

    lam = (jnp.exp(jnp.sum(lq1_ref[...] * lk1_ref[...], axis=-1, keepdims=True))
           - jnp.exp(jnp.sum(lq2_ref[...] * lk2_ref[...], axis=-1, keepdims=True)) + lam_init)
    o = a1 / l1 - lam * (a2 / l2)
    o_ref[...] = (_rms(o, sw_ref[...]) * (1.0 - lam_init)).astype(o_ref.dtype)


def _diff_attention(qkv, lq1, lk1, lq2, lk2, subln_w, *, batch, seq, n_heads, tq, lam_init):
    T = qkv.shape[0]
    nq = seq // tq
    assert 8 % n_heads == 0 and seq <= 256 * LANES
    vec = lambda n: pl.BlockSpec((1, n), lambda b, h, i: (0, 0))
    return pl.pallas_call(
        functools.partial(_attn_kernel, tq=tq, seq=seq, n_heads=n_heads, lam_init=lam_init),
        grid=(batch, n_heads, nq),
        in_specs=[
            pl.BlockSpec((tq, LANES), lambda b, h, i: (b * nq + i, h)),
            pl.BlockSpec((seq, LANES), lambda b, h, i: (b, n_heads + h)),
            pl.BlockSpec((seq, LANES), lambda b, h, i: (b, 2 * n_heads + h)),
            vec(lq1.shape[1]), vec(lq1.shape[1]), vec(lq1.shape[1]), vec(lq1.shape[1]),
            vec(LANES),
        ],
        out_specs=pl.BlockSpec((tq, LANES), lambda b, h, i: (b * nq + i, h)),
        out_shape=jax.ShapeDtypeStruct((T, n_heads * LANES), BF16),
        scratch_shapes=[pltpu.VMEM((seq, LANES), BF16), pltpu.VMEM((seq, LANES), BF16), pltpu.VMEM((1, 1), F32)],
        compiler_params=_cparams("arbitrary", "arbitrary", "arbitrary"),
        name="diff_attn",
    )(qkv, qkv, qkv, lq1, lk1, lq2, lk2, subln_w)


def _split_bf16(x, n):
    parts = []
    for _ in range(n):
        p = x.astype(BF16)
        parts.append(p)
        x = x - p.astype(F32)
    return parts


def _dot(a, b):
    return jnp.dot(a.astype(BF16), b.astype(BF16), preferred_element_type=F32)


def _dot_exact_rhs(x, m_bf16, n):
    out = None
    for p in _split_bf16(x, n):
        t = jnp.dot(p, m_bf16, preferred_element_type=F32)
        out = t if out is None else out + t
    return out


def _chunk_cumsum(x, chunk):
    pos = lax.broadcasted_iota(jnp.int32, x.shape, 0) % chunk
    step = 1
    while step < chunk:
        x = x + jnp.where(pos >= step, pltpu.roll(x, step, 0), 0.0)
        step *= 2
    return x


def _rwkv_kernel(r_ref, k_ref, v_ref, lo_ref, w0_ref, a0_ref, kk_ref, ka_ref, rk_ref, lw_ref, lb_ref,
                 w2_ref, a2_ref, g2_ref, o_ref, st_ref, *, rows):
    L = RWKV_CHUNK
    N = LANES // 2
    P = LANES

    @pl.when(pl.program_id(2) == 0)
    def _():
        st_ref[...] = jnp.zeros(st_ref.shape, F32)

    r = r_ref[...].astype(F32)
    k = k_ref[...].astype(F32)
    v = v_ref[...].astype(F32)
    lo = lo_ref[...]
    lo_wa = lo[:, :P]
    wl = jnp.dot(jnp.tanh(lo_wa.astype(F32)).astype(BF16), w2_ref[...], preferred_element_type=F32)
    al = jnp.dot(lo_wa, a2_ref[...], preferred_element_type=F32)
    g = jnp.dot(jax.nn.sigmoid(lo[:, P:].astype(F32)).astype(BF16), g2_ref[...], preferred_element_type=F32)

    wx = -(w0_ref[...] + wl)
    softplus = jnp.maximum(wx, 0.0) + jnp.log1p(jnp.exp(-jnp.abs(wx)))
    logdecay = -jnp.exp(-softplus - 0.5)
    a = jax.nn.sigmoid(a0_ref[...] + al)

    ri = lax.broadcasted_iota(jnp.int32, (P, P), 0)
    ci = lax.broadcasted_iota(jnp.int32, (P, P), 1)
    ones_bd = ((ri // N) == (ci // N)).astype(BF16)
    eye = (ri == ci).astype(F32)
    strict = ((ri % L) > (ci % L)).astype(F32)
    lower = ((ri % L) >= (ci % L)).astype(F32)
    head0 = lax.broadcasted_iota(jnp.int32, (1, P), 1) < N

    kkr = k * kk_ref[...]
    norm = jnp.sqrt(_dot_exact_rhs(kkr * kkr, ones_bd, HEAD_SUM_PIECES))
    kk = kkr / jnp.maximum(norm, 1e-12)
    kf = k * (1.0 + (a - 1.0) * ka_ref[...])
    avec = -kk
    bvec = kk * a

    def bd(x):
        x = x.astype(BF16)
        z = jnp.zeros_like(x)
        return jnp.concatenate([jnp.where(head0, x, z), jnp.where(head0, z, x)], axis=0)

    nc = rows // L
    chunks = range(nc)
    sls = [slice(c * L, (c + 1) * L) for c in chunks]
    cum_all = _chunk_cumsum(logdecay, L)
    cum = [cum_all[s] for s in sls]
    tot = [cm[L - 1:L, :] for cm in cum]
    at = [bd(avec[s] * jnp.exp(cm - logdecay[s])) for s, cm in zip(sls, cum)]
    rt = [bd(r[s] * jnp.exp(cm)) for s, cm in zip(sls, cum)]
    inv = [jnp.exp(-cm) for cm in cum]
    bt = [bd(bvec[s] * iv) for s, iv in zip(sls, inv)]
    kt = [bd(kf[s] * iv) for s, iv in zip(sls, inv)]
    rest = [jnp.exp(t - cm) for t, cm in zip(tot, cum)]
    blkl_t = [jnp.concatenate([bd(bvec[s] * rs), bd(kf[s] * rs)], axis=1).astype(F32).T.astype(BF16)
              for s, rs in zip(sls, rest)]
    vd = [bd(v[s]) for s in sls]

    amat = [lax.dot_general(jnp.concatenate([at[c], rt[c]], axis=0), jnp.concatenate([bt[c], kt[c]], axis=0), _NT,
                            preferred_element_type=F32) for c in chunks]
    a_ab = [m[:P, :P] * strict for m in amat]
    a_ak = [(m[:P, P:] * strict).astype(BF16) for m in amat]
    a_rb = [(m[P:, :P] * lower).astype(BF16) for m in amat]
    a_rk = [(m[P:, P:] * lower).astype(BF16) for m in amat]

    tinv = [eye + n for n in a_ab]
    pw = [_dot(n, n) for n in a_ab]
    for _ in range(int(math.log2(L)) - 2):
        both = [_dot(pw[c], jnp.concatenate([pw[c], tinv[c]], axis=1)) for c in chunks]
        pw = [b[:, :P] for b in both]
        tinv = [tinv[c] + both[c][:, P:] for c in chunks]
    tinv = [tinv[c] + _dot(pw[c], tinv[c]) for c in chunks]

    avd = [_dot(jnp.concatenate([a_ak[c], a_rk[c], blkl_t[c][P:]], axis=0), vd[c]) for c in chunks]
    hat = [_dot(tinv[c], jnp.concatenate([at[c], avd[c][:P].astype(BF16)], axis=1)) for c in chunks]
    mix = [_dot(jnp.concatenate([a_rb[c], blkl_t[c][:P]], axis=0), hat[c]) for c in chunks]
    rhat = [rt[c].astype(F32) + mix[c][:P, :P] for c in chunks]
    ohat = [mix[c][:P, P:] + avd[c][P:2 * P] for c in chunks]
    mmat = [eye * jnp.exp(tot[c]) + mix[c][P:, :P] for c in chunks]
    cmat = [mix[c][P:, P:] + avd[c][2 * P:] for c in chunks]
    rhat = [x.astype(BF16) for x in rhat]
    mmat = [x.astype(BF16) for x in mmat]

    states = []
    st = st_ref[...]
    for c in chunks:
        states.append(st.astype(BF16))
        st = _dot(mmat[c], states[c]) + cmat[c]
    st_ref[...] = st
    outs = []
    for c in chunks:
        od = _dot(rhat[c], states[c]) + ohat[c]
        outs.append(od[:L] + od[L:])
    wkv = jnp.concatenate(outs, axis=0)

    inv_n = 1.0 / N
    mu = _dot_exact_rhs(wkv, ones_bd, HEAD_SUM_PIECES) * inv_n
    xc = wkv - mu
    var = _dot_exact_rhs(xc * xc, ones_bd, HEAD_SUM_PIECES) * inv_n
    o = xc * lax.rsqrt(var + LNX_EPS) * lw_ref[...] + lb_ref[...]
    bonus = _dot_exact_rhs(r * kf * rk_ref[...], ones_bd, HEAD_SUM_PIECES) * v
    o_ref[...] = ((o + bonus) * g).astype(o_ref.dtype)


def _rwkv(cols, w0, a0, k_k, k_a, r_k, lnx_w, lnx_b, w2p, a2p, g2p, *, batch, seq, n_pairs, rows):
    T = cols.shape[0]
    nt = seq // rows
    lora_blk = (cols.shape[1] - 3 * n_pairs * LANES) // LANES
    row_map = lambda off: (lambda b, p, t: (b * nt + t, off + p))
    vec = pl.BlockSpec((1, LANES), lambda b, p, t: (0, p))
    return pl.pallas_call(
        functools.partial(_rwkv_kernel, rows=rows),
        grid=(batch, n_pairs, nt),
        in_specs=[
            pl.BlockSpec((rows, LANES), row_map(0)),
            pl.BlockSpec((rows, LANES), row_map(n_pairs)),
            pl.BlockSpec((rows, LANES), row_map(2 * n_pairs)),
            pl.BlockSpec((rows, lora_blk * LANES), lambda b, p, t: (b * nt + t, 3 * n_pairs // lora_blk)),
            vec, vec, vec, vec, vec, vec, vec,
            pl.BlockSpec((LANES, LANES), lambda b, p, t: (0, p)),
            pl.BlockSpec((LANES, LANES), lambda b, p, t: (0, p)),
            pl.BlockSpec((2 * LANES, LANES), lambda b, p, t: (0, p)),
        ],
        out_specs=pl.BlockSpec((rows, LANES), row_map(0)),
        out_shape=jax.ShapeDtypeStruct((T, n_pairs * LANES), BF16),
        scratch_shapes=[pltpu.VMEM((LANES, LANES), F32)],
        compiler_params=_cparams("arbitrary", "arbitrary", "arbitrary"),
        name="rwkv7",
    )(cols, cols, cols, cols, w0, a0, k_k, k_a, r_k, lnx_w, lnx_b, w2p, a2p, g2p)


def _out_proj_kernel(x_ref, oa_ref, or_ref, wa_ref, wr_ref, nw_ref, x1_ref, h2_ref):
    x1 = (x_ref[...] + jnp.dot(oa_ref[...], wa_ref[...], preferred_element_type=F32)
          + jnp.dot(or_ref[...], wr_ref[...], preferred_element_type=F32))
    x1_ref[...] = x1
    h2_ref[...] = _rms(x1, nw_ref[...]).astype(BF16)


def _out_proj(x2, oa, orw, wa, wr, nw, *, tm):
    T, D = x2.shape
    W = oa.shape[1]
    return pl.pallas_call(
        _out_proj_kernel,
        grid=(T // tm,),
        in_specs=[
            pl.BlockSpec((tm, D), lambda i: (i, 0)),
            pl.BlockSpec((tm, W), lambda i: (i, 0)),
            pl.BlockSpec((tm, W), lambda i: (i, 0)),
            pl.BlockSpec((W, D), lambda i: (0, 0)),
            pl.BlockSpec((W, D), lambda i: (0, 0)),
            pl.BlockSpec((1, D), lambda i: (0, 0)),
        ],
        out_specs=[pl.BlockSpec((tm, D), lambda i: (i, 0)), pl.BlockSpec((tm, D), lambda i: (i, 0))],
        out_shape=[jax.ShapeDtypeStruct((T, D), F32), jax.ShapeDtypeStruct((T, D), BF16)],
        compiler_params=_cparams("arbitrary"),
        name="out_proj",
    )(x2, oa, orw, wa, wr, nw)


def _topk_rows(curs, k):
    curs = list(curs)
    rows = [[] for _ in curs]
    for _ in range(k):
        ms = [jnp.max(cur, axis=0, keepdims=True) for cur in curs]
        curs = [jnp.where(cur == m, -jnp.inf, cur) for cur, m in zip(curs, ms)]
        for r, m in zip(rows, ms):
            r.append(m)
    return rows


_NLIST = PEER_TOPK + 1
_CANDS = [(a, b) for a in range(_NLIST) for b in range(_NLIST) if (a + 1) * (b + 1) <= _NLIST]
_NCAND = -(-len(_CANDS) // 8) * 8


def _route_kernel(h2_ref, wq_ref, keys_ref, thr_ref, e1_ref, e2_ref, sc_ref, cand_ref, *, n_heads):
    q = jnp.dot(h2_ref[...], wq_ref[...], preferred_element_type=F32).astype(BF16)
    for g in range(2 * n_heads):
        sc_ref[g] = lax.dot_general(keys_ref[g], q[:, g * LANES:(g + 1) * LANES], _NT,
                                    preferred_element_type=F32)
    cand_ref[...] = jnp.full(cand_ref.shape, -jnp.inf, F32)

    def per_head(h, _):
        s1 = sc_ref[2 * h]
        s2 = sc_ref[2 * h + 1]
        v1, v2 = _topk_rows((s1, s2), _NLIST)
        for n, (ia, ib) in enumerate(_CANDS):
            cand_ref[n:n + 1, :] = v1[ia] + v2[ib]
        cand = cand_ref[...]
        top, = _topk_rows((cand,), _NLIST)
        tau = 0.5 * (top[PEER_TOPK - 1] + top[PEER_TOPK])
        inv_z = 1.0 / jnp.sum(jnp.where(cand > tau, jnp.exp(cand - (v1[0] + v2[0])), 0.0), axis=0, keepdims=True)
        e1_ref[h] = jnp.exp(s1 - v1[0])
        e2_ref[h] = jnp.exp(s2 - v2[0]) * inv_z
        thr_ref[h] = jnp.exp((tau - s1) - v2[0]) * inv_z
        return 0

    lax.fori_loop(0, n_heads, per_head, 0)


def _route(h2, wq, keys, *, tt, n_heads):
    T, D = h2.shape
    nk = keys.shape[1]
    big = jax.ShapeDtypeStruct((n_heads, nk, T), F32)
    blk = pl.BlockSpec((n_heads, nk, tt), lambda i: (0, 0, i))
    return pl.pallas_call(
        functools.partial(_route_kernel, n_heads=n_heads),
        grid=(T // tt,),
        in_specs=[
            pl.BlockSpec((tt, D), lambda i: (i, 0)),
            pl.BlockSpec(wq.shape, lambda i: (0, 0)),
            pl.BlockSpec(keys.shape, lambda i: (0, 0, 0)),
        ],
        out_specs=[blk, blk, blk],
        out_shape=[big, big, big],
        scratch_shapes=[pltpu.VMEM((2 * n_heads, nk, tt), F32), pltpu.VMEM((_NCAND, tt), F32)],
        compiler_params=_cparams("arbitrary"),
        name="peer_route",
    )(h2, wq, keys)


def _peer_kernel(h2_ref, down_ref, upt_ref, thr_ref, e1_ref, e2_ref, o_ref, act_ref, *, n_heads, nk):
    @pl.when(pl.program_id(1) == 0)
    def _():
        o_ref[...] = jnp.zeros(o_ref.shape, F32)

    n_i1 = down_ref.shape[0] // nk
    per_chunk = n_i1 // PEER_Z_CHUNKS
    for q in range(PEER_Z_CHUNKS):
        zr = slice(q * per_chunk * nk, (q + 1) * per_chunk * nk)
        z = lax.dot_general(down_ref[zr, :], h2_ref[...], _NT, preferred_element_type=F32)
        for ic in range(per_chunk):
            ii = q * per_chunk + ic
            rs = slice(ii * nk, (ii + 1) * nk)
            gate = None
            for h in range(n_heads):
                e2 = e2_ref[h]
                val = jnp.where(e2 >= thr_ref[h, ii:ii + 1, :], e2, 0.0) * e1_ref[h, ii:ii + 1, :]
                gate = val if gate is None else gate + val
            zz = z[ic * nk:(ic + 1) * nk]
            gelu = zz * (1.0 + lax.erf(zz * (math.sqrt(0.5) / PEER_DOWN_SCALE)))
            act_ref[rs, :] = (gelu * gate).astype(FP8)
    d_rows = upt_ref.shape[0] // PEER_UP_CHUNKS
    e_rows = down_ref.shape[0] // PEER_Z_CHUNKS
    for q in range(PEER_Z_CHUNKS):
        es = slice(q * e_rows, (q + 1) * e_rows)
        for u in range(PEER_UP_CHUNKS):
            us = slice(u * d_rows, (u + 1) * d_rows)
            o_ref[us, :] += jnp.dot(upt_ref[us, es], act_ref[es, :], preferred_element_type=F32)


def _peer(h2, down, upt, thr, e1, e2, *, tt, et, n_heads, nk):
    T, D = h2.shape
    E = down.shape[0]
    i1_blk = et // nk
    row_blk = pl.BlockSpec((n_heads, i1_blk, tt), lambda i, j: (0, j, i))
    once = pl.Buffered(1)
    return pl.pallas_call(
        functools.partial(_peer_kernel, n_heads=n_heads, nk=nk),
        grid=(T // tt, E // et),
        in_specs=[
            pl.BlockSpec((tt, D), lambda i, j: (i, 0), pipeline_mode=once),
            pl.BlockSpec((et, D), lambda i, j: (j, 0)),
            pl.BlockSpec((D, et), lambda i, j: (0, j)),
            row_blk, row_blk,
            pl.BlockSpec((n_heads, nk, tt), lambda i, j: (0, 0, i), pipeline_mode=once),
        ],
        out_specs=pl.BlockSpec((D, tt), lambda i, j: (0, i)),
        out_shape=jax.ShapeDtypeStruct((D, T), F32),
        scratch_shapes=[pltpu.VMEM((et, tt), FP8)],
        compiler_params=_cparams("arbitrary", "arbitrary"),
        name="peer_experts",
    )(h2, down, upt, thr, e1, e2)


def _final_kernel(x1_ref, pt_ref, nw_ref, o_ref):
    o_ref[...] = _rms(x1_ref[...] + pt_ref[...].T * (1.0 / PEER_ACT_SCALE), nw_ref[...])


def _final(x1, peer_t, nw, *, tm):
    T, D = x1.shape
    return pl.pallas_call(
        _final_kernel,
        grid=(T // tm,),
        in_specs=[
            pl.BlockSpec((tm, D), lambda i: (i, 0)),
            pl.BlockSpec((D, tm), lambda i: (0, i)),
            pl.BlockSpec((1, D), lambda i: (0, 0)),
        ],
        out_specs=pl.BlockSpec((tm, D), lambda i: (i, 0)),
        out_shape=jax.ShapeDtypeStruct((T, D), F32),
        compiler_params=_cparams("arbitrary"),
        name="final_norm",
    )(x1, peer_t, nw)


def _tiles(seq, tokens):
    return dict(
        proj_tm=min(1024, seq),
        attn_tq=min(512, seq),
        rwkv_rows=min(512, seq),
        out_tm=min(512, tokens),
        route_tt=min(256, tokens),
        peer_tt=min(512, tokens),
        peer_et=2048,
        final_tm=min(512, tokens),
    )


def _pad_rows(w, rows_before, rows_total):
    return jnp.pad(w, ((rows_before, rows_total - rows_before - w.shape[0]), (0, 0)))


def kernel(x, norm1_w, w_in, tshift_mu, w0, w2, a0, a2, g2, k_k, k_a, r_k, lnx_w, lnx_b, lambda_q1, lambda_k1,
           lambda_q2, lambda_k2, subln_w, w_out, norm2_w, peer_w_query, peer_sub_keys, peer_down, peer_up,
           norm_f_w):
    B, S, D = x.shape
    T = B * S
    depth = norm1_w.shape[0]
    dh = lambda_q1.shape[-1]
    n_rheads, rhead = r_k.shape[1], r_k.shape[2]
    rw = n_rheads * rhead
    n_pairs = rw // LANES
    w_lora, a_lora, g_lora = w2.shape[1], a2.shape[1], g2.shape[1]
    da_cols = w_in.shape[2] - (3 * rw + w_lora + a_lora + g_lora)
    n_aheads = da_cols // (3 * 2 * dh)
    p_heads, nk = peer_sub_keys.shape[1], peer_sub_keys.shape[3]
    assert 2 * dh == LANES and 2 * rhead == LANES and rhead == RWKV_CHUNK and nk == LANES
    assert w_lora + a_lora == LANES and g_lora <= 2 * LANES
    t = _tiles(S, T)

    xt = x.reshape(T, D)
    for l in range(depth):
        row = lambda p: p[l].reshape(1, -1)
        lam_init = 0.8 - 0.6 * math.exp(-0.3 * l)

        w_attn = w_in[l][:, :da_cols].astype(BF16)
        rcols = w_in[l].shape[1] - da_cols
        rpad = 3 * rw + 3 * LANES - rcols
        w_rwkv = jnp.pad(w_in[l][:, da_cols:], ((0, 0), (0, rpad))).astype(BF16)
        mu = jnp.pad(row(tshift_mu), ((0, 0), (0, rpad)))
        w2p = _pad_rows(w2[l], 0, LANES).astype(BF16)
        a2p = _pad_rows(a2[l], w_lora, LANES).astype(BF16)
        g2p = _pad_rows(g2[l], 0, 2 * LANES).astype(BF16)

        qkv = _in_proj(xt, row(norm1_w), w_attn, None, seq=S, tm=t["proj_tm"], tn=da_cols // 3)
        rcol = _in_proj(xt, row(norm1_w), w_rwkv, mu, seq=S, tm=t["proj_tm"], tn=w_rwkv.shape[1] // 3)

        o_attn = _diff_attention(qkv, row(lambda_q1), row(lambda_k1), row(lambda_q2), row(lambda_k2),
                                 row(subln_w), batch=B, seq=S, n_heads=n_aheads, tq=t["attn_tq"],
                                 lam_init=lam_init)
        o_rwkv = _rwkv(rcol, row(w0), row(a0), row(k_k), row(k_a), row(r_k), row(lnx_w), row(lnx_b),
                       w2p, a2p, g2p, batch=B, seq=S, n_pairs=n_pairs, rows=t["rwkv_rows"])

        aw = o_attn.shape[1]
        x1, h2 = _out_proj(xt, o_attn, o_rwkv, w_out[l][:aw].astype(BF16), w_out[l][aw:].astype(BF16),
                           row(norm2_w), tm=t["out_tm"])

        keys = peer_sub_keys[l].reshape(2 * p_heads, nk, -1).astype(BF16)
        thr, e1, e2 = _route(h2, peer_w_query[l].astype(BF16), keys, tt=t["route_tt"], n_heads=p_heads)
        peer_t = _peer(h2.astype(FP8), (peer_down[l] * PEER_DOWN_SCALE).astype(FP8), peer_up[l].T.astype(FP8), thr, e1, e2,
                       tt=t["peer_tt"], et=t["peer_et"], n_heads=p_heads, nk=nk)
        if l + 1 < depth:
            xt = x1 + peer_t.T * (1.0 / PEER_ACT_SCALE)
        else:
            return _final(x1, peer_t, norm_f_w.reshape(1, -1), tm=t["final_tm"]).reshape(B, S, D)
```

```python
import functools
import math

import jax
import jax.numpy as jnp
from jax import lax
from jax.experimental import pallas as pl
from jax.experimental.pallas import tpu as pltpu

F32 = jnp.float32
BF16 = jnp.bfloat16
FP8 = jnp.float8_e4m3fn

LANES = 128
RMS_EPS = 1e-6
LNX_EPS = 64e-5
PEER_TOPK = 16
RWKV_CHUNK = 64
HEAD_SUM_PIECES = 1
PEER_DOWN_SCALE = 32.0
PEER_ACT_SCALE = 2.0 * PEER_DOWN_SCALE
PEER_Z_CHUNKS = 2
PEER_UP_CHUNKS = 4
VMEM_LIMIT = 56 * 1024 * 1024
NEG = -1e30
SOFTMAX_UNDERFLOW = 110.0
FIXED_SHIFT_SPREAD = 40.0

_NT = (((1,), (1,)), ((), ()))


def _cparams(*sem):
    return pltpu.CompilerParams(dimension_semantics=sem, vmem_limit_bytes=VMEM_LIMIT)


def _rms(x, w):
    ms = jnp.mean(x * x, axis=-1, keepdims=True)
    return x * lax.rsqrt(ms + RMS_EPS) * w


def _proj_kernel(x_ref, nw_ref, w_ref, o_ref, h_ref):
    @pl.when(pl.program_id(1) == 0)
    def _():
        h_ref[...] = _rms(x_ref[...], nw_ref[...]).astype(BF16)

    o_ref[...] = jnp.dot(h_ref[...], w_ref[...], preferred_element_type=F32).astype(o_ref.dtype)


def _proj_shift_kernel(x_ref, nw_ref, w_ref, mu_ref, o_ref, h_ref, carry_ref, *, tiles_per_seq):
    i = pl.program_id(0)
    j = pl.program_id(1)

    @pl.when(j == 0)
    def _():
        h_ref[...] = _rms(x_ref[...], nw_ref[...]).astype(BF16)

    @pl.when(i % tiles_per_seq == 0)
    def _():
        carry_ref[j] = jnp.zeros(carry_ref.shape[1:], F32)

    p = jnp.dot(h_ref[...], w_ref[...], preferred_element_type=F32)
    tm = p.shape[0]
    rolled = pltpu.roll(p, 1, 0)
    row = lax.broadcasted_iota(jnp.int32, p.shape, 0)
    prev = jnp.where(row == 0, carry_ref[j][7:8, :], rolled)
    o_ref[...] = (p + (prev - p) * mu_ref[...]).astype(o_ref.dtype)
    carry_ref[j] = p[tm - 8:tm, :]


def _in_proj(x2, nw, w, mu, *, seq, tm, tn):
    T, D = x2.shape
    N = w.shape[1]
    grid = (T // tm, N // tn)
    in_specs = [
        pl.BlockSpec((tm, D), lambda i, j: (i, 0)),
        pl.BlockSpec((1, D), lambda i, j: (0, 0)),
        pl.BlockSpec((D, tn), lambda i, j: (0, j)),
    ]
    args = [x2, nw, w]
    scratch = [pltpu.VMEM((tm, D), BF16)]
    if mu is None:
        body = _proj_kernel
    else:
        body = functools.partial(_proj_shift_kernel, tiles_per_seq=seq // tm)
        in_specs.append(pl.BlockSpec((1, tn), lambda i, j: (0, j)))
        args.append(mu)
        scratch.append(pltpu.VMEM((N // tn, 8, tn), F32))
    return pl.pallas_call(
        body,
        grid=grid,
        in_specs=in_specs,
        out_specs=pl.BlockSpec((tm, tn), lambda i, j: (i, j)),
        out_shape=jax.ShapeDtypeStruct((T, N), BF16),
        scratch_shapes=scratch,
        compiler_params=_cparams("arbitrary", "arbitrary"),
        name="in_proj" if mu is None else "in_proj_shift",
    )(*args)


def _half_sqnorms(x, lane2, dh):
    sq = x * x
    lo = jnp.sum(jnp.where(lane2 < dh, sq, 0.0), axis=-1, keepdims=True)
    hi = jnp.sum(jnp.where(lane2 >= dh, sq, 0.0), axis=-1, keepdims=True)
    per_row = jnp.maximum(lo, hi)
    return per_row, jnp.max(per_row, axis=0, keepdims=True)


def _attn_kernel(q_ref, k_ref, v_ref, lq1_ref, lk1_ref, lq2_ref, lk2_ref, sw_ref, o_ref, kx1_ref, kx2_ref, knorm_ref,
                 *, tq, seq, n_heads, lam_init):
    h = pl.program_id(1)
    qi = pl.program_id(2)
    tk = tq
    dh = LANES // 2
    lane2 = lax.broadcasted_iota(jnp.int32, (tq, LANES), 1)

    @pl.when(qi == 0)
    def _():
        def build(c, knorm):
            rows = pl.ds(pl.multiple_of(c * tk, tk), tk)
            kb = k_ref[rows, :].astype(F32)
            pos = c * tk + lax.broadcasted_iota(jnp.int32, (tk, LANES), 0)
            hi = (pos // LANES).astype(F32)
            lo = (pos % LANES).astype(F32)
            kx1_ref[rows, :] = jnp.where(lane2 < dh, kb, jnp.where(lane2 == dh, hi, jnp.where(lane2 == dh + 1, lo, 0.0))
                                         ).astype(BF16)
            kx2_ref[rows, :] = jnp.where(lane2 >= dh, kb, jnp.where(lane2 == 0, hi, jnp.where(lane2 == 1, lo, 0.0))
                                         ).astype(BF16)
            return jnp.maximum(knorm, _half_sqnorms(kb, lane2, dh)[1])

        knorm_ref[...] = lax.fori_loop(0, seq // tk, build, jnp.zeros((1, 1), F32))

    q = q_ref[...].astype(F32) * dh ** -0.5
    hv = jnp.full((tq, LANES), h + 1, jnp.int32).astype(F32)
    slope = jnp.exp2(hv * (-8.0 / n_heads))
    q1 = jnp.where(lane2 < dh, q, jnp.where(lane2 == dh, slope * LANES, jnp.where(lane2 == dh + 1, slope, 0.0))
                   ).astype(BF16)
    q2 = jnp.where(lane2 >= dh, q, jnp.where(lane2 == 0, slope * LANES, jnp.where(lane2 == 1, slope, 0.0))
                   ).astype(BF16)

    col = lax.broadcasted_iota(jnp.int32, (1, tk), 1)
    row = lax.broadcasted_iota(jnp.int32, (tq, 1), 0)

    def step(kc, carry, masked):
        rows = pl.ds(pl.multiple_of(kc * tk, tk), tk)
        vb = v_ref[rows, :]
        maps = (0, 1)
        ss = [lax.dot_general(qc, kx_ref[rows, :], _NT, preferred_element_type=F32)
              for qc, kx_ref in ((q1, kx1_ref), (q2, kx2_ref))]
        if masked:
            ss = [jnp.where(col <= row, s, NEG) for s in ss]
        ms, ls, accs = carry[0::3], carry[1::3], carry[2::3]
        mns = [jnp.maximum(ms[c], jnp.max(ss[c], axis=-1, keepdims=True)) for c in maps]
        ps = [jnp.exp(ss[c] - mns[c]) for c in maps]
        als = [jnp.exp(ms[c] - mns[c]) for c in maps]
        ls = [als[c] * ls[c] + jnp.sum(ps[c], axis=-1, keepdims=True) for c in maps]
        pvs = [jnp.dot(ps[c].astype(BF16), vb, preferred_element_type=F32) for c in maps]
        accs = [als[c] * accs[c] + pvs[c] for c in maps]
        return (mns[0], ls[0], accs[0], mns[1], ls[1], accs[1])

    row_sq, tile_sq = _half_sqnorms(q, lane2, dh)
    bound = jnp.sqrt(tile_sq * knorm_ref[...])
    reach = (SOFTMAX_UNDERFLOW + 2.0 * bound) / slope[:1, :1]
    q0 = jnp.full((1, 1), qi * tq, jnp.int32).astype(F32)
    first = jnp.max(jnp.clip(jnp.floor((q0 + 1.0 - reach) / tk), 0.0, q0 / tq).astype(jnp.int32))

    def online_softmax():
        init = (jnp.full((tq, 1), NEG, F32), jnp.zeros((tq, 1), F32), jnp.zeros((tq, LANES), F32)) * 2
        carry = lax.fori_loop(first, qi, lambda kc, c: step(kc, c, False), init)
        _, l1, a1, _, l2, a2 = step(qi, carry, True)
        return l1, a1, l2, a2

    slope_col = jnp.exp2(jnp.full((tq, 1), h + 1, jnp.int32).astype(F32) * (-8.0 / n_heads))
    pos_col = (qi * tq + row).astype(F32)
    shift = jnp.sqrt(row_sq * knorm_ref[...]) + slope_col * pos_col

    def fixed_step(kc, carry, masked):
        rows = pl.ds(pl.multiple_of(kc * tk, tk), tk)
        vb = v_ref[rows, :]
        maps = (0, 1)
        ps = [jnp.exp(lax.dot_general(qc, kx_ref[rows, :], _NT, preferred_element_type=F32) - shift)
              for qc, kx_ref in ((q1, kx1_ref), (q2, kx2_ref))]
        if masked:
            ps = [jnp.where(col <= row, p, 0.0) for p in ps]
        ls = [carry[2 * c] + jnp.sum(ps[c], axis=-1, keepdims=True) for c in maps]
        accs = [carry[2 * c + 1] + jnp.dot(ps[c].astype(BF16), vb, preferred_element_type=F32) for c in maps]
        return (ls[0], accs[0], ls[1], accs[1])

    def fixed_shift_softmax():
        init = (jnp.zeros((tq, 1), F32), jnp.zeros((tq, LANES), F32)) * 2
        carry = lax.fori_loop(first, qi, lambda kc, c: fixed_step(kc, c, False), init)
        return fixed_step(qi, carry, True)

    l1, a1, l2, a2 = lax.cond(jnp.max(bound) * 2.0 <= FIXED_SHIFT_SPREAD, fixed_shift_softmax, online_softmax)

    lam = (jnp.exp(jnp.sum(lq1_ref[...] * lk1_ref[...], axis=-1, keepdims=True))
           - jnp.exp(jnp.sum(lq2_ref[...] * lk2_ref[...], axis=-1, keepdims=True)) + lam_init)
    o = a1 / l1 - lam * (a2 / l2)
    o_ref[...] = (_rms(o, sw_ref[...]) * (1.0 - lam_init)).astype(o_ref.dtype)


def _diff_attention(qkv, lq1, lk1, lq2, lk2, subln_w, *, batch, seq, n_heads, tq, lam_init):
    T = qkv.shape[0]
    nq = seq // tq
    assert 8 % n_heads == 0 and seq <= 256 * LANES
    vec = lambda n: pl.BlockSpec((1, n), lambda b, h, i: (0, 0))
    return pl.pallas_call(
        functools.partial(_attn_kernel, tq=tq, seq=seq, n_heads=n_heads, lam_init=lam_init),
        grid=(batch, n_heads, nq),
        in_specs=[
            pl.BlockSpec((tq, LANES), lambda b, h, i: (b * nq + i, h)),
            pl.BlockSpec((seq, LANES), lambda b, h, i: (b, n_heads + h)),
            pl.BlockSpec((seq, LANES), lambda b, h, i: (b, 2 * n_heads + h)),
            vec(lq1.shape[1]), vec(lq1.shape[1]), vec(lq1.shape[1]), vec(lq1.shape[1]),
            vec(LANES),
        ],
        out_specs=pl.BlockSpec((tq, LANES), lambda b, h, i: (b * nq + i, h)),
        out_shape=jax.ShapeDtypeStruct((T, n_heads * LANES), BF16),
        scratch_shapes=[pltpu.VMEM((seq, LANES), BF16), pltpu.VMEM((seq, LANES), BF16), pltpu.VMEM((1, 1), F32)],
        compiler_params=_cparams("arbitrary", "arbitrary", "arbitrary"),
        name="diff_attn",
    )(qkv, qkv, qkv, lq1, lk1, lq2, lk2, subln_w)


def _split_bf16(x, n):
    parts = []
    for _ in range(n):
        p = x.astype(BF16)
        parts.append(p)
        x = x - p.astype(F32)
    return parts


def _dot(a, b):
    return jnp.dot(a.astype(BF16), b.astype(BF16), preferred_element_type=F32)


def _dot_exact_rhs(x, m_bf16, n):
    out = None
    for p in _split_bf16(x, n):
        t = jnp.dot(p, m_bf16, preferred_element_type=F32)
        out = t if out is None else out + t
    return out


def _chunk_cumsum(x, chunk):
    pos = lax.broadcasted_iota(jnp.int32, x.shape, 0) % chunk
    step = 1
    while step < chunk:
        x = x + jnp.where(pos >= step, pltpu.roll(x, step, 0), 0.0)
        step *= 2
    return x


def _rwkv_kernel(r_ref, k_ref, v_ref, lo_ref, w0_ref, a0_ref, kk_ref, ka_ref, rk_ref, lw_ref, lb_ref,
                 w2_ref, a2_ref, g2_ref, o_ref, st_ref, *, rows):
    L = RWKV_CHUNK
    N = LANES // 2
    P = LANES

    @pl.when(pl.program_id(2) == 0)
    def _():
        st_ref[...] = jnp.zeros(st_ref.shape, F32)

    r = r_ref[...].astype(F32)
    k = k_ref[...].astype(F32)
    v = v_ref[...].astype(F32)
    lo = lo_ref[...]
    lo_wa = lo[:, :P]
    wl = jnp.dot(jnp.tanh(lo_wa.astype(F32)).astype(BF16), w2_ref[...], preferred_element_type=F32)
    al = jnp.dot(lo_wa, a2_ref[...], preferred_element_type=F32)
    g = jnp.dot(jax.nn.sigmoid(lo[:, P:].astype(F32)).astype(BF16), g2_ref[...], preferred_element_type=F32)

    wx = -(w0_ref[...] + wl)
    softplus = jnp.maximum(wx, 0.0) + jnp.log1p(jnp.exp(-jnp.abs(wx)))
    logdecay = -jnp.exp(-softplus - 0.5)
    a = jax.nn.sigmoid(a0_ref[...] + al)

    ri = lax.broadcasted_iota(jnp.int32, (P, P), 0)
    ci = lax.broadcasted_iota(jnp.int32, (P, P), 1)
    ones_bd = ((ri // N) == (ci // N)).astype(BF16)
    eye = (ri == ci).astype(F32)
    strict = ((ri % L) > (ci % L)).astype(F32)
    lower = ((ri % L) >= (ci % L)).astype(F32)
    head0 = lax.broadcasted_iota(jnp.int32, (1, P), 1) < N

    kkr = k * kk_ref[...]
    norm = jnp.sqrt(_dot_exact_rhs(kkr * kkr, ones_bd, HEAD_SUM_PIECES))
    kk = kkr / jnp.maximum(norm, 1e-12)
    kf = k * (1.0 + (a - 1.0) * ka_ref[...])
    avec = -kk
    bvec = kk * a

    def bd(x):
        x = x.astype(BF16)
        z = jnp.zeros_like(x)
        return jnp.concatenate([jnp.where(head0, x, z), jnp.where(head0, z, x)], axis=0)

    nc = rows // L
    chunks = range(nc)
    sls = [slice(c * L, (c + 1) * L) for c in chunks]
    cum_all = _chunk_cumsum(logdecay, L)
    cum = [cum_all[s] for s in sls]
    tot = [cm[L - 1:L, :] for cm in cum]
    at = [bd(avec[s] * jnp.exp(cm - logdecay[s])) for s, cm in zip(sls, cum)]
    rt = [bd(r[s] * jnp.exp(cm)) for s, cm in zip(sls, cum)]
    inv = [jnp.exp(-cm) for cm in cum]
    bt = [bd(bvec[s] * iv) for s, iv in zip(sls, inv)]
    kt = [bd(kf[s] * iv) for s, iv in zip(sls, inv)]
    rest = [jnp.exp(t - cm) for t, cm in zip(tot, cum)]
    blkl_t = [jnp.concatenate([bd(bvec[s] * rs), bd(kf[s] * rs)], axis=1).astype(F32).T.astype(BF16)
              for s, rs in zip(sls, rest)]
    vd = [bd(v[s]) for s in sls]

    amat = [lax.dot_general(jnp.concatenate([at[c], rt[c]], axis=0), jnp.concatenate([bt[c], kt[c]], axis=0), _NT,
                            preferred_element_type=F32) for c in chunks]
    a_ab = [m[:P, :P] * strict for m in amat]
    a_ak = [(m[:P, P:] * strict).astype(BF16) for m in amat]
    a_rb = [(m[P:, :P] * lower).astype(BF16) for m in amat]
    a_rk = [(m[P:, P:] * lower).astype(BF16) for m in amat]

    tinv = [eye + n for n in a_ab]
    pw = [_dot(n, n) for n in a_ab]
    for _ in range(int(math.log2(L)) - 2):
        both = [_dot(pw[c], jnp.concatenate([pw[c], tinv[c]], axis=1)) for c in chunks]
        pw = [b[:, :P] for b in both]
        tinv = [tinv[c] + both[c][:, P:] for c in chunks]
    tinv = [tinv[c] + _dot(pw[c], tinv[c]) for c in chunks]

    avd = [_dot(jnp.concatenate([a_ak[c], a_rk[c], blkl_t[c][P:]], axis=0), vd[c]) for c in chunks]
    hat = [_dot(tinv[c], jnp.concatenate([at[c], avd[c][:P].astype(BF16)], axis=1)) for c in chunks]
    mix = [_dot(jnp.concatenate([a_rb[c], blkl_t[c][:P]], axis=0), hat[c]) for c in chunks]
    rhat = [rt[c].astype(F32) + mix[c][:P, :P] for c in chunks]
    ohat = [mix[c][:P, P:] + avd[c][P:2 * P] for c in chunks]
    mmat = [eye * jnp.exp(tot[c]) + mix[c][P:, :P] for c in chunks]
    cmat = [mix[c][P:, P:] + avd[c][2 * P:] for c in chunks]
    rhat = [x.astype(BF16) for x in rhat]
    mmat = [x.astype(BF16) for x in mmat]

    states = []
    st = st_ref[...]
    for c in chunks:
        states.append(st.astype(BF16))
        st = _dot(mmat[c], states[c]) + cmat[c]
    st_ref[...] = st
    outs = []
    for c in chunks:
        od = _dot(rhat[c], states[c]) + ohat[c]
        outs.append(od[:L] + od[L:])
    wkv = jnp.concatenate(outs, axis=0)

    inv_n = 1.0 / N
    mu = _dot_exact_rhs(wkv, ones_bd, HEAD_SUM_PIECES) * inv_n
    xc = wkv - mu
    var = _dot_exact_rhs(xc * xc, ones_bd, HEAD_SUM_PIECES) * inv_n
    o = xc * lax.rsqrt(var + LNX_EPS) * lw_ref[...] + lb_ref[...]
    bonus = _dot_exact_rhs(r * kf * rk_ref[...], ones_bd, HEAD_SUM_PIECES) * v
    o_ref[...] = ((o + bonus) * g).astype(o_ref.dtype)


def _rwkv(cols, w0, a0, k_k, k_a, r_k, lnx_w, lnx_b, w2p, a2p, g2p, *, batch, seq, n_pairs, rows):
    T = cols.shape[0]
    nt = seq // rows
    lora_blk = (cols.shape[1] - 3 * n_pairs * LANES) // LANES
    row_map = lambda off: (lambda b, p, t: (b * nt + t, off + p))
    vec = pl.BlockSpec((1, LANES), lambda b, p, t: (0, p))
    return pl.pallas_call(
        functools.partial(_rwkv_kernel, rows=rows),
        grid=(batch, n_pairs, nt),
        in_specs=[
            pl.BlockSpec((rows, LANES), row_map(0)),
            pl.BlockSpec((rows, LANES), row_map(n_pairs)),
            pl.BlockSpec((rows, LANES), row_map(2 * n_pairs)),
            pl.BlockSpec((rows, lora_blk * LANES), lambda b, p, t: (b * nt + t, 3 * n_pairs // lora_blk)),
            vec, vec, vec, vec, vec, vec, vec,
            pl.BlockSpec((LANES, LANES), lambda b, p, t: (0, p)),
            pl.BlockSpec((LANES, LANES), lambda b, p, t: (0, p)),
            pl.BlockSpec((2 * LANES, LANES), lambda b, p, t: (0, p)),
        ],
        out_specs=pl.BlockSpec((rows, LANES), row_map(0)),
        out_shape=jax.ShapeDtypeStruct((T, n_pairs * LANES), BF16),
        scratch_shapes=[pltpu.VMEM((LANES, LANES), F32)],
        compiler_params=_cparams("arbitrary", "arbitrary", "arbitrary"),
        name="rwkv7",
    )(cols, cols, cols, cols, w0, a0, k_k, k_a, r_k, lnx_w, lnx_b, w2p, a2p, g2p)


def _out_proj_kernel(x_ref, oa_ref, or_ref, wa_ref, wr_ref, nw_ref, x1_ref, h2_ref):
    x1 = (x_ref[...] + jnp.dot(oa_ref[...], wa_ref[...], preferred_element_type=F32)
          + jnp.dot(or_ref[...], wr_ref[...], preferred_element_type=F32))
    x1_ref[...] = x1
    h2_ref[...] = _rms(x1, nw_ref[...]).astype(BF16)


def _out_proj(x2, oa, orw, wa, wr, nw, *, tm):
    T, D = x2.shape
    W = oa.shape[1]
    return pl.pallas_call(
        _out_proj_kernel,
        grid=(T // tm,),
        in_specs=[
            pl.BlockSpec((tm, D), lambda i: (i, 0)),
            pl.BlockSpec((tm, W), lambda i: (i, 0)),
            pl.BlockSpec((tm, W), lambda i: (i, 0)),
            pl.BlockSpec((W, D), lambda i: (0, 0)),
            pl.BlockSpec((W, D), lambda i: (0, 0)),
            pl.BlockSpec((1, D), lambda i: (0, 0)),
        ],
        out_specs=[pl.BlockSpec((tm, D), lambda i: (i, 0)), pl.BlockSpec((tm, D), lambda i: (i, 0))],
        out_shape=[jax.ShapeDtypeStruct((T, D), F32), jax.ShapeDtypeStruct((T, D), BF16)],
        compiler_params=_cparams("arbitrary"),
        name="out_proj",
    )(x2, oa, orw, wa, wr, nw)


def _topk_rows(curs, k):
    curs = list(curs)
    rows = [[] for _ in curs]
    for _ in range(k):
        ms = [jnp.max(cur, axis=0, keepdims=True) for cur in curs]
        curs = [jnp.where(cur == m, -jnp.inf, cur) for cur, m in zip(curs, ms)]
        for r, m in zip(rows, ms):
            r.append(m)
    return rows


_NLIST = PEER_TOPK + 1
_CANDS = [(a, b) for a in range(_NLIST) for b in range(_NLIST) if (a + 1) * (b + 1) <= _NLIST]
_NCAND = -(-len(_CANDS) // 8) * 8


def _route_kernel(h2_ref, wq_ref, keys_ref, thr_ref, e1_ref, e2_ref, sc_ref, cand_ref, *, n_heads):
    q = jnp.dot(h2_ref[...], wq_ref[...], preferred_element_type=F32).astype(BF16)
    for g in range(2 * n_heads):
        sc_ref[g] = lax.dot_general(keys_ref[g], q[:, g * LANES:(g + 1) * LANES], _NT,
                                    preferred_element_type=F32)
    cand_ref[...] = jnp.full(cand_ref.shape, -jnp.inf, F32)

    def per_head(h, _):
        s1 = sc_ref[2 * h]
        s2 = sc_ref[2 * h + 1]
        v1, v2 = _topk_rows((s1, s2), _NLIST)
        for n, (ia, ib) in enumerate(_CANDS):
            cand_ref[n:n + 1, :] = v1[ia] + v2[ib]
        cand = cand_ref[...]
        top, = _topk_rows((cand,), _NLIST)
        tau = 0.5 * (top[PEER_TOPK - 1] + top[PEER_TOPK])
        inv_z = 1.0 / jnp.sum(jnp.where(cand > tau, jnp.exp(cand - (v1[0] + v2[0])), 0.0), axis=0, keepdims=True)
        e1_ref[h] = jnp.exp(s1 - v1[0])
        e2_ref[h] = jnp.exp(s2 - v2[0]) * inv_z
        thr_ref[h] = jnp.exp((tau - s1) - v2[0]) * inv_z
        return 0

    lax.fori_loop(0, n_heads, per_head, 0)


def _route(h2, wq, keys, *, tt, n_heads):
    T, D = h2.shape
    nk = keys.shape[1]
    big = jax.ShapeDtypeStruct((n_heads, nk, T), F32)
    blk = pl.BlockSpec((n_heads, nk, tt), lambda i: (0, 0, i))
    return pl.pallas_call(
        functools.partial(_route_kernel, n_heads=n_heads),
        grid=(T // tt,),
        in_specs=[
            pl.BlockSpec((tt, D), lambda i: (i, 0)),
            pl.BlockSpec(wq.shape, lambda i: (0, 0)),
            pl.BlockSpec(keys.shape, lambda i: (0, 0, 0)),
        ],
        out_specs=[blk, blk, blk],
        out_shape=[big, big, big],
        scratch_shapes=[pltpu.VMEM((2 * n_heads, nk, tt), F32), pltpu.VMEM((_NCAND, tt), F32)],
        compiler_params=_cparams("arbitrary"),
        name="peer_route",
    )(h2, wq, keys)


def _peer_kernel(h2_ref, down_ref, upt_ref, thr_ref, e1_ref, e2_ref, o_ref, act_ref, *, n_heads, nk):
    @pl.when(pl.program_id(1) == 0)
    def _():
        o_ref[...] = jnp.zeros(o_ref.shape, F32)

    n_i1 = down_ref.shape[0] // nk
    per_chunk = n_i1 // PEER_Z_CHUNKS
    for q in range(PEER_Z_CHUNKS):
        zr = slice(q * per_chunk * nk, (q + 1) * per_chunk * nk)
        z = lax.dot_general(down_ref[zr, :], h2_ref[...], _NT, preferred_element_type=F32)
        for ic in range(per_chunk):
            ii = q * per_chunk + ic
            rs = slice(ii * nk, (ii + 1) * nk)
            gate = None
            for h in range(n_heads):
                e2 = e2_ref[h]
                val = jnp.where(e2 >= thr_ref[h, ii:ii + 1, :], e2, 0.0) * e1_ref[h, ii:ii + 1, :]
                gate = val if gate is None else gate + val
            zz = z[ic * nk:(ic + 1) * nk]
            gelu = zz * (1.0 + lax.erf(zz * (math.sqrt(0.5) / PEER_DOWN_SCALE)))
            act_ref[rs, :] = (gelu * gate).astype(FP8)
    d_rows = upt_ref.shape[0] // PEER_UP_CHUNKS
    e_rows = down_ref.shape[0] // PEER_Z_CHUNKS
    for q in range(PEER_Z_CHUNKS):
        es = slice(q * e_rows, (q + 1) * e_rows)
        for u in range(PEER_UP_CHUNKS):
            us = slice(u * d_rows, (u + 1) * d_rows)
            o_ref[us, :] += jnp.dot(upt_ref[us, es], act_ref[es, :], preferred_element_type=F32)


def _peer(h2, down, upt, thr, e1, e2, *, tt, et, n_heads, nk):
    T, D = h2.shape
    E = down.shape[0]
    i1_blk = et // nk
    row_blk = pl.BlockSpec((n_heads, i1_blk, tt), lambda i, j: (0, j, i))
    once = pl.Buffered(1)
    return pl.pallas_call(
        functools.partial(_peer_kernel, n_heads=n_heads, nk=nk),
        grid=(T // tt, E // et),
        in_specs=[
            pl.BlockSpec((tt, D), lambda i, j: (i, 0), pipeline_mode=once),
            pl.BlockSpec((et, D), lambda i, j: (j, 0)),
            pl.BlockSpec((D, et), lambda i, j: (0, j)),
            row_blk, row_blk,
            pl.BlockSpec((n_heads, nk, tt), lambda i, j: (0, 0, i), pipeline_mode=once),
        ],
        out_specs=pl.BlockSpec((D, tt), lambda i, j: (0, i)),
        out_shape=jax.ShapeDtypeStruct((D, T), F32),
        scratch_shapes=[pltpu.VMEM((et, tt), FP8)],
        compiler_params=_cparams("arbitrary", "arbitrary"),
        name="peer_experts",
    )(h2, down, upt, thr, e1, e2)


def _final_kernel(x1_ref, pt_ref, nw_ref, o_ref):
    o_ref[...] = _rms(x1_ref[...] + pt_ref[...].T * (1.0 / PEER_ACT_SCALE), nw_ref[...])


def _final(x1, peer_t, nw, *, tm):
    T, D = x1.shape
    return pl.pallas_call(
        _final_kernel,
        grid=(T // tm,),
        in_specs=[
            pl.BlockSpec((tm, D), lambda i: (i, 0)),
            pl.BlockSpec((D, tm), lambda i: (0, i)),
            pl.BlockSpec((1, D), lambda i: (0, 0)),
        ],
        out_specs=pl.BlockSpec((tm, D), lambda i: (i, 0)),
        out_shape=jax.ShapeDtypeStruct((T, D), F32),
        compiler_params=_cparams("arbitrary"),
        name="final_norm",
    )(x1, peer_t, nw)


def _tiles(seq, tokens):
    return dict(
        proj_tm=min(1024, seq),
        attn_tq=min(512, seq),
        rwkv_rows=min(512, seq),
        out_tm=min(512, tokens),
        route_tt=min(256, tokens),
        peer_tt=min(512, tokens),
        peer_et=2048,
        final_tm=min(512, tokens),
    )


def _pad_rows(w, rows_before, rows_total):
    return jnp.pad(w, ((rows_before, rows_total - rows_before - w.shape[0]), (0, 0)))


def kernel(x, norm1_w, w_in, tshift_mu, w0, w2, a0, a2, g2, k_k, k_a, r_k, lnx_w, lnx_b, lambda_q1, lambda_k1,
           lambda_q2, lambda_k2, subln_w, w_out, norm2_w, peer_w_query, peer_sub_keys, peer_down, peer_up,
           norm_f_w):
    B, S, D = x.shape
    T = B * S
    depth = norm1_w.shape[0]
    dh = lambda_q1.shape[-1]
    n_rheads, rhead = r_k.shape[1], r_k.shape[2]
    rw = n_rheads * rhead
    n_pairs = rw // LANES
    w_lora, a_lora, g_lora = w2.shape[1], a2.shape[1], g2.shape[1]
    da_cols = w_in.shape[2] - (3 * rw + w_lora + a_lora + g_lora)
    n_aheads = da_cols // (3 * 2 * dh)
    p_heads, nk = peer_sub_keys.shape[1], peer_sub_keys.shape[3]
    assert 2 * dh == LANES and 2 * rhead == LANES and rhead == RWKV_CHUNK and nk == LANES
    assert w_lora + a_lora == LANES and g_lora <= 2 * LANES
    t = _tiles(S, T)

    xt = x.reshape(T, D)
    for l in range(depth):
        row = lambda p: p[l].reshape(1, -1)
        lam_init = 0.8 - 0.6 * math.exp(-0.3 * l)

        w_attn = w_in[l][:, :da_cols].astype(BF16)
        rcols = w_in[l].shape[1] - da_cols
        rpad = 3 * rw + 3 * LANES - rcols
        w_rwkv = jnp.pad(w_in[l][:, da_cols:], ((0, 0), (0, rpad))).astype(BF16)
        mu = jnp.pad(row(tshift_mu), ((0, 0), (0, rpad)))
        w2p = _pad_rows(w2[l], 0, LANES).astype(BF16)
        a2p = _pad_rows(a2[l], w_lora, LANES).astype(BF16)
        g2p = _pad_rows(g2[l], 0, 2 * LANES).astype(BF16)

        qkv = _in_proj(xt, row(norm1_w), w_attn, None, seq=S, tm=t["proj_tm"], tn=da_cols // 3)
        rcol = _in_proj(xt, row(norm1_w), w_rwkv, mu, seq=S, tm=t["proj_tm"], tn=w_rwkv.shape[1] // 3)

        o_attn = _diff_attention(qkv, row(lambda_q1), row(lambda_k1), row(lambda_q2), row(lambda_k2),
                                 row(subln_w), batch=B, seq=S, n_heads=n_aheads, tq=t["attn_tq"],
                                 lam_init=lam_init)
        o_rwkv = _rwkv(rcol, row(w0), row(a0), row(k_k), row(k_a), row(r_k), row(lnx_w), row(lnx_b),
                       w2p, a2p, g2p, batch=B, seq=S, n_pairs=n_pairs, rows=t["rwkv_rows"])

        aw = o_attn.shape[1]
        x1, h2 = _out_proj(xt, o_attn, o_rwkv, w_out[l][:aw].astype(BF16), w_out[l][aw:].astype(BF16),
                           row(norm2_w), tm=t["out_tm"])

        keys = peer_sub_keys[l].reshape(2 * p_heads, nk, -1).astype(BF16)
        thr, e1, e2 = _route(h2, peer_w_query[l].astype(BF16), keys, tt=t["route_tt"], n_heads=p_heads)
        peer_t = _peer(h2.astype(FP8), (peer_down[l] * PEER_DOWN_SCALE).astype(FP8), peer_up[l].T.astype(FP8), thr, e1, e2,
                       tt=t["peer_tt"], et=t["peer_et"], n_heads=p_heads, nk=nk)
        if l + 1 < depth:
            xt = x1 + peer_t.T * (1.0 / PEER_ACT_SCALE)
        else:
            return _final(x1, peer_t, norm_f_w.reshape(1, -1), tm=t["final_tm"]).reshape(B, S, D)
```

```python
import functools
import math

import jax
import jax.numpy as jnp
from jax import lax
from jax.experimental import pallas as pl
from jax.experimental.pallas import tpu as pltpu

F32 = jnp.float32
BF16 = jnp.bfloat16
FP8 = jnp.float8_e4m3fn

LANES = 128
RMS_EPS = 1e-6
LNX_EPS = 64e-5
PEER_TOPK = 16
RWKV_CHUNK = 64
HEAD_SUM_PIECES = 1
PEER_DOWN_SCALE = 32.0
PEER_ACT_SCALE = 2.0 * PEER_DOWN_SCALE
PEER_Z_CHUNKS = 2
PEER_UP_CHUNKS = 4
VMEM_LIMIT = 56 * 1024 * 1024
NEG = -1e30
SOFTMAX_UNDERFLOW = 110.0
FIXED_SHIFT_SPREAD = 40.0

_NT = (((1,), (1,)), ((), ()))


def _cparams(*sem):
    return pltpu.CompilerParams(dimension_semantics=sem, vmem_limit_bytes=VMEM_LIMIT)


def _rms(x, w):
    ms = jnp.mean(x * x, axis=-1, keepdims=True)
    return x * lax.rsqrt(ms + RMS_EPS) * w


def _proj_kernel(x_ref, nw_ref, w_ref, o_ref, h_ref):
    @pl.when(pl.program_id(1) == 0)
    def _():
        h_ref[...] = _rms(x_ref[...], nw_ref[...]).astype(BF16)

    o_ref[...] = jnp.dot(h_ref[...], w_ref[...], preferred_element_type=F32).astype(o_ref.dtype)


def _proj_shift_kernel(x_ref, nw_ref, w_ref, mu_ref, o_ref, h_ref, carry_ref, *, tiles_per_seq):
    i = pl.program_id(0)
    j = pl.program_id(1)

    @pl.when(j == 0)
    def _():
        h_ref[...] = _rms(x_ref[...], nw_ref[...]).astype(BF16)

    @pl.when(i % tiles_per_seq == 0)
    def _():
        carry_ref[j] = jnp.zeros(carry_ref.shape[1:], F32)

    p = jnp.dot(h_ref[...], w_ref[...], preferred_element_type=F32)
    tm = p.shape[0]
    rolled = pltpu.roll(p, 1, 0)
    row = lax.broadcasted_iota(jnp.int32, p.shape, 0)
    prev = jnp.where(row == 0, carry_ref[j][7:8, :], rolled)
    o_ref[...] = (p + (prev - p) * mu_ref[...]).astype(o_ref.dtype)
    carry_ref[j] = p[tm - 8:tm, :]


def _in_proj(x2, nw, w, mu, *, seq, tm, tn):
    T, D = x2.shape
    N = w.shape[1]
    grid = (T // tm, N // tn)
    in_specs = [
        pl.BlockSpec((tm, D), lambda i, j: (i, 0)),
        pl.BlockSpec((1, D), lambda i, j: (0, 0)),
        pl.BlockSpec((D, tn), lambda i, j: (0, j)),
    ]
    args = [x2, nw, w]
    scratch = [pltpu.VMEM((tm, D), BF16)]
    if mu is None:
        body = _proj_kernel
    else:
        body = functools.partial(_proj_shift_kernel, tiles_per_seq=seq // tm)
        in_specs.append(pl.BlockSpec((1, tn), lambda i, j: (0, j)))
        args.append(mu)
        scratch.append(pltpu.VMEM((N // tn, 8, tn), F32))
    return pl.pallas_call(
        body,
        grid=grid,
        in_specs=in_specs,
        out_specs=pl.BlockSpec((tm, tn), lambda i, j: (i, j)),
        out_shape=jax.ShapeDtypeStruct((T, N), BF16),
        scratch_shapes=scratch,
        compiler_params=_cparams("arbitrary", "arbitrary"),
        name="in_proj" if mu is None else "in_proj_shift",
    )(*args)


def _half_sqnorms(x, lane2, dh):
    sq = x * x
    lo = jnp.sum(jnp.where(lane2 < dh, sq, 0.0), axis=-1, keepdims=True)
    hi = jnp.sum(jnp.where(lane2 >= dh, sq, 0.0), axis=-1, keepdims=True)
    per_row = jnp.maximum(lo, hi)
    return per_row, jnp.max(per_row, axis=0, keepdims=True)


def _attn_kernel(q_ref, k_ref, v_ref, lq1_ref, lk1_ref, lq2_ref, lk2_ref, sw_ref, o_ref, kx1_ref, kx2_ref, knorm_ref,
                 *, tq, seq, n_heads, lam_init):
    h = pl.program_id(1)
    qi = pl.program_id(2)
    tk = tq
    dh = LANES // 2
    lane2 = lax.broadcasted_iota(jnp.int32, (tq, LANES), 1)

    @pl.when(qi == 0)
    def _():
        def build(c, knorm):
            rows = pl.ds(pl.multiple_of(c * tk, tk), tk)
            kb = k_ref[rows, :].astype(F32)
            pos = c * tk + lax.broadcasted_iota(jnp.int32, (tk, LANES), 0)
            hi = (pos // LANES).astype(F32)
            lo = (pos % LANES).astype(F32)
            kx1_ref[rows, :] = jnp.where(lane2 < dh, kb, jnp.where(lane2 == dh, hi, jnp.where(lane2 == dh + 1, lo, 0.0))
                                         ).astype(BF16)
            kx2_ref[rows, :] = jnp.where(lane2 >= dh, kb, jnp.where(lane2 == 0, hi, jnp.where(lane2 == 1, lo, 0.0))
                                         ).astype(BF16)
            return jnp.maximum(knorm, _half_sqnorms(kb, lane2, dh)[1])

        knorm_ref[...] = lax.fori_loop(0, seq // tk, build, jnp.zeros((1, 1), F32))

    q = q_ref[...].astype(F32) * dh ** -0.5
    hv = jnp.full((tq, LANES), h + 1, jnp.int32).astype(F32)
    slope = jnp.exp2(hv * (-8.0 / n_heads))
    q1 = jnp.where(lane2 < dh, q, jnp.where(lane2 == dh, slope * LANES, jnp.where(lane2 == dh + 1, slope, 0.0))
                   ).astype(BF16)
    q2 = jnp.where(lane2 >= dh, q, jnp.where(lane2 == 0, slope * LANES, jnp.where(lane2 == 1, slope, 0.0))
                   ).astype(BF16)

    col = lax.broadcasted_iota(jnp.int32, (1, tk), 1)
    row = lax.broadcasted_iota(jnp.int32, (tq, 1), 0)

    def step(kc, carry, masked):
        rows = pl.ds(pl.multiple_of(kc * tk, tk), tk)
        vb = v_ref[rows, :]
        maps = (0, 1)
        ss = [lax.dot_general(qc, kx_ref[rows, :], _NT, preferred_element_type=F32)
              for qc, kx_ref in ((q1, kx1_ref), (q2, kx2_ref))]
        if masked:
            ss = [jnp.where(col <= row, s, NEG) for s in ss]
        ms, ls, accs = carry[0::3], carry[1::3], carry[2::3]
        mns = [jnp.maximum(ms[c], jnp.max(ss[c], axis=-1, keepdims=True)) for c in maps]
        ps = [jnp.exp(ss[c] - mns[c]) for c in maps]
        als = [jnp.exp(ms[c] - mns[c]) for c in maps]
        ls = [als[c] * ls[c] + jnp.sum(ps[c], axis=-1, keepdims=True) for c in maps]
        pvs = [jnp.dot(ps[c].astype(BF16), vb, preferred_element_type=F32) for c in maps]
        accs = [als[c] * accs[c] + pvs[c] for c in maps]
        return (mns[0], ls[0], accs[0], mns[1], ls[1], accs[1])

    row_sq, tile_sq = _half_sqnorms(q, lane2, dh)
    bound = jnp.sqrt(tile_sq * knorm_ref[...])
    reach = (SOFTMAX_UNDERFLOW + 2.0 * bound) / slope[:1, :1]
    q0 = jnp.full((1, 1), qi * tq, jnp.int32).astype(F32)
    first = jnp.max(jnp.clip(jnp.floor((q0 + 1.0 - reach) / tk), 0.0, q0 / tq).astype(jnp.int32))

    def online_softmax():
        init = (jnp.full((tq, 1), NEG, F32), jnp.zeros((tq, 1), F32), jnp.zeros((tq, LANES), F32)) * 2
        carry = lax.fori_loop(first, qi, lambda kc, c: step(kc, c, False), init)
        _, l1, a1, _, l2, a2 = step(qi, carry, True)
        return l1, a1, l2, a2

    slope_col = jnp.exp2(jnp.full((tq, 1), h + 1, jnp.int32).astype(F32) * (-8.0 / n_heads))
    pos_col = (qi * tq + row).astype(F32)
    shift = jnp.sqrt(row_sq * knorm_ref[...]) + slope_col * pos_col

    def fixed_step(kc, carry, masked):
        rows = pl.ds(pl.multiple_of(kc * tk, tk), tk)
        vb = v_ref[rows, :]
        maps = (0, 1)
        ps = [jnp.exp(lax.dot_general(qc, kx_ref[rows, :], _NT, preferred_element_type=F32) - shift)
              for qc, kx_ref in ((q1, kx1_ref), (q2, kx2_ref))]
        if masked:
            ps = [jnp.where(col <= row, p, 0.0) for p in ps]
        ls = [carry[2 * c] + jnp.sum(ps[c], axis=-1, keepdims=True) for c in maps]
        accs = [carry[2 * c + 1] + jnp.dot(ps[c].astype(BF16), vb, preferred_element_type=F32) for c in maps]
        return (ls[0], accs[0], ls[1], accs[1])

    def fixed_shift_softmax():
        init = (jnp.zeros((tq, 1), F32), jnp.zeros((tq, LANES), F32)) * 2
        carry = lax.fori_loop(first, qi, lambda kc, c: fixed_step(kc, c, False), init)
        return fixed_step(qi, carry, True)

    l1, a1, l2, a2 = lax.cond(jnp.max(bound) * 2.0 <= FIXED_SHIFT_SPREAD, fixed_shift_softmax, online_softmax)

    lam = (jnp.exp(jnp.sum(lq1_ref[...] * lk1_ref[...], axis=-1, keepdims=True))
           - jnp.exp(jnp.sum(lq2_ref[...] * lk2_ref[...], axis=-1, keepdims=True)) + lam_init)
    o = a1 / l1 - lam * (a2 / l2)
    o_ref[...] = (_rms(o, sw_ref[...]) * (1.0 - lam_init)).astype(o_ref.dtype)


def _diff_attention(qkv, lq1, lk1, lq2, lk2, subln_w, *, batch, seq, n_heads, tq, lam_init):
    T = qkv.shape[0]
    nq = seq // tq
    assert 8 % n_heads == 0 and seq <= 256 * LANES
    vec = lambda n: pl.BlockSpec((1, n), lambda b, h, i: (0, 0))
    return pl.pallas_call(
        functools.partial(_attn_kernel, tq=tq, seq=seq, n_heads=n_heads, lam_init=lam_init),
        grid=(batch, n_heads, nq),
        in_specs=[
            pl.BlockSpec((tq, LANES), lambda b, h, i: (b * nq + i, h)),
            pl.BlockSpec((seq, LANES), lambda b, h, i: (b, n_heads + h)),
            pl.BlockSpec((seq, LANES), lambda b, h, i: (b, 2 * n_heads + h)),
            vec(lq1.shape[1]), vec(lq1.shape[1]), vec(lq1.shape[1]), vec(lq1.shape[1]),
            vec(LANES),
        ],
        out_specs=pl.BlockSpec((tq, LANES), lambda b, h, i: (b * nq + i, h)),
        out_shape=jax.ShapeDtypeStruct((T, n_heads * LANES), BF16),
        scratch_shapes=[pltpu.VMEM((seq, LANES), BF16), pltpu.VMEM((seq, LANES), BF16), pltpu.VMEM((1, 1), F32)],
        compiler_params=_cparams("arbitrary", "arbitrary", "arbitrary"),
        name="diff_attn",
    )(qkv, qkv, qkv, lq1, lk1, lq2, lk2, subln_w)


def _split_bf16(x, n):
    parts = []
    for _ in range(n):
        p = x.astype(BF16)
        parts.append(p)
        x = x - p.astype(F32)
    return parts


def _dot(a, b):
    return jnp.dot(a.astype(BF16), b.astype(BF16), preferred_element_type=F32)


def _dot_exact_rhs(x, m_bf16, n):
    out = None
    for p in _split_bf16(x, n):
        t = jnp.dot(p, m_bf16, preferred_element_type=F32)
        out = t if out is None else out + t
    return out


def _chunk_cumsum(x, chunk):
    pos = lax.broadcasted_iota(jnp.int32, x.shape, 0) % chunk
    step = 1
    while step < chunk:
        x = x + jnp.where(pos >= step, pltpu.roll(x, step, 0), 0.0)
        step *= 2
    return x


def _rwkv_kernel(r_ref, k_ref, v_ref, lo_ref, w0_ref, a0_ref, kk_ref, ka_ref, rk_ref, lw_ref, lb_ref,
                 w2_ref, a2_ref, g2_ref, o_ref, st_ref, *, rows):
    L = RWKV_CHUNK
    N = LANES // 2
    P = LANES

    @pl.when(pl.program_id(2) == 0)
    def _():
        st_ref[...] = jnp.zeros(st_ref.shape, F32)

    r = r_ref[...].astype(F32)
    k = k_ref[...].astype(F32)
    v = v_ref[...].astype(F32)
    lo = lo_ref[...]
    lo_wa = lo[:, :P]
    wl = jnp.dot(jnp.tanh(lo_wa.astype(F32)).astype(BF16), w2_ref[...], preferred_element_type=F32)
    al = jnp.dot(lo_wa, a2_ref[...], preferred_element_type=F32)
    g = jnp.dot(jax.nn.sigmoid(lo[:, P:].astype(F32)).astype(BF16), g2_ref[...], preferred_element_type=F32)

    wx = -(w0_ref[...] + wl)
    softplus = jnp.maximum(wx, 0.0) + jnp.log1p(jnp.exp(-jnp.abs(wx)))
    logdecay = -jnp.exp(-softplus - 0.5)
    a = jax.nn.sigmoid(a0_ref[...] + al)

    ri = lax.broadcasted_iota(jnp.int32, (P, P), 0)
    ci = lax.broadcasted_iota(jnp.int32, (P, P), 1)
    ones_bd = ((ri // N) == (ci // N)).astype(BF16)
    eye = (ri == ci).astype(F32)
    strict = ((ri % L) > (ci % L)).astype(F32)
    lower = ((ri % L) >= (ci % L)).astype(F32)
    head0 = lax.broadcasted_iota(jnp.int32, (1, P), 1) < N

    kkr = k * kk_ref[...]
    norm = jnp.sqrt(_dot_exact_rhs(kkr * kkr, ones_bd, HEAD_SUM_PIECES))
    kk = kkr / jnp.maximum(norm, 1e-12)
    kf = k * (1.0 + (a - 1.0) * ka_ref[...])
    avec = -kk
    bvec = kk * a

    def bd(x):
        x = x.astype(BF16)
        z = jnp.zeros_like(x)
        return jnp.concatenate([jnp.where(head0, x, z), jnp.where(head0, z, x)], axis=0)

    nc = rows // L
    chunks = range(nc)
    sls = [slice(c * L, (c + 1) * L) for c in chunks]
    cum_all = _chunk_cumsum(logdecay, L)
    cum = [cum_all[s] for s in sls]
    tot = [cm[L - 1:L, :] for cm in cum]
    at = [bd(avec[s] * jnp.exp(cm - logdecay[s])) for s, cm in zip(sls, cum)]
    rt = [bd(r[s] * jnp.exp(cm)) for s, cm in zip(sls, cum)]
    inv = [jnp.exp(-cm) for cm in cum]
    bt = [bd(bvec[s] * iv) for s, iv in zip(sls, inv)]
    kt = [bd(kf[s] * iv) for s, iv in zip(sls, inv)]
    rest = [jnp.exp(t - cm) for t, cm in zip(tot, cum)]
    blkl_t = [jnp.concatenate([bd(bvec[s] * rs), bd(kf[s] * rs)], axis=1).astype(F32).T.astype(BF16)
              for s, rs in zip(sls, rest)]
    vd = [bd(v[s]) for s in sls]

    amat = [lax.dot_general(jnp.concatenate([at[c], rt[c]], axis=0), jnp.concatenate([bt[c], kt[c]], axis=0), _NT,
                            preferred_element_type=F32) for c in chunks]
    a_ab = [m[:P, :P] * strict for m in amat]
    a_ak = [(m[:P, P:] * strict).astype(BF16) for m in amat]
    a_rb = [(m[P:, :P] * lower).astype(BF16) for m in amat]
    a_rk = [(m[P:, P:] * lower).astype(BF16) for m in amat]

    tinv = [eye + n for n in a_ab]
    pw = [_dot(n, n) for n in a_ab]
    for _ in range(int(math.log2(L)) - 2):
        both = [_dot(pw[c], jnp.concatenate([pw[c], tinv[c]], axis=1)) for c in chunks]
        pw = [b[:, :P] for b in both]
        tinv = [tinv[c] + both[c][:, P:] for c in chunks]
    tinv = [tinv[c] + _dot(pw[c], tinv[c]) for c in chunks]

    avd = [_dot(jnp.concatenate([a_ak[c], a_rk[c], blkl_t[c][P:]], axis=0), vd[c]) for c in chunks]
    hat = [_dot(tinv[c], jnp.concatenate([at[c], avd[c][:P].astype(BF16)], axis=1)) for c in chunks]
    mix = [_dot(jnp.concatenate([a_rb[c], blkl_t[c][:P]], axis=0), hat[c]) for c in chunks]
    rhat = [rt[c].astype(F32) + mix[c][:P, :P] for c in chunks]
    ohat = [mix[c][:P, P:] + avd[c][P:2 * P] for c in chunks]
    mmat = [eye * jnp.exp(tot[c]) + mix[c][P:, :P] for c in chunks]
    cmat = [mix[c][P:, P:] + avd[c][2 * P:] for c in chunks]
    rhat = [x.astype(BF16) for x in rhat]
    mmat = [x.astype(BF16) for x in mmat]

    states = []
    st = st_ref[...]
    for c in chunks:
        states.append(st.astype(BF16))
        st = _dot(mmat[c], states[c]) + cmat[c]
    st_ref[...] = st
    outs = []
    for c in chunks:
        od = _dot(rhat[c], states[c]) + ohat[c]
        outs.append(od[:L] + od[L:])
    wkv = jnp.concatenate(outs, axis=0)

    inv_n = 1.0 / N
    mu = _dot_exact_rhs(wkv, ones_bd, HEAD_SUM_PIECES) * inv_n
    xc = wkv - mu
    var = _dot_exact_rhs(xc * xc, ones_bd, HEAD_SUM_PIECES) * inv_n
    o = xc * lax.rsqrt(var + LNX_EPS) * lw_ref[...] + lb_ref[...]
    bonus = _dot_exact_rhs(r * kf * rk_ref[...], ones_bd, HEAD_SUM_PIECES) * v
    o_ref[...] = ((o + bonus) * g).astype(o_ref.dtype)


def _rwkv(cols, w0, a0, k_k, k_a, r_k, lnx_w, lnx_b, w2p, a2p, g2p, *, batch, seq, n_pairs, rows):
    T = cols.shape[0]
    nt = seq // rows
    lora_blk = (cols.shape[1] - 3 * n_pairs * LANES) // LANES
    row_map = lambda off: (lambda b, p, t: (b * nt + t, off + p))
    vec = pl.BlockSpec((1, LANES), lambda b, p, t: (0, p))
    return pl.pallas_call(
        functools.partial(_rwkv_kernel, rows=rows),
        grid=(batch, n_pairs, nt),
        in_specs=[
            pl.BlockSpec((rows, LANES), row_map(0)),
            pl.BlockSpec((rows, LANES), row_map(n_pairs)),
            pl.BlockSpec((rows, LANES), row_map(2 * n_pairs)),
            pl.BlockSpec((rows, lora_blk * LANES), lambda b, p, t: (b * nt + t, 3 * n_pairs // lora_blk)),
            vec, vec, vec, vec, vec, vec, vec,
            pl.BlockSpec((LANES, LANES), lambda b, p, t: (0, p)),
            pl.BlockSpec((LANES, LANES), lambda b, p, t: (0, p)),
            pl.BlockSpec((2 * LANES, LANES), lambda b, p, t: (0, p)),
        ],
        out_specs=pl.BlockSpec((rows, LANES), row_map(0)),
        out_shape=jax.ShapeDtypeStruct((T, n_pairs * LANES), BF16),
        scratch_shapes=[pltpu.VMEM((LANES, LANES), F32)],
        compiler_params=_cparams("arbitrary", "arbitrary", "arbitrary"),
        name="rwkv7",
    )(cols, cols, cols, cols, w0, a0, k_k, k_a, r_k, lnx_w, lnx_b, w2p, a2p, g2p)


def _out_proj_kernel(x_ref, oa_ref, or_ref, wa_ref, wr_ref, nw_ref, x1_ref, h2_ref):
    x1 = (x_ref[...] + jnp.dot(oa_ref[...], wa_ref[...], preferred_element_type=F32)
          + jnp.dot(or_ref[...], wr_ref[...], preferred_element_type=F32))
    x1_ref[...] = x1
    h2_ref[...] = _rms(x1, nw_ref[...]).astype(BF16)


def _out_proj(x2, oa, orw, wa, wr, nw, *, tm):
    T, D = x2.shape
    W = oa.shape[1]
    return pl.pallas_call(
        _out_proj_kernel,
        grid=(T // tm,),
        in_specs=[
            pl.BlockSpec((tm, D), lambda i: (i, 0)),
            pl.BlockSpec((tm, W), lambda i: (i, 0)),
            pl.BlockSpec((tm, W), lambda i: (i, 0)),
            pl.BlockSpec((W, D), lambda i: (0, 0)),
            pl.BlockSpec((W, D), lambda i: (0, 0)),
            pl.BlockSpec((1, D), lambda i: (0, 0)),
        ],
        out_specs=[pl.BlockSpec((tm, D), lambda i: (i, 0)), pl.BlockSpec((tm, D), lambda i: (i, 0))],
        out_shape=[jax.ShapeDtypeStruct((T, D), F32), jax.ShapeDtypeStruct((T, D), BF16)],
        compiler_params=_cparams("arbitrary"),
        name="out_proj",
    )(x2, oa, orw, wa, wr, nw)


def _topk_rows(curs, k):
    curs = list(curs)
    rows = [[] for _ in curs]
    for _ in range(k):
        ms = [jnp.max(cur, axis=0, keepdims=True) for cur in curs]
        curs = [jnp.where(cur == m, -jnp.inf, cur) for cur, m in zip(curs, ms)]
        for r, m in zip(rows, ms):
            r.append(m)
    return rows


_NLIST = PEER_TOPK + 1
_CANDS = [(a, b) for a in range(_NLIST) for b in range(_NLIST) if (a + 1) * (b + 1) <= _NLIST]
_NCAND = -(-len(_CANDS) // 8) * 8


def _route_kernel(h2_ref, wq_ref, keys_ref, thr_ref, e1_ref, e2_ref, sc_ref, cand_ref, *, n_heads):
    q = jnp.dot(h2_ref[...], wq_ref[...], preferred_element_type=F32).astype(BF16)
    for g in range(2 * n_heads):
        sc_ref[g] = lax.dot_general(keys_ref[g], q[:, g * LANES:(g + 1) * LANES], _NT,
                                    preferred_element_type=F32)
    cand_ref[...] = jnp.full(cand_ref.shape, -jnp.inf, F32)

    def per_head(h, _):
        s1 = sc_ref[2 * h]
        s2 = sc_ref[2 * h + 1]
        v1, v2 = _topk_rows((s1, s2), _NLIST)
        for n, (ia, ib) in enumerate(_CANDS):
            cand_ref[n:n + 1, :] = v1[ia] + v2[ib]
        cand = cand_ref[...]
        top, = _topk_rows((cand,), _NLIST)
        tau = 0.5 * (top[PEER_TOPK - 1] + top[PEER_TOPK])
        inv_z = 1.0 / jnp.sum(jnp.where(cand > tau, jnp.exp(cand - (v1[0] + v2[0])), 0.0), axis=0, keepdims=True)
        e1_ref[h] = jnp.exp(s1 - v1[0])
        e2_ref[h] = jnp.exp(s2 - v2[0]) * inv_z
        thr_ref[h] = jnp.exp((tau - s1) - v2[0]) * inv_z
        return 0

    lax.fori_loop(0, n_heads, per_head, 0)


def _route(h2, wq, keys, *, tt, n_heads):
    T, D = h2.shape
    nk = keys.shape[1]
    big = jax.ShapeDtypeStruct((n_heads, nk, T), F32)
    blk = pl.BlockSpec((n_heads, nk, tt), lambda i: (0, 0, i))
    return pl.pallas_call(
        functools.partial(_route_kernel, n_heads=n_heads),
        grid=(T // tt,),
        in_specs=[
            pl.BlockSpec((tt, D), lambda i: (i, 0)),
            pl.BlockSpec(wq.shape, lambda i: (0, 0)),
            pl.BlockSpec(keys.shape, lambda i: (0, 0, 0)),
        ],
        out_specs=[blk, blk, blk],
        out_shape=[big, big, big],
        scratch_shapes=[pltpu.VMEM((2 * n_heads, nk, tt), F32), pltpu.VMEM((_NCAND, tt), F32)],
        compiler_params=_cparams("arbitrary"),
        name="peer_route",
    )(h2, wq, keys)


def _peer_kernel(h2_ref, down_ref, upt_ref, thr_ref, e1_ref, e2_ref, o_ref, act_ref, *, n_heads, nk):
    @pl.when(pl.program_id(1) == 0)
    def _():
        o_ref[...] = jnp.zeros(o_ref.shape, F32)

    n_i1 = down_ref.shape[0] // nk
    per_chunk = n_i1 // PEER_Z_CHUNKS
    for q in range(PEER_Z_CHUNKS):
        zr = slice(q * per_chunk * nk, (q + 1) * per_chunk * nk)
        z = lax.dot_general(down_ref[zr, :], h2_ref[...], _NT, preferred_element_type=F32)
        for ic in range(per_chunk):
            ii = q * per_chunk + ic
            rs = slice(ii * nk, (ii + 1) * nk)
            gate = None
            for h in range(n_heads):
                e2 = e2_ref[h]
                val = jnp.where(e2 >= thr_ref[h, ii:ii + 1, :], e2, 0.0) * e1_ref[h, ii:ii + 1, :]
                gate = val if gate is None else gate + val
            zz = z[ic * nk:(ic + 1) * nk]
            gelu = zz * (1.0 + lax.erf(zz * (math.sqrt(0.5) / PEER_DOWN_SCALE)))
            act_ref[rs, :] = (gelu * gate).astype(FP8)
    d_rows = upt_ref.shape[0] // PEER_UP_CHUNKS
    e_rows = down_ref.shape[0] // PEER_Z_CHUNKS
    for q in range(PEER_Z_CHUNKS):
        es = slice(q * e_rows, (q + 1) * e_rows)
        for u in range(PEER_UP_CHUNKS):
            us = slice(u * d_rows, (u + 1) * d_rows)
            o_ref[us, :] += jnp.dot(upt_ref[us, es], act_ref[es, :], preferred_element_type=F32)


def _peer(h2, down, upt, thr, e1, e2, *, tt, et, n_heads, nk):
    T, D = h2.shape
    E = down.shape[0]
    i1_blk = et // nk
    row_blk = pl.BlockSpec((n_heads, i1_blk, tt), lambda i, j: (0, j, i))
    once = pl.Buffered(1)
    return pl.pallas_call(
        functools.partial(_peer_kernel, n_heads=n_heads, nk=nk),
        grid=(T // tt, E // et),
        in_specs=[
            pl.BlockSpec((tt, D), lambda i, j: (i, 0), pipeline_mode=once),
            pl.BlockSpec((et, D), lambda i, j: (j, 0)),
            pl.BlockSpec((D, et), lambda i, j: (0, j)),
            row_blk, row_blk,
            pl.BlockSpec((n_heads, nk, tt), lambda i, j: (0, 0, i), pipeline_mode=once),
        ],
        out_specs=pl.BlockSpec((D, tt), lambda i, j: (0, i)),
        out_shape=jax.ShapeDtypeStruct((D, T), F32),
        scratch_shapes=[pltpu.VMEM((et, tt), FP8)],
        compiler_params=_cparams("arbitrary", "arbitrary"),
        name="peer_experts",
    )(h2, down, upt, thr, e1, e2)


def _final_kernel(x1_ref, pt_ref, nw_ref, o_ref):
    o_ref[...] = _rms(x1_ref[...] + pt_ref[...].T * (1.0 / PEER_ACT_SCALE), nw_ref[...])


def _final(x1, peer_t, nw, *, tm):
    T, D = x1.shape
    return pl.pallas_call(
        _final_kernel,
        grid=(T // tm,),
        in_specs=[
            pl.BlockSpec((tm, D), lambda i: (i, 0)),
            pl.BlockSpec((D, tm), lambda i: (0, i)),
            pl.BlockSpec((1, D), lambda i: (0, 0)),
        ],
        out_specs=pl.BlockSpec((tm, D), lambda i: (i, 0)),
        out_shape=jax.ShapeDtypeStruct((T, D), F32),
        compiler_params=_cparams("arbitrary"),
        name="final_norm",
    )(x1, peer_t, nw)


def _tiles(seq, tokens):
    return dict(
        proj_tm=min(1024, seq),
        attn_tq=min(512, seq),
        rwkv_rows=min(512, seq),
        out_tm=min(512, tokens),
        route_tt=min(512, tokens),
        peer_tt=min(512, tokens),
        peer_et=2048,
        final_tm=min(512, tokens),
    )


def _pad_rows(w, rows_before, rows_total):
    return jnp.pad(w, ((rows_before, rows_total - rows_before - w.shape[0]), (0, 0)))


def kernel(x, norm1_w, w_in, tshift_mu, w0, w2, a0, a2, g2, k_k, k_a, r_k, lnx_w, lnx_b, lambda_q1, lambda_k1,
           lambda_q2, lambda_k2, subln_w, w_out, norm2_w, peer_w_query, peer_sub_keys, peer_down, peer_up,
           norm_f_w):
    B, S, D = x.shape
    T = B * S
    depth = norm1_w.shape[0]
    dh = lambda_q1.shape[-1]
    n_rheads, rhead = r_k.shape[1], r_k.shape[2]
    rw = n_rheads * rhead
    n_pairs = rw // LANES
    w_lora, a_lora, g_lora = w2.shape[1], a2.shape[1], g2.shape[1]
    da_cols = w_in.shape[2] - (3 * rw + w_lora + a_lora + g_lora)
    n_aheads = da_cols // (3 * 2 * dh)
    p_heads, nk = peer_sub_keys.shape[1], peer_sub_keys.shape[3]
    assert 2 * dh == LANES and 2 * rhead == LANES and rhead == RWKV_CHUNK and nk == LANES
    assert w_lora + a_lora == LANES and g_lora <= 2 * LANES
    t = _tiles(S, T)

    xt = x.reshape(T, D)
    for l in range(depth):
        row = lambda p: p[l].reshape(1, -1)
        lam_init = 0.8 - 0.6 * math.exp(-0.3 * l)

        w_attn = w_in[l][:, :da_cols].astype(BF16)
        rcols = w_in[l].shape[1] - da_cols
        rpad = 3 * rw + 3 * LANES - rcols
        w_rwkv = jnp.pad(w_in[l][:, da_cols:], ((0, 0), (0, rpad))).astype(BF16)
        mu = jnp.pad(row(tshift_mu), ((0, 0), (0, rpad)))
        w2p = _pad_rows(w2[l], 0, LANES).astype(BF16)
        a2p = _pad_rows(a2[l], w_lora, LANES).astype(BF16)
        g2p = _pad_rows(g2[l], 0, 2 * LANES).astype(BF16)

        qkv = _in_proj(xt, row(norm1_w), w_attn, None, seq=S, tm=t["proj_tm"], tn=da_cols // 3)
        rcol = _in_proj(xt, row(norm1_w), w_rwkv, mu, seq=S, tm=t["proj_tm"], tn=w_rwkv.shape[1] // 3)

        o_attn = _diff_attention(qkv, row(lambda_q1), row(lambda_k1), row(lambda_q2), row(lambda_k2),
                                 row(subln_w), batch=B, seq=S, n_heads=n_aheads, tq=t["attn_tq"],
                                 lam_init=lam_init)
        o_rwkv = _rwkv(rcol, row(w0), row(a0), row(k_k), row(k_a), row(r_k), row(lnx_w), row(lnx_b),
                       w2p, a2p, g2p, batch=B, seq=S, n_pairs=n_pairs, rows=t["rwkv_rows"])

        aw = o_attn.shape[1]
        x1, h2 = _out_proj(xt, o_attn, o_rwkv, w_out[l][:aw].astype(BF16), w_out[l][aw:].astype(BF16),
                           row(norm2_w), tm=t["out_tm"])

        keys = peer_sub_keys[l].reshape(2 * p_heads, nk, -1).astype(BF16)
        thr, e1, e2 = _route(h2, peer_w_query[l].astype(BF16), keys, tt=t["route_tt"], n_heads=p_heads)
        peer_t = _peer(h2.astype(FP8), (peer_down[l] * PEER_DOWN_SCALE).astype(FP8), peer_up[l].T.astype(FP8), thr, e1, e2,
                       tt=t["peer_tt"], et=t["peer_et"], n_heads=p_heads, nk=nk)
        if l + 1 < depth:
            xt = x1 + peer_t.T * (1.0 / PEER_ACT_SCALE)
        else:
            return _final(x1, peer_t, norm_f_w.reshape(1, -1), tm=t["final_tm"]).reshape(B, S, D)
```

```python
import functools
import math

import jax
import jax.numpy as jnp
from jax import lax
from jax.experimental import pallas as pl
from jax.experimental.pallas import tpu as pltpu

F32 = jnp.float32
BF16 = jnp.bfloat16
FP8 = jnp.bfloat16

LANES = 128
RMS_EPS = 1e-6
LNX_EPS = 64e-5
PEER_TOPK = 16
RWKV_CHUNK = 64
HEAD_SUM_PIECES = 1
PEER_DOWN_SCALE = 1.0
PEER_ACT_SCALE = 2.0 * PEER_DOWN_SCALE
PEER_Z_CHUNKS = 2
PEER_UP_CHUNKS = 4
VMEM_LIMIT = 56 * 1024 * 1024
NEG = -1e30
SOFTMAX_UNDERFLOW = 110.0
FIXED_SHIFT_SPREAD = 40.0

_NT = (((1,), (1,)), ((), ()))


def _cparams(*sem):
    return pltpu.CompilerParams(dimension_semantics=sem, vmem_limit_bytes=VMEM_LIMIT)


def _rms(x, w):
    ms = jnp.mean(x * x, axis=-1, keepdims=True)
    return x * lax.rsqrt(ms + RMS_EPS) * w


def _proj_kernel(x_ref, nw_ref, w_ref, o_ref, h_ref):
    @pl.when(pl.program_id(1) == 0)
    def _():
        h_ref[...] = _rms(x_ref[...], nw_ref[...]).astype(BF16)

    o_ref[...] = jnp.dot(h_ref[...], w_ref[...], preferred_element_type=F32).astype(o_ref.dtype)


def _proj_shift_kernel(x_ref, nw_ref, w_ref, mu_ref, o_ref, h_ref, carry_ref, *, tiles_per_seq):
    i = pl.program_id(0)
    j = pl.program_id(1)

    @pl.when(j == 0)
    def _():
        h_ref[...] = _rms(x_ref[...], nw_ref[...]).astype(BF16)

    @pl.when(i % tiles_per_seq == 0)
    def _():
        carry_ref[j] = jnp.zeros(carry_ref.shape[1:], F32)

    p = jnp.dot(h_ref[...], w_ref[...], preferred_element_type=F32)
    tm = p.shape[0]
    rolled = pltpu.roll(p, 1, 0)
    row = lax.broadcasted_iota(jnp.int32, p.shape, 0)
    prev = jnp.where(row == 0, carry_ref[j][7:8, :], rolled)
    o_ref[...] = (p + (prev - p) * mu_ref[...]).astype(o_ref.dtype)
    carry_ref[j] = p[tm - 8:tm, :]


def _in_proj(x2, nw, w, mu, *, seq, tm, tn):
    T, D = x2.shape
    N = w.shape[1]
    grid = (T // tm, N // tn)
    in_specs = [
        pl.BlockSpec((tm, D), lambda i, j: (i, 0)),
        pl.BlockSpec((1, D), lambda i, j: (0, 0)),
        pl.BlockSpec((D, tn), lambda i, j: (0, j)),
    ]
    args = [x2, nw, w]
    scratch = [pltpu.VMEM((tm, D), BF16)]
    if mu is None:
        body = _proj_kernel
    else:
        body = functools.partial(_proj_shift_kernel, tiles_per_seq=seq // tm)
        in_specs.append(pl.BlockSpec((1, tn), lambda i, j: (0, j)))
        args.append(mu)
        scratch.append(pltpu.VMEM((N // tn, 8, tn), F32))
    return pl.pallas_call(
        body,
        grid=grid,
        in_specs=in_specs,
        out_specs=pl.BlockSpec((tm, tn), lambda i, j: (i, j)),
        out_shape=jax.ShapeDtypeStruct((T, N), BF16),
        scratch_shapes=scratch,
        compiler_params=_cparams("arbitrary", "arbitrary"),
        name="in_proj" if mu is None else "in_proj_shift",
    )(*args)


def _half_sqnorms(x, lane2, dh):
    sq = x * x
    lo = jnp.sum(jnp.where(lane2 < dh, sq, 0.0), axis=-1, keepdims=True)
    hi = jnp.sum(jnp.where(lane2 >= dh, sq, 0.0), axis=-1, keepdims=True)
    per_row = jnp.maximum(lo, hi)
    return per_row, jnp.max(per_row, axis=0, keepdims=True)


def _attn_kernel(q_ref, k_ref, v_ref, lq1_ref, lk1_ref, lq2_ref, lk2_ref, sw_ref, o_ref, kx1_ref, kx2_ref, knorm_ref,
                 *, tq, seq, n_heads, lam_init):
    h = pl.program_id(1)
    qi = pl.program_id(2)
    tk = tq
    dh = LANES // 2
    lane2 = lax.broadcasted_iota(jnp.int32, (tq, LANES), 1)

    @pl.when(qi == 0)
    def _():
        def build(c, knorm):
            rows = pl.ds(pl.multiple_of(c * tk, tk), tk)
            kb = k_ref[rows, :].astype(F32)
            pos = c * tk + lax.broadcasted_iota(jnp.int32, (tk, LANES), 0)
            hi = (pos // LANES).astype(F32)
            lo = (pos % LANES).astype(F32)
            kx1_ref[rows, :] = jnp.where(lane2 < dh, kb, jnp.where(lane2 == dh, hi, jnp.where(lane2 == dh + 1, lo, 0.0))
                                         ).astype(BF16)
            kx2_ref[rows, :] = jnp.where(lane2 >= dh, kb, jnp.where(lane2 == 0, hi, jnp.where(lane2 == 1, lo, 0.0))
                                         ).astype(BF16)
            return jnp.maximum(knorm, _half_sqnorms(kb, lane2, dh)[1])

        knorm_ref[...] = lax.fori_loop(0, seq // tk, build, jnp.zeros((1, 1), F32))

    q = q_ref[...].astype(F32) * dh ** -0.5
    hv = jnp.full((tq, LANES), h + 1, jnp.int32).astype(F32)
    slope = jnp.exp2(hv * (-8.0 / n_heads))
    q1 = jnp.where(lane2 < dh, q, jnp.where(lane2 == dh, slope * LANES, jnp.where(lane2 == dh + 1, slope, 0.0))
                   ).astype(BF16)
    q2 = jnp.where(lane2 >= dh, q, jnp.where(lane2 == 0, slope * LANES, jnp.where(lane2 == 1, slope, 0.0))
                   ).astype(BF16)

    col = lax.broadcasted_iota(jnp.int32, (1, tk), 1)
    row = lax.broadcasted_iota(jnp.int32, (tq, 1), 0)

    def step(kc, carry, masked):
        rows = pl.ds(pl.multiple_of(kc * tk, tk), tk)
        vb = v_ref[rows, :]
        maps = (0, 1)
        ss = [lax.dot_general(qc, kx_ref[rows, :], _NT, preferred_element_type=F32)
              for qc, kx_ref in ((q1, kx1_ref), (q2, kx2_ref))]
        if masked:
            ss = [jnp.where(col <= row, s, NEG) for s in ss]
        ms, ls, accs = carry[0::3], carry[1::3], carry[2::3]
        mns = [jnp.maximum(ms[c], jnp.max(ss[c], axis=-1, keepdims=True)) for c in maps]
        ps = [jnp.exp(ss[c] - mns[c]) for c in maps]
        als = [jnp.exp(ms[c] - mns[c]) for c in maps]
        ls = [als[c] * ls[c] + jnp.sum(ps[c], axis=-1, keepdims=True) for c in maps]
        pvs = [jnp.dot(ps[c].astype(BF16), vb, preferred_element_type=F32) for c in maps]
        accs = [als[c] * accs[c] + pvs[c] for c in maps]
        return (mns[0], ls[0], accs[0], mns[1], ls[1], accs[1])

    row_sq, tile_sq = _half_sqnorms(q, lane2, dh)
    bound = jnp.sqrt(tile_sq * knorm_ref[...])
    reach = (SOFTMAX_UNDERFLOW + 2.0 * bound) / slope[:1, :1]
    q0 = jnp.full((1, 1), qi * tq, jnp.int32).astype(F32)
    first = jnp.max(jnp.clip(jnp.floor((q0 + 1.0 - reach) / tk), 0.0, q0 / tq).astype(jnp.int32))

    def online_softmax():
        init = (jnp.full((tq, 1), NEG, F32), jnp.zeros((tq, 1), F32), jnp.zeros((tq, LANES), F32)) * 2
        carry = lax.fori_loop(first, qi, lambda kc, c: step(kc, c, False), init)
        _, l1, a1, _, l2, a2 = step(qi, carry, True)
        return l1, a1, l2, a2

    slope_col = jnp.exp2(jnp.full((tq, 1), h + 1, jnp.int32).astype(F32) * (-8.0 / n_heads))
    pos_col = (qi * tq + row).astype(F32)
    shift = jnp.sqrt(row_sq * knorm_ref[...]) + slope_col * pos_col

    def fixed_step(kc, carry, masked):
        rows = pl.ds(pl.multiple_of(kc * tk, tk), tk)
        vb = v_ref[rows, :]
        maps = (0, 1)
        ps = [jnp.exp(lax.dot_general(qc, kx_ref[rows, :], _NT, preferred_element_type=F32) - shift)
              for qc, kx_ref in ((q1, kx1_ref), (q2, kx2_ref))]
        if masked:
            ps = [jnp.where(col <= row, p, 0.0) for p in ps]
        ls = [carry[2 * c] + jnp.sum(ps[c], axis=-1, keepdims=True) for c in maps]
        accs = [carry[2 * c + 1] + jnp.dot(ps[c].astype(BF16), vb, preferred_element_type=F32) for c in maps]
        return (ls[0], accs[0], ls[1], accs[1])

    def fixed_shift_softmax():
        init = (jnp.zeros((tq, 1), F32), jnp.zeros((tq, LANES), F32)) * 2
        carry = lax.fori_loop(first, qi, lambda kc, c: fixed_step(kc, c, False), init)
        return fixed_step(qi, carry, True)

    l1, a1, l2, a2 = lax.cond(jnp.max(bound) * 2.0 <= FIXED_SHIFT_SPREAD, fixed_shift_softmax, online_softmax)

    lam = (jnp.exp(jnp.sum(lq1_ref[...] * lk1_ref[...], axis=-1, keepdims=True))
           - jnp.exp(jnp.sum(lq2_ref[...] * lk2_ref[...], axis=-1, keepdims=True)) + lam_init)
    o = a1 / l1 - lam * (a2 / l2)
    o_ref[...] = (_rms(o, sw_ref[...]) * (1.0 - lam_init)).astype(o_ref.dtype)


def _diff_attention(qkv, lq1, lk1, lq2, lk2, subln_w, *, batch, seq, n_heads, tq, lam_init):
    T = qkv.shape[0]
    nq = seq // tq
    assert 8 % n_heads == 0 and seq <= 256 * LANES
    vec = lambda n: pl.BlockSpec((1, n), lambda b, h, i: (0, 0))
    return pl.pallas_call(
        functools.partial(_attn_kernel, tq=tq, seq=seq, n_heads=n_heads, lam_init=lam_init),
        grid=(batch, n_heads, nq),
        in_specs=[
            pl.BlockSpec((tq, LANES), lambda b, h, i: (b * nq + i, h)),
            pl.BlockSpec((seq, LANES), lambda b, h, i: (b, n_heads + h)),
            pl.BlockSpec((seq, LANES), lambda b, h, i: (b, 2 * n_heads + h)),
            vec(lq1.shape[1]), vec(lq1.shape[1]), vec(lq1.shape[1]), vec(lq1.shape[1]),
            vec(LANES),
        ],
        out_specs=pl.BlockSpec((tq, LANES), lambda b, h, i: (b * nq + i, h)),
        out_shape=jax.ShapeDtypeStruct((T, n_heads * LANES), BF16),
        scratch_shapes=[pltpu.VMEM((seq, LANES), BF16), pltpu.VMEM((seq, LANES), BF16), pltpu.VMEM((1, 1), F32)],
        compiler_params=_cparams("arbitrary", "arbitrary", "arbitrary"),
        name="diff_attn",
    )(qkv, qkv, qkv, lq1, lk1, lq2, lk2, subln_w)


def _split_bf16(x, n):
    parts = []
    for _ in range(n):
        p = x.astype(BF16)
        parts.append(p)
        x = x - p.astype(F32)
    return parts


def _dot(a, b):
    return jnp.dot(a.astype(BF16), b.astype(BF16), preferred_element_type=F32)


def _dot_exact_rhs(x, m_bf16, n):
    out = None
    for p in _split_bf16(x, n):
        t = jnp.dot(p, m_bf16, preferred_element_type=F32)
        out = t if out is None else out + t
    return out


def _chunk_cumsum(x, chunk):
    pos = lax.broadcasted_iota(jnp.int32, x.shape, 0) % chunk
    step = 1
    while step < chunk:
        x = x + jnp.where(pos >= step, pltpu.roll(x, step, 0), 0.0)
        step *= 2
    return x


def _rwkv_kernel(r_ref, k_ref, v_ref, lo_ref, w0_ref, a0_ref, kk_ref, ka_ref, rk_ref, lw_ref, lb_ref,
                 w2_ref, a2_ref, g2_ref, o_ref, st_ref, *, rows):
    L = RWKV_CHUNK
    N = LANES // 2
    P = LANES

    @pl.when(pl.program_id(2) == 0)
    def _():
        st_ref[...] = jnp.zeros(st_ref.shape, F32)

    r = r_ref[...].astype(F32)
    k = k_ref[...].astype(F32)
    v = v_ref[...].astype(F32)
    lo = lo_ref[...]
    lo_wa = lo[:, :P]
    wl = jnp.dot(jnp.tanh(lo_wa.astype(F32)).astype(BF16), w2_ref[...], preferred_element_type=F32)
    al = jnp.dot(lo_wa, a2_ref[...], preferred_element_type=F32)
    g = jnp.dot(jax.nn.sigmoid(lo[:, P:].astype(F32)).astype(BF16), g2_ref[...], preferred_element_type=F32)

    wx = -(w0_ref[...] + wl)
    softplus = jnp.maximum(wx, 0.0) + jnp.log1p(jnp.exp(-jnp.abs(wx)))
    logdecay = -jnp.exp(-softplus - 0.5)
    a = jax.nn.sigmoid(a0_ref[...] + al)

    ri = lax.broadcasted_iota(jnp.int32, (P, P), 0)
    ci = lax.broadcasted_iota(jnp.int32, (P, P), 1)
    ones_bd = ((ri // N) == (ci // N)).astype(BF16)
    eye = (ri == ci).astype(F32)
    strict = ((ri % L) > (ci % L)).astype(F32)
    lower = ((ri % L) >= (ci % L)).astype(F32)
    head0 = lax.broadcasted_iota(jnp.int32, (1, P), 1) < N

    kkr = k * kk_ref[...]
    norm = jnp.sqrt(_dot_exact_rhs(kkr * kkr, ones_bd, HEAD_SUM_PIECES))
    kk = kkr / jnp.maximum(norm, 1e-12)
    kf = k * (1.0 + (a - 1.0) * ka_ref[...])
    avec = -kk
    bvec = kk * a

    def bd(x):
        x = x.astype(BF16)
        z = jnp.zeros_like(x)
        return jnp.concatenate([jnp.where(head0, x, z), jnp.where(head0, z, x)], axis=0)

    nc = rows // L
    chunks = range(nc)
    sls = [slice(c * L, (c + 1) * L) for c in chunks]
    cum_all = _chunk_cumsum(logdecay, L)
    cum = [cum_all[s] for s in sls]
    tot = [cm[L - 1:L, :] for cm in cum]
    at = [bd(avec[s] * jnp.exp(cm - logdecay[s])) for s, cm in zip(sls, cum)]
    rt = [bd(r[s] * jnp.exp(cm)) for s, cm in zip(sls, cum)]
    inv = [jnp.exp(-cm) for cm in cum]
    bt = [bd(bvec[s] * iv) for s, iv in zip(sls, inv)]
    kt = [bd(kf[s] * iv) for s, iv in zip(sls, inv)]
    rest = [jnp.exp(t - cm) for t, cm in zip(tot, cum)]
    blkl_t = [jnp.concatenate([bd(bvec[s] * rs), bd(kf[s] * rs)], axis=1).astype(F32).T.astype(BF16)
              for s, rs in zip(sls, rest)]
    vd = [bd(v[s]) for s in sls]

    amat = [lax.dot_general(jnp.concatenate([at[c], rt[c]], axis=0), jnp.concatenate([bt[c], kt[c]], axis=0), _NT,
                            preferred_element_type=F32) for c in chunks]
    a_ab = [m[:P, :P] * strict for m in amat]
    a_ak = [(m[:P, P:] * strict).astype(BF16) for m in amat]
    a_rb = [(m[P:, :P] * lower).astype(BF16) for m in amat]
    a_rk = [(m[P:, P:] * lower).astype(BF16) for m in amat]

    tinv = [eye + n for n in a_ab]
    pw = [_dot(n, n) for n in a_ab]
    for _ in range(int(math.log2(L)) - 2):
        both = [_dot(pw[c], jnp.concatenate([pw[c], tinv[c]], axis=1)) for c in chunks]
        pw = [b[:, :P] for b in both]
        tinv = [tinv[c] + both[c][:, P:] for c in chunks]
    tinv = [tinv[c] + _dot(pw[c], tinv[c]) for c in chunks]

    avd = [_dot(jnp.concatenate([a_ak[c], a_rk[c], blkl_t[c][P:]], axis=0), vd[c]) for c in chunks]
    hat = [_dot(tinv[c], jnp.concatenate([at[c], avd[c][:P].astype(BF16)], axis=1)) for c in chunks]
    mix = [_dot(jnp.concatenate([a_rb[c], blkl_t[c][:P]], axis=0), hat[c]) for c in chunks]
    rhat = [rt[c].astype(F32) + mix[c][:P, :P] for c in chunks]
    ohat = [mix[c][:P, P:] + avd[c][P:2 * P] for c in chunks]
    mmat = [eye * jnp.exp(tot[c]) + mix[c][P:, :P] for c in chunks]
    cmat = [mix[c][P:, P:] + avd[c][2 * P:] for c in chunks]
    rhat = [x.astype(BF16) for x in rhat]
    mmat = [x.astype(BF16) for x in mmat]

    states = []
    st = st_ref[...]
    for c in chunks:
        states.append(st.astype(BF16))
        st = _dot(mmat[c], states[c]) + cmat[c]
    st_ref[...] = st
    outs = []
    for c in chunks:
        od = _dot(rhat[c], states[c]) + ohat[c]
        outs.append(od[:L] + od[L:])
    wkv = jnp.concatenate(outs, axis=0)

    inv_n = 1.0 / N
    mu = _dot_exact_rhs(wkv, ones_bd, HEAD_SUM_PIECES) * inv_n
    xc = wkv - mu
    var = _dot_exact_rhs(xc * xc, ones_bd, HEAD_SUM_PIECES) * inv_n
    o = xc * lax.rsqrt(var + LNX_EPS) * lw_ref[...] + lb_ref[...]
    bonus = _dot_exact_rhs(r * kf * rk_ref[...], ones_bd, HEAD_SUM_PIECES) * v
    o_ref[...] = ((o + bonus) * g).astype(o_ref.dtype)


def _rwkv(cols, w0, a0, k_k, k_a, r_k, lnx_w, lnx_b, w2p, a2p, g2p, *, batch, seq, n_pairs, rows):
    T = cols.shape[0]
    nt = seq // rows
    lora_blk = (cols.shape[1] - 3 * n_pairs * LANES) // LANES
    row_map = lambda off: (lambda b, p, t: (b * nt + t, off + p))
    vec = pl.BlockSpec((1, LANES), lambda b, p, t: (0, p))
    return pl.pallas_call(
        functools.partial(_rwkv_kernel, rows=rows),
        grid=(batch, n_pairs, nt),
        in_specs=[
            pl.BlockSpec((rows, LANES), row_map(0)),
            pl.BlockSpec((rows, LANES), row_map(n_pairs)),
            pl.BlockSpec((rows, LANES), row_map(2 * n_pairs)),
            pl.BlockSpec((rows, lora_blk * LANES), lambda b, p, t: (b * nt + t, 3 * n_pairs // lora_blk)),
            vec, vec, vec, vec, vec, vec, vec,
            pl.BlockSpec((LANES, LANES), lambda b, p, t: (0, p)),
            pl.BlockSpec((LANES, LANES), lambda b, p, t: (0, p)),
            pl.BlockSpec((2 * LANES, LANES), lambda b, p, t: (0, p)),
        ],
        out_specs=pl.BlockSpec((rows, LANES), row_map(0)),
        out_shape=jax.ShapeDtypeStruct((T, n_pairs * LANES), BF16),
        scratch_shapes=[pltpu.VMEM((LANES, LANES), F32)],
        compiler_params=_cparams("arbitrary", "arbitrary", "arbitrary"),
        name="rwkv7",
    )(cols, cols, cols, cols, w0, a0, k_k, k_a, r_k, lnx_w, lnx_b, w2p, a2p, g2p)


def _out_proj_kernel(x_ref, oa_ref, or_ref, wa_ref, wr_ref, nw_ref, x1_ref, h2_ref):
    x1 = (x_ref[...] + jnp.dot(oa_ref[...], wa_ref[...], preferred_element_type=F32)
          + jnp.dot(or_ref[...], wr_ref[...], preferred_element_type=F32))
    x1_ref[...] = x1
    h2_ref[...] = _rms(x1, nw_ref[...]).astype(BF16)


def _out_proj(x2, oa, orw, wa, wr, nw, *, tm):
    T, D = x2.shape
    W = oa.shape[1]
    return pl.pallas_call(
        _out_proj_kernel,
        grid=(T // tm,),
        in_specs=[
            pl.BlockSpec((tm, D), lambda i: (i, 0)),
            pl.BlockSpec((tm, W), lambda i: (i, 0)),
            pl.BlockSpec((tm, W), lambda i: (i, 0)),
            pl.BlockSpec((W, D), lambda i: (0, 0)),
            pl.BlockSpec((W, D), lambda i: (0, 0)),
            pl.BlockSpec((1, D), lambda i: (0, 0)),
        ],
        out_specs=[pl.BlockSpec((tm, D), lambda i: (i, 0)), pl.BlockSpec((tm, D), lambda i: (i, 0))],
        out_shape=[jax.ShapeDtypeStruct((T, D), F32), jax.ShapeDtypeStruct((T, D), BF16)],
        compiler_params=_cparams("arbitrary"),
        name="out_proj",
    )(x2, oa, orw, wa, wr, nw)


def _topk_rows(curs, k):
    curs = list(curs)
    rows = [[] for _ in curs]
    for _ in range(k):
        ms = [jnp.max(cur, axis=0, keepdims=True) for cur in curs]
        curs = [jnp.where(cur == m, -jnp.inf, cur) for cur, m in zip(curs, ms)]
        for r, m in zip(rows, ms):
            r.append(m)
    return rows


_NLIST = PEER_TOPK + 1
_CANDS = [(a, b) for a in range(_NLIST) for b in range(_NLIST) if (a + 1) * (b + 1) <= _NLIST]
_NCAND = -(-len(_CANDS) // 8) * 8


def _route_kernel(h2_ref, wq_ref, keys_ref, thr_ref, e1_ref, e2_ref, sc_ref, cand_ref, *, n_heads):
    q = jnp.dot(h2_ref[...], wq_ref[...], preferred_element_type=F32).astype(BF16)
    for g in range(2 * n_heads):
        sc_ref[g] = lax.dot_general(keys_ref[g], q[:, g * LANES:(g + 1) * LANES], _NT,
                                    preferred_element_type=F32)
    cand_ref[...] = jnp.full(cand_ref.shape, -jnp.inf, F32)

    def per_head(h, _):
        s1 = sc_ref[2 * h]
        s2 = sc_ref[2 * h + 1]
        v1, v2 = _topk_rows((s1, s2), _NLIST)
        for n, (ia, ib) in enumerate(_CANDS):
            cand_ref[n:n + 1, :] = v1[ia] + v2[ib]
        cand = cand_ref[...]
        top, = _topk_rows((cand,), _NLIST)
        tau = 0.5 * (top[PEER_TOPK - 1] + top[PEER_TOPK])
        inv_z = 1.0 / jnp.sum(jnp.where(cand > tau, jnp.exp(cand - (v1[0] + v2[0])), 0.0), axis=0, keepdims=True)
        e1_ref[h] = jnp.exp(s1 - v1[0])
        e2_ref[h] = jnp.exp(s2 - v2[0]) * inv_z
        thr_ref[h] = jnp.exp((tau - s1) - v2[0]) * inv_z
        return 0

    lax.fori_loop(0, n_heads, per_head, 0)


def _route(h2, wq, keys, *, tt, n_heads):
    T, D = h2.shape
    nk = keys.shape[1]
    big = jax.ShapeDtypeStruct((n_heads, nk, T), F32)
    blk = pl.BlockSpec((n_heads, nk, tt), lambda i: (0, 0, i))
    return pl.pallas_call(
        functools.partial(_route_kernel, n_heads=n_heads),
        grid=(T // tt,),
        in_specs=[
            pl.BlockSpec((tt, D), lambda i: (i, 0)),
            pl.BlockSpec(wq.shape, lambda i: (0, 0)),
            pl.BlockSpec(keys.shape, lambda i: (0, 0, 0)),
        ],
        out_specs=[blk, blk, blk],
        out_shape=[big, big, big],
        scratch_shapes=[pltpu.VMEM((2 * n_heads, nk, tt), F32), pltpu.VMEM((_NCAND, tt), F32)],
        compiler_params=_cparams("arbitrary"),
        name="peer_route",
    )(h2, wq, keys)


def _peer_kernel(h2_ref, down_ref, upt_ref, thr_ref, e1_ref, e2_ref, o_ref, act_ref, *, n_heads, nk):
    @pl.when(pl.program_id(1) == 0)
    def _():
        o_ref[...] = jnp.zeros(o_ref.shape, F32)

    n_i1 = down_ref.shape[0] // nk
    per_chunk = n_i1 // PEER_Z_CHUNKS
    for q in range(PEER_Z_CHUNKS):
        zr = slice(q * per_chunk * nk, (q + 1) * per_chunk * nk)
        z = lax.dot_general(down_ref[zr, :], h2_ref[...], _NT, preferred_element_type=F32)
        for ic in range(per_chunk):
            ii = q * per_chunk + ic
            rs = slice(ii * nk, (ii + 1) * nk)
            gate = None
            for h in range(n_heads):
                e2 = e2_ref[h]
                val = jnp.where(e2 >= thr_ref[h, ii:ii + 1, :], e2, 0.0) * e1_ref[h, ii:ii + 1, :]
                gate = val if gate is None else gate + val
            zz = z[ic * nk:(ic + 1) * nk]
            gelu = zz * (1.0 + lax.erf(zz * (math.sqrt(0.5) / PEER_DOWN_SCALE)))
            act_ref[rs, :] = (gelu * gate).astype(FP8)
    d_rows = upt_ref.shape[0] // PEER_UP_CHUNKS
    e_rows = down_ref.shape[0] // PEER_Z_CHUNKS
    for q in range(PEER_Z_CHUNKS):
        es = slice(q * e_rows, (q + 1) * e_rows)
        for u in range(PEER_UP_CHUNKS):
            us = slice(u * d_rows, (u + 1) * d_rows)
            o_ref[us, :] += jnp.dot(upt_ref[us, es], act_ref[es, :], preferred_element_type=F32)


def _peer(h2, down, upt, thr, e1, e2, *, tt, et, n_heads, nk):
    T, D = h2.shape
    E = down.shape[0]
    i1_blk = et // nk
    row_blk = pl.BlockSpec((n_heads, i1_blk, tt), lambda i, j: (0, j, i))
    once = pl.Buffered(1)
    return pl.pallas_call(
        functools.partial(_peer_kernel, n_heads=n_heads, nk=nk),
        grid=(T // tt, E // et),
        in_specs=[
            pl.BlockSpec((tt, D), lambda i, j: (i, 0), pipeline_mode=once),
            pl.BlockSpec((et, D), lambda i, j: (j, 0)),
            pl.BlockSpec((D, et), lambda i, j: (0, j)),
            row_blk, row_blk,
            pl.BlockSpec((n_heads, nk, tt), lambda i, j: (0, 0, i), pipeline_mode=once),
        ],
        out_specs=pl.BlockSpec((D, tt), lambda i, j: (0, i)),
        out_shape=jax.ShapeDtypeStruct((D, T), F32),
        scratch_shapes=[pltpu.VMEM((et, tt), FP8)],
        compiler_params=_cparams("arbitrary", "arbitrary"),
        name="peer_experts",
    )(h2, down, upt, thr, e1, e2)


def _final_kernel(x1_ref, pt_ref, nw_ref, o_ref):
    o_ref[...] = _rms(x1_ref[...] + pt_ref[...].T * (1.0 / PEER_ACT_SCALE), nw_ref[...])


def _final(x1, peer_t, nw, *, tm):
    T, D = x1.shape
    return pl.pallas_call(
        _final_kernel,
        grid=(T // tm,),
        in_specs=[
            pl.BlockSpec((tm, D), lambda i: (i, 0)),
            pl.BlockSpec((D, tm), lambda i: (0, i)),
            pl.BlockSpec((1, D), lambda i: (0, 0)),
        ],
        out_specs=pl.BlockSpec((tm, D), lambda i: (i, 0)),
        out_shape=jax.ShapeDtypeStruct((T, D), F32),
        compiler_params=_cparams("arbitrary"),
        name="final_norm",
    )(x1, peer_t, nw)


def _tiles(seq, tokens):
    return dict(
        proj_tm=min(1024, seq),
        attn_tq=min(512, seq),
        rwkv_rows=min(512, seq),
        out_tm=min(512, tokens),
        route_tt=min(512, tokens),
        peer_tt=min(512, tokens),
        peer_et=2048,
        final_tm=min(512, tokens),
    )


def _pad_rows(w, rows_before, rows_total):
    return jnp.pad(w, ((rows_before, rows_total - rows_before - w.shape[0]), (0, 0)))


def kernel(x, norm1_w, w_in, tshift_mu, w0, w2, a0, a2, g2, k_k, k_a, r_k, lnx_w, lnx_b, lambda_q1, lambda_k1,
           lambda_q2, lambda_k2, subln_w, w_out, norm2_w, peer_w_query, peer_sub_keys, peer_down, peer_up,
           norm_f_w):
    B, S, D = x.shape
    T = B * S
    depth = norm1_w.shape[0]
    dh = lambda_q1.shape[-1]
    n_rheads, rhead = r_k.shape[1], r_k.shape[2]
    rw = n_rheads * rhead
    n_pairs = rw // LANES
    w_lora, a_lora, g_lora = w2.shape[1], a2.shape[1], g2.shape[1]
    da_cols = w_in.shape[2] - (3 * rw + w_lora + a_lora + g_lora)
    n_aheads = da_cols // (3 * 2 * dh)
    p_heads, nk = peer_sub_keys.shape[1], peer_sub_keys.shape[3]
    assert 2 * dh == LANES and 2 * rhead == LANES and rhead == RWKV_CHUNK and nk == LANES
    assert w_lora + a_lora == LANES and g_lora <= 2 * LANES
    t = _tiles(S, T)

    xt = x.reshape(T, D)
    for l in range(depth):
        row = lambda p: p[l].reshape(1, -1)
        lam_init = 0.8 - 0.6 * math.exp(-0.3 * l)

        w_attn = w_in[l][:, :da_cols].astype(BF16)
        rcols = w_in[l].shape[1] - da_cols
        rpad = 3 * rw + 3 * LANES - rcols
        w_rwkv = jnp.pad(w_in[l][:, da_cols:], ((0, 0), (0, rpad))).astype(BF16)
        mu = jnp.pad(row(tshift_mu), ((0, 0), (0, rpad)))
        w2p = _pad_rows(w2[l], 0, LANES).astype(BF16)
        a2p = _pad_rows(a2[l], w_lora, LANES).astype(BF16)
        g2p = _pad_rows(g2[l], 0, 2 * LANES).astype(BF16)

        qkv = _in_proj(xt, row(norm1_w), w_attn, None, seq=S, tm=t["proj_tm"], tn=da_cols // 3)
        rcol = _in_proj(xt, row(norm1_w), w_rwkv, mu, seq=S, tm=t["proj_tm"], tn=w_rwkv.shape[1] // 3)

        o_attn = _diff_attention(qkv, row(lambda_q1), row(lambda_k1), row(lambda_q2), row(lambda_k2),
                                 row(subln_w), batch=B, seq=S, n_heads=n_aheads, tq=t["attn_tq"],
                                 lam_init=lam_init)
        o_rwkv = _rwkv(rcol, row(w0), row(a0), row(k_k), row(k_a), row(r_k), row(lnx_w), row(lnx_b),
                       w2p, a2p, g2p, batch=B, seq=S, n_pairs=n_pairs, rows=t["rwkv_rows"])

        aw = o_attn.shape[1]
        x1, h2 = _out_proj(xt, o_attn, o_rwkv, w_out[l][:aw].astype(BF16), w_out[l][aw:].astype(BF16),
                           row(norm2_w), tm=t["out_tm"])

        keys = peer_sub_keys[l].reshape(2 * p_heads, nk, -1).astype(BF16)
        thr, e1, e2 = _route(h2, peer_w_query[l].astype(BF16), keys, tt=t["route_tt"], n_heads=p_heads)
        peer_t = _peer(h2.astype(FP8), (peer_down[l] * PEER_DOWN_SCALE).astype(FP8), peer_up[l].T.astype(FP8), thr, e1, e2,
                       tt=t["peer_tt"], et=t["peer_et"], n_heads=p_heads, nk=nk)
        if l + 1 < depth:
            xt = x1 + peer_t.T * (1.0 / PEER_ACT_SCALE)
        else:
            return _final(x1, peer_t, norm_f_w.reshape(1, -1), tm=t["final_tm"]).reshape(B, S, D)
```

```python
import functools
import math

import jax
import jax.numpy as jnp
from jax import lax
from jax.experimental import pallas as pl
from jax.experimental.pallas import tpu as pltpu

F32 = jnp.float32
BF16 = jnp.bfloat16
FP8 = jnp.float8_e4m3fn

LANES = 128
RMS_EPS = 1e-6
LNX_EPS = 64e-5
PEER_TOPK = 16
RWKV_CHUNK = 64
HEAD_SUM_PIECES = 1
PEER_DOWN_SCALE = 32.0
PEER_ACT_SCALE = 2.0 * PEER_DOWN_SCALE
PEER_Z_CHUNKS = 2
PEER_UP_CHUNKS = 4
VMEM_LIMIT = 56 * 1024 * 1024
NEG = -1e30
SOFTMAX_UNDERFLOW = 110.0
FIXED_SHIFT_SPREAD = 40.0

_NT = (((1,), (1,)), ((), ()))


def _cparams(*sem):
    return pltpu.CompilerParams(dimension_semantics=sem, vmem_limit_bytes=VMEM_LIMIT)


def _rms(x, w):
    ms = jnp.mean(x * x, axis=-1, keepdims=True)
    return x * lax.rsqrt(ms + RMS_EPS) * w


def _proj_kernel(x_ref, nw_ref, w_ref, o_ref, h_ref):
    @pl.when(pl.program_id(1) == 0)
    def _():
        h_ref[...] = _rms(x_ref[...], nw_ref[...]).astype(BF16)

    o_ref[...] = jnp.dot(h_ref[...], w_ref[...], preferred_element_type=F32).astype(o_ref.dtype)


def _proj_shift_kernel(x_ref, nw_ref, w_ref, mu_ref, o_ref, h_ref, carry_ref, *, tiles_per_seq):
    i = pl.program_id(0)
    j = pl.program_id(1)

    @pl.when(j == 0)
    def _():
        h_ref[...] = _rms(x_ref[...], nw_ref[...]).astype(BF16)

    @pl.when(i % tiles_per_seq == 0)
    def _():
        carry_ref[j] = jnp.zeros(carry_ref.shape[1:], F32)

    p = jnp.dot(h_ref[...], w_ref[...], preferred_element_type=F32)
    tm = p.shape[0]
    rolled = pltpu.roll(p, 1, 0)
    row = lax.broadcasted_iota(jnp.int32, p.shape, 0)
    prev = jnp.where(row == 0, carry_ref[j][7:8, :], rolled)
    o_ref[...] = (p + (prev - p) * mu_ref[...]).astype(o_ref.dtype)
    carry_ref[j] = p[tm - 8:tm, :]


def _in_proj(x2, nw, w, mu, *, seq, tm, tn):
    T, D = x2.shape
    N = w.shape[1]
    grid = (T // tm, N // tn)
    in_specs = [
        pl.BlockSpec((tm, D), lambda i, j: (i, 0)),
        pl.BlockSpec((1, D), lambda i, j: (0, 0)),
        pl.BlockSpec((D, tn), lambda i, j: (0, j)),
    ]
    args = [x2, nw, w]
    scratch = [pltpu.VMEM((tm, D), BF16)]
    if mu is None:
        body = _proj_kernel
    else:
        body = functools.partial(_proj_shift_kernel, tiles_per_seq=seq // tm)
        in_specs.append(pl.BlockSpec((1, tn), lambda i, j: (0, j)))
        args.append(mu)
        scratch.append(pltpu.VMEM((N // tn, 8, tn), F32))
    return pl.pallas_call(
        body,
        grid=grid,
        in_specs=in_specs,
        out_specs=pl.BlockSpec((tm, tn), lambda i, j: (i, j)),
        out_shape=jax.ShapeDtypeStruct((T, N), BF16),
        scratch_shapes=scratch,
        compiler_params=_cparams("arbitrary", "arbitrary"),
        name="in_proj" if mu is None else "in_proj_shift",
    )(*args)


def _half_sqnorms(x, lane2, dh):
    sq = x * x
    lo = jnp.sum(jnp.where(lane2 < dh, sq, 0.0), axis=-1, keepdims=True)
    hi = jnp.sum(jnp.where(lane2 >= dh, sq, 0.0), axis=-1, keepdims=True)
    per_row = jnp.maximum(lo, hi)
    return per_row, jnp.max(per_row, axis=0, keepdims=True)


def _attn_kernel(q_ref, k_ref, v_ref, lq1_ref, lk1_ref, lq2_ref, lk2_ref, sw_ref, o_ref, kx1_ref, kx2_ref, knorm_ref,
                 *, tq, seq, n_heads, lam_init):
    h = pl.program_id(1)
    qi = pl.program_id(2)
    tk = tq
    dh = LANES // 2
    lane2 = lax.broadcasted_iota(jnp.int32, (tq, LANES), 1)

    @pl.when(qi == 0)
    def _():
        def build(c, knorm):
            rows = pl.ds(pl.multiple_of(c * tk, tk), tk)
            kb = k_ref[rows, :].astype(F32)
            pos = c * tk + lax.broadcasted_iota(jnp.int32, (tk, LANES), 0)
            hi = (pos // LANES).astype(F32)
            lo = (pos % LANES).astype(F32)
            kx1_ref[rows, :] = jnp.where(lane2 < dh, kb, jnp.where(lane2 == dh, hi, jnp.where(lane2 == dh + 1, lo, 0.0))
                                         ).astype(BF16)
            kx2_ref[rows, :] = jnp.where(lane2 >= dh, kb, jnp.where(lane2 == 0, hi, jnp.where(lane2 == 1, lo, 0.0))
                                         ).astype(BF16)
            return jnp.maximum(knorm, _half_sqnorms(kb, lane2, dh)[1])

        knorm_ref[...] = lax.fori_loop(0, seq // tk, build, jnp.zeros((1, 1), F32))

    q = q_ref[...].astype(F32) * dh ** -0.5
    hv = jnp.full((tq, LANES), h + 1, jnp.int32).astype(F32)
    slope = jnp.exp2(hv * (-8.0 / n_heads))
    q1 = jnp.where(lane2 < dh, q, jnp.where(lane2 == dh, slope * LANES, jnp.where(lane2 == dh + 1, slope, 0.0))
                   ).astype(BF16)
    q2 = jnp.where(lane2 >= dh, q, jnp.where(lane2 == 0, slope * LANES, jnp.where(lane2 == 1, slope, 0.0))
                   ).astype(BF16)

    col = lax.broadcasted_iota(jnp.int32, (1, tk), 1)
    row = lax.broadcasted_iota(jnp.int32, (tq, 1), 0)

    def step(kc, carry, masked):
        rows = pl.ds(pl.multiple_of(kc * tk, tk), tk)
        vb = v_ref[rows, :]
        maps = (0, 1)
        ss = [lax.dot_general(qc, kx_ref[rows, :], _NT, preferred_element_type=F32)
              for qc, kx_ref in ((q1, kx1_ref), (q2, kx2_ref))]
        if masked:
            ss = [jnp.where(col <= row, s, NEG) for s in ss]
        ms, ls, accs = carry[0::3], carry[1::3], carry[2::3]
        mns = [jnp.maximum(ms[c], jnp.max(ss[c], axis=-1, keepdims=True)) for c in maps]
        ps = [jnp.exp(ss[c] - mns[c]) for c in maps]
        als = [jnp.exp(ms[c] - mns[c]) for c in maps]
        ls = [als[c] * ls[c] + jnp.sum(ps[c], axis=-1, keepdims=True) for c in maps]
        pvs = [jnp.dot(ps[c].astype(BF16), vb, preferred_element_type=F32) for c in maps]
        accs = [als[c] * accs[c] + pvs[c] for c in maps]
        return (mns[0], ls[0], accs[0], mns[1], ls[1], accs[1])

    row_sq, tile_sq = _half_sqnorms(q, lane2, dh)
    bound = jnp.sqrt(tile_sq * knorm_ref[...])
    reach = (SOFTMAX_UNDERFLOW + 2.0 * bound) / slope[:1, :1]
    q0 = jnp.full((1, 1), qi * tq, jnp.int32).astype(F32)
    first = jnp.max(jnp.clip(jnp.floor((q0 + 1.0 - reach) / tk), 0.0, q0 / tq).astype(jnp.int32))

    def online_softmax():
        init = (jnp.full((tq, 1), NEG, F32), jnp.zeros((tq, 1), F32), jnp.zeros((tq, LANES), F32)) * 2
        carry = lax.fori_loop(first, qi, lambda kc, c: step(kc, c, False), init)
        _, l1, a1, _, l2, a2 = step(qi, carry, True)
        return l1, a1, l2, a2

    slope_col = jnp.exp2(jnp.full((tq, 1), h + 1, jnp.int32).astype(F32) * (-8.0 / n_heads))
    pos_col = (qi * tq + row).astype(F32)
    shift = jnp.sqrt(row_sq * knorm_ref[...]) + slope_col * pos_col

    def fixed_step(kc, carry, masked):
        rows = pl.ds(pl.multiple_of(kc * tk, tk), tk)
        vb = v_ref[rows, :]
        maps = (0, 1)
        ps = [jnp.exp(lax.dot_general(qc, kx_ref[rows, :], _NT, preferred_element_type=F32) - shift)
              for qc, kx_ref in ((q1, kx1_ref), (q2, kx2_ref))]
        if masked:
            ps = [jnp.where(col <= row, p, 0.0) for p in ps]
        ls = [carry[2 * c] + jnp.sum(ps[c], axis=-1, keepdims=True) for c in maps]
        accs = [carry[2 * c + 1] + jnp.dot(ps[c].astype(BF16), vb, preferred_element_type=F32) for c in maps]
        return (ls[0], accs[0], ls[1], accs[1])

    def fixed_shift_softmax():
        init = (jnp.zeros((tq, 1), F32), jnp.zeros((tq, LANES), F32)) * 2
        carry = lax.fori_loop(first, qi, lambda kc, c: fixed_step(kc, c, False), init)
        return fixed_step(qi, carry, True)

    l1, a1, l2, a2 = lax.cond(jnp.max(bound) * 2.0 <= FIXED_SHIFT_SPREAD, fixed_shift_softmax, online_softmax)

    lam = (jnp.exp(jnp.sum(lq1_ref[...] * lk1_ref[...], axis=-1, keepdims=True))
           - jnp.exp(jnp.sum(lq2_ref[...] * lk2_ref[...], axis=-1, keepdims=True)) + lam_init)
    o = a1 / l1 - lam * (a2 / l2)
    o_ref[...] = (_rms(o, sw_ref[...]) * (1.0 - lam_init)).astype(o_ref.dtype)


def _diff_attention(qkv, lq1, lk1, lq2, lk2, subln_w, *, batch, seq, n_heads, tq, lam_init):
    T = qkv.shape[0]
    nq = seq // tq
    assert 8 % n_heads == 0 and seq <= 256 * LANES
    vec = lambda n: pl.BlockSpec((1, n), lambda b, h, i: (0, 0))
    return pl.pallas_call(
        functools.partial(_attn_kernel, tq=tq, seq=seq, n_heads=n_heads, lam_init=lam_init),
        grid=(batch, n_heads, nq),
        in_specs=[
            pl.BlockSpec((tq, LANES), lambda b, h, i: (b * nq + i, h)),
            pl.BlockSpec((seq, LANES), lambda b, h, i: (b, n_heads + h)),
            pl.BlockSpec((seq, LANES), lambda b, h, i: (b, 2 * n_heads + h)),
            vec(lq1.shape[1]), vec(lq1.shape[1]), vec(lq1.shape[1]), vec(lq1.shape[1]),
            vec(LANES),
        ],
        out_specs=pl.BlockSpec((tq, LANES), lambda b, h, i: (b * nq + i, h)),
        out_shape=jax.ShapeDtypeStruct((T, n_heads * LANES), BF16),
        scratch_shapes=[pltpu.VMEM((seq, LANES), BF16), pltpu.VMEM((seq, LANES), BF16), pltpu.VMEM((1, 1), F32)],
        compiler_params=_cparams("arbitrary", "arbitrary", "arbitrary"),
        name="diff_attn",
    )(qkv, qkv, qkv, lq1, lk1, lq2, lk2, subln_w)


def _split_bf16(x, n):
    parts = []
    for _ in range(n):
        p = x.astype(BF16)
        parts.append(p)
        x = x - p.astype(F32)
    return parts


def _dot(a, b):
    return jnp.dot(a.astype(BF16), b.astype(BF16), preferred_element_type=F32)


def _dot_exact_rhs(x, m_bf16, n):
    out = None
    for p in _split_bf16(x, n):
        t = jnp.dot(p, m_bf16, preferred_element_type=F32)
        out = t if out is None else out + t
    return out


def _chunk_cumsum(x, chunk):
    pos = lax.broadcasted_iota(jnp.int32, x.shape, 0) % chunk
    step = 1
    while step < chunk:
        x = x + jnp.where(pos >= step, pltpu.roll(x, step, 0), 0.0)
        step *= 2
    return x


def _rwkv_kernel(r_ref, k_ref, v_ref, lo_ref, w0_ref, a0_ref, kk_ref, ka_ref, rk_ref, lw_ref, lb_ref,
                 w2_ref, a2_ref, g2_ref, o_ref, st_ref, *, rows):
    L = RWKV_CHUNK
    N = LANES // 2
    P = LANES

    @pl.when(pl.program_id(2) == 0)
    def _():
        st_ref[...] = jnp.zeros(st_ref.shape, F32)

    r = r_ref[...].astype(F32)
    k = k_ref[...].astype(F32)
    v = v_ref[...].astype(F32)
    lo = lo_ref[...]
    lo_wa = lo[:, :P]
    wl = jnp.dot(jnp.tanh(lo_wa.astype(F32)).astype(BF16), w2_ref[...], preferred_element_type=F32)
    al = jnp.dot(lo_wa, a2_ref[...], preferred_element_type=F32)
    g = jnp.dot(jax.nn.sigmoid(lo[:, P:].astype(F32)).astype(BF16), g2_ref[...], preferred_element_type=F32)

    wx = -(w0_ref[...] + wl)
    softplus = jnp.maximum(wx, 0.0) + jnp.log1p(jnp.exp(-jnp.abs(wx)))
    logdecay = -jnp.exp(-softplus - 0.5)
    a = jax.nn.sigmoid(a0_ref[...] + al)

    ri = lax.broadcasted_iota(jnp.int32, (P, P), 0)
    ci = lax.broadcasted_iota(jnp.int32, (P, P), 1)
    ones_bd = ((ri // N) == (ci // N)).astype(BF16)
    eye = (ri == ci).astype(F32)
    strict = ((ri % L) > (ci % L)).astype(F32)
    lower = ((ri % L) >= (ci % L)).astype(F32)
    head0 = lax.broadcasted_iota(jnp.int32, (1, P), 1) < N

    kkr = k * kk_ref[...]
    norm = jnp.sqrt(_dot_exact_rhs(kkr * kkr, ones_bd, HEAD_SUM_PIECES))
    kk = kkr / jnp.maximum(norm, 1e-12)
    kf = k * (1.0 + (a - 1.0) * ka_ref[...])
    avec = -kk
    bvec = kk * a

    def bd(x):
        x = x.astype(BF16)
        z = jnp.zeros_like(x)
        return jnp.concatenate([jnp.where(head0, x, z), jnp.where(head0, z, x)], axis=0)

    nc = rows // L
    chunks = range(nc)
    sls = [slice(c * L, (c + 1) * L) for c in chunks]
    cum_all = _chunk_cumsum(logdecay, L)
    cum = [cum_all[s] for s in sls]
    tot = [cm[L - 1:L, :] for cm in cum]
    at = [bd(avec[s] * jnp.exp(cm - logdecay[s])) for s, cm in zip(sls, cum)]
    rt = [bd(r[s] * jnp.exp(cm)) for s, cm in zip(sls, cum)]
    inv = [jnp.exp(-cm) for cm in cum]
    bt = [bd(bvec[s] * iv) for s, iv in zip(sls, inv)]
    kt = [bd(kf[s] * iv) for s, iv in zip(sls, inv)]
    rest = [jnp.exp(t - cm) for t, cm in zip(tot, cum)]
    blkl_t = [jnp.concatenate([bd(bvec[s] * rs), bd(kf[s] * rs)], axis=1).astype(F32).T.astype(BF16)
              for s, rs in zip(sls, rest)]
    vd = [bd(v[s]) for s in sls]

    amat = [lax.dot_general(jnp.concatenate([at[c], rt[c]], axis=0), jnp.concatenate([bt[c], kt[c]], axis=0), _NT,
                            preferred_element_type=F32) for c in chunks]
    a_ab = [m[:P, :P] * strict for m in amat]
    a_ak = [(m[:P, P:] * strict).astype(BF16) for m in amat]
    a_rb = [(m[P:, :P] * lower).astype(BF16) for m in amat]
    a_rk = [(m[P:, P:] * lower).astype(BF16) for m in amat]

    tinv = [eye + n for n in a_ab]
    pw = [_dot(n, n) for n in a_ab]
    for _ in range(int(math.log2(L)) - 2):
        both = [_dot(pw[c], jnp.concatenate([pw[c], tinv[c]], axis=1)) for c in chunks]
        pw = [b[:, :P] for b in both]
        tinv = [tinv[c] + both[c][:, P:] for c in chunks]
    tinv = [tinv[c] + _dot(pw[c], tinv[c]) for c in chunks]

    avd = [_dot(jnp.concatenate([a_ak[c], a_rk[c], blkl_t[c][P:]], axis=0), vd[c]) for c in chunks]
    hat = [_dot(tinv[c], jnp.concatenate([at[c], avd[c][:P].astype(BF16)], axis=1)) for c in chunks]
    mix = [_dot(jnp.concatenate([a_rb[c], blkl_t[c][:P]], axis=0), hat[c]) for c in chunks]
    rhat = [rt[c].astype(F32) + mix[c][:P, :P] for c in chunks]
    ohat = [mix[c][:P, P:] + avd[c][P:2 * P] for c in chunks]
    mmat = [eye * jnp.exp(tot[c]) + mix[c][P:, :P] for c in chunks]
    cmat = [mix[c][P:, P:] + avd[c][2 * P:] for c in chunks]
    rhat = [x.astype(BF16) for x in rhat]
    mmat = [x.astype(BF16) for x in mmat]

    states = []
    st = st_ref[...]
    for c in chunks:
        states.append(st.astype(BF16))
        st = _dot(mmat[c], states[c]) + cmat[c]
    st_ref[...] = st
    outs = []
    for c in chunks:
        od = _dot(rhat[c], states[c]) + ohat[c]
        outs.append(od[:L] + od[L:])
    wkv = jnp.concatenate(outs, axis=0)

    inv_n = 1.0 / N
    mu = _dot_exact_rhs(wkv, ones_bd, HEAD_SUM_PIECES) * inv_n
    xc = wkv - mu
    var = _dot_exact_rhs(xc * xc, ones_bd, HEAD_SUM_PIECES) * inv_n
    o = xc * lax.rsqrt(var + LNX_EPS) * lw_ref[...] + lb_ref[...]
    bonus = _dot_exact_rhs(r * kf * rk_ref[...], ones_bd, HEAD_SUM_PIECES) * v
    o_ref[...] = ((o + bonus) * g).astype(o_ref.dtype)


def _rwkv(cols, w0, a0, k_k, k_a, r_k, lnx_w, lnx_b, w2p, a2p, g2p, *, batch, seq, n_pairs, rows, col0=0):
    T = cols.shape[0]
    nt = seq // rows
    lora_at = col0 + 3 * n_pairs
    lora_blk = cols.shape[1] // LANES - lora_at
    row_map = lambda off: (lambda b, p, t: (b * nt + t, col0 + off + p))
    vec = pl.BlockSpec((1, LANES), lambda b, p, t: (0, p))
    return pl.pallas_call(
        functools.partial(_rwkv_kernel, rows=rows),
        grid=(batch, n_pairs, nt),
        in_specs=[
            pl.BlockSpec((rows, LANES), row_map(0)),
            pl.BlockSpec((rows, LANES), row_map(n_pairs)),
            pl.BlockSpec((rows, LANES), row_map(2 * n_pairs)),
            pl.BlockSpec((rows, lora_blk * LANES), lambda b, p, t: (b * nt + t, lora_at // lora_blk)),
            vec, vec, vec, vec, vec, vec, vec,
            pl.BlockSpec((LANES, LANES), lambda b, p, t: (0, p)),
            pl.BlockSpec((LANES, LANES), lambda b, p, t: (0, p)),
            pl.BlockSpec((2 * LANES, LANES), lambda b, p, t: (0, p)),
        ],
        out_specs=pl.BlockSpec((rows, LANES), lambda b, p, t: (b * nt + t, p)),
        out_shape=jax.ShapeDtypeStruct((T, n_pairs * LANES), BF16),
        scratch_shapes=[pltpu.VMEM((LANES, LANES), F32)],
        compiler_params=_cparams("arbitrary", "arbitrary", "arbitrary"),
        name="rwkv7",
    )(cols, cols, cols, cols, w0, a0, k_k, k_a, r_k, lnx_w, lnx_b, w2p, a2p, g2p)


def _out_proj_kernel(x_ref, oa_ref, or_ref, wa_ref, wr_ref, nw_ref, x1_ref, h2_ref):
    x1 = (x_ref[...] + jnp.dot(oa_ref[...], wa_ref[...], preferred_element_type=F32)
          + jnp.dot(or_ref[...], wr_ref[...], preferred_element_type=F32))
    x1_ref[...] = x1
    h2_ref[...] = _rms(x1, nw_ref[...]).astype(BF16)


def _out_proj(x2, oa, orw, wa, wr, nw, *, tm):
    T, D = x2.shape
    W = oa.shape[1]
    return pl.pallas_call(
        _out_proj_kernel,
        grid=(T // tm,),
        in_specs=[
            pl.BlockSpec((tm, D), lambda i: (i, 0)),
            pl.BlockSpec((tm, W), lambda i: (i, 0)),
            pl.BlockSpec((tm, W), lambda i: (i, 0)),
            pl.BlockSpec((W, D), lambda i: (0, 0)),
            pl.BlockSpec((W, D), lambda i: (0, 0)),
            pl.BlockSpec((1, D), lambda i: (0, 0)),
        ],
        out_specs=[pl.BlockSpec((tm, D), lambda i: (i, 0)), pl.BlockSpec((tm, D), lambda i: (i, 0))],
        out_shape=[jax.ShapeDtypeStruct((T, D), F32), jax.ShapeDtypeStruct((T, D), BF16)],
        compiler_params=_cparams("arbitrary"),
        name="out_proj",
    )(x2, oa, orw, wa, wr, nw)


def _topk_rows(curs, k):
    curs = list(curs)
    rows = [[] for _ in curs]
    for _ in range(k):
        ms = [jnp.max(cur, axis=0, keepdims=True) for cur in curs]
        curs = [jnp.where(cur == m, -jnp.inf, cur) for cur, m in zip(curs, ms)]
        for r, m in zip(rows, ms):
            r.append(m)
    return rows


_NLIST = PEER_TOPK + 1
_CANDS = [(a, b) for a in range(_NLIST) for b in range(_NLIST) if (a + 1) * (b + 1) <= _NLIST]
_NCAND = -(-len(_CANDS) // 8) * 8


def _route_kernel(h2_ref, wq_ref, keys_ref, thr_ref, e1_ref, e2_ref, sc_ref, cand_ref, *, n_heads):
    q = jnp.dot(h2_ref[...], wq_ref[...], preferred_element_type=F32).astype(BF16)
    for g in range(2 * n_heads):
        sc_ref[g] = lax.dot_general(keys_ref[g], q[:, g * LANES:(g + 1) * LANES], _NT,
                                    preferred_element_type=F32)
    cand_ref[...] = jnp.full(cand_ref.shape, -jnp.inf, F32)

    def per_head(h, _):
        s1 = sc_ref[2 * h]
        s2 = sc_ref[2 * h + 1]
        v1, v2 = _topk_rows((s1, s2), _NLIST)
        for n, (ia, ib) in enumerate(_CANDS):
            cand_ref[n:n + 1, :] = v1[ia] + v2[ib]
        cand = cand_ref[...]
        top, = _topk_rows((cand,), _NLIST)
        tau = 0.5 * (top[PEER_TOPK - 1] + top[PEER_TOPK])
        inv_z = 1.0 / jnp.sum(jnp.where(cand > tau, jnp.exp(cand - (v1[0] + v2[0])), 0.0), axis=0, keepdims=True)
        e1_ref[h] = jnp.exp(s1 - v1[0])
        e2_ref[h] = jnp.exp(s2 - v2[0]) * inv_z
        thr_ref[h] = jnp.exp((tau - s1) - v2[0]) * inv_z
        return 0

    lax.fori_loop(0, n_heads, per_head, 0)


def _route(h2, wq, keys, *, tt, n_heads):
    T, D = h2.shape
    nk = keys.shape[1]
    big = jax.ShapeDtypeStruct((n_heads, nk, T), F32)
    blk = pl.BlockSpec((n_heads, nk, tt), lambda i: (0, 0, i))
    return pl.pallas_call(
        functools.partial(_route_kernel, n_heads=n_heads),
        grid=(T // tt,),
        in_specs=[
            pl.BlockSpec((tt, D), lambda i: (i, 0)),
            pl.BlockSpec(wq.shape, lambda i: (0, 0)),
            pl.BlockSpec(keys.shape, lambda i: (0, 0, 0)),
        ],
        out_specs=[blk, blk, blk],
        out_shape=[big, big, big],
        scratch_shapes=[pltpu.VMEM((2 * n_heads, nk, tt), F32), pltpu.VMEM((_NCAND, tt), F32)],
        compiler_params=_cparams("arbitrary"),
        name="peer_route",
    )(h2, wq, keys)


def _peer_kernel(h2_ref, down_ref, upt_ref, thr_ref, e1_ref, e2_ref, o_ref, act_ref, *, n_heads, nk):
    @pl.when(pl.program_id(1) == 0)
    def _():
        o_ref[...] = jnp.zeros(o_ref.shape, F32)

    n_i1 = down_ref.shape[0] // nk
    per_chunk = n_i1 // PEER_Z_CHUNKS
    for q in range(PEER_Z_CHUNKS):
        zr = slice(q * per_chunk * nk, (q + 1) * per_chunk * nk)
        z = lax.dot_general(down_ref[zr, :], h2_ref[...], _NT, preferred_element_type=F32)
        for ic in range(per_chunk):
            ii = q * per_chunk + ic
            rs = slice(ii * nk, (ii + 1) * nk)
            gate = None
            for h in range(n_heads):
                e2 = e2_ref[h]
                val = jnp.where(e2 >= thr_ref[h, ii:ii + 1, :], e2, 0.0) * e1_ref[h, ii:ii + 1, :]
                gate = val if gate is None else gate + val
            zz = z[ic * nk:(ic + 1) * nk]
            gelu = zz * (1.0 + lax.erf(zz * (math.sqrt(0.5) / PEER_DOWN_SCALE)))
            act_ref[rs, :] = (gelu * gate).astype(FP8)
    d_rows = upt_ref.shape[0] // PEER_UP_CHUNKS
    e_rows = down_ref.shape[0] // PEER_Z_CHUNKS
    for q in range(PEER_Z_CHUNKS):
        es = slice(q * e_rows, (q + 1) * e_rows)
        for u in range(PEER_UP_CHUNKS):
            us = slice(u * d_rows, (u + 1) * d_rows)
            o_ref[us, :] += jnp.dot(upt_ref[us, es], act_ref[es, :], preferred_element_type=F32)


def _peer(h2, down, upt, thr, e1, e2, *, tt, et, n_heads, nk):
    T, D = h2.shape
    E = down.shape[0]
    i1_blk = et // nk
    row_blk = pl.BlockSpec((n_heads, i1_blk, tt), lambda i, j: (0, j, i))
    once = pl.Buffered(1)
    return pl.pallas_call(
        functools.partial(_peer_kernel, n_heads=n_heads, nk=nk),
        grid=(T // tt, E // et),
        in_specs=[
            pl.BlockSpec((tt, D), lambda i, j: (i, 0), pipeline_mode=once),
            pl.BlockSpec((et, D), lambda i, j: (j, 0)),
            pl.BlockSpec((D, et), lambda i, j: (0, j)),
            row_blk, row_blk,
            pl.BlockSpec((n_heads, nk, tt), lambda i, j: (0, 0, i), pipeline_mode=once),
        ],
        out_specs=pl.BlockSpec((D, tt), lambda i, j: (0, i)),
        out_shape=jax.ShapeDtypeStruct((D, T), F32),
        scratch_shapes=[pltpu.VMEM((et, tt), FP8)],
        compiler_params=_cparams("arbitrary", "arbitrary"),
        name="peer_experts",
    )(h2, down, upt, thr, e1, e2)


def _final_kernel(x1_ref, pt_ref, nw_ref, o_ref):
    o_ref[...] = _rms(x1_ref[...] + pt_ref[...].T * (1.0 / PEER_ACT_SCALE), nw_ref[...])


def _final(x1, peer_t, nw, *, tm):
    T, D = x1.shape
    return pl.pallas_call(
        _final_kernel,
        grid=(T // tm,),
        in_specs=[
            pl.BlockSpec((tm, D), lambda i: (i, 0)),
            pl.BlockSpec((D, tm), lambda i: (0, i)),
            pl.BlockSpec((1, D), lambda i: (0, 0)),
        ],
        out_specs=pl.BlockSpec((tm, D), lambda i: (i, 0)),
        out_shape=jax.ShapeDtypeStruct((T, D), F32),
        compiler_params=_cparams("arbitrary"),
        name="final_norm",
    )(x1, peer_t, nw)


def _tiles(seq, tokens):
    return dict(
        proj_tm=min(1024, seq),
        attn_tq=min(512, seq),
        rwkv_rows=min(512, seq),
        out_tm=min(512, tokens),
        route_tt=min(512, tokens),
        peer_tt=min(512, tokens),
        peer_et=2048,
        final_tm=min(512, tokens),
    )


def _pad_rows(w, rows_before, rows_total):
    return jnp.pad(w, ((rows_before, rows_total - rows_before - w.shape[0]), (0, 0)))


def kernel(x, norm1_w, w_in, tshift_mu, w0, w2, a0, a2, g2, k_k, k_a, r_k, lnx_w, lnx_b, lambda_q1, lambda_k1,
           lambda_q2, lambda_k2, subln_w, w_out, norm2_w, peer_w_query, peer_sub_keys, peer_down, peer_up,
           norm_f_w):
    B, S, D = x.shape
    T = B * S
    depth = norm1_w.shape[0]
    dh = lambda_q1.shape[-1]
    n_rheads, rhead = r_k.shape[1], r_k.shape[2]
    rw = n_rheads * rhead
    n_pairs = rw // LANES
    w_lora, a_lora, g_lora = w2.shape[1], a2.shape[1], g2.shape[1]
    da_cols = w_in.shape[2] - (3 * rw + w_lora + a_lora + g_lora)
    n_aheads = da_cols // (3 * 2 * dh)
    p_heads, nk = peer_sub_keys.shape[1], peer_sub_keys.shape[3]
    assert 2 * dh == LANES and 2 * rhead == LANES and rhead == RWKV_CHUNK and nk == LANES
    assert w_lora + a_lora == LANES and g_lora <= 2 * LANES
    t = _tiles(S, T)

    xt = x.reshape(T, D)
    for l in range(depth):
        row = lambda p: p[l].reshape(1, -1)
        lam_init = 0.8 - 0.6 * math.exp(-0.3 * l)

        rcols = w_in[l].shape[1] - da_cols
        rpad = 3 * rw + 3 * LANES - rcols
        assert da_cols % LANES == 0 and (da_cols + 3 * rw) % (3 * LANES) == 0
        w_all = jnp.pad(w_in[l], ((0, 0), (0, rpad))).astype(BF16)
        mu = jnp.pad(row(tshift_mu), ((0, 0), (da_cols, rpad)))
        w2p = _pad_rows(w2[l], 0, LANES).astype(BF16)
        a2p = _pad_rows(a2[l], w_lora, LANES).astype(BF16)
        g2p = _pad_rows(g2[l], 0, 2 * LANES).astype(BF16)

        proj = _in_proj(xt, row(norm1_w), w_all, mu, seq=S, tm=t["proj_tm"], tn=w_all.shape[1] // 3)

        o_attn = _diff_attention(proj, row(lambda_q1), row(lambda_k1), row(lambda_q2), row(lambda_k2),
                                 row(subln_w), batch=B, seq=S, n_heads=n_aheads, tq=t["attn_tq"],
                                 lam_init=lam_init)
        o_rwkv = _rwkv(proj, row(w0), row(a0), row(k_k), row(k_a), row(r_k), row(lnx_w), row(lnx_b),
                       w2p, a2p, g2p, batch=B, seq=S, n_pairs=n_pairs, rows=t["rwkv_rows"], col0=da_cols // LANES)

        aw = o_attn.shape[1]
        x1, h2 = _out_proj(xt, o_attn, o_rwkv, w_out[l][:aw].astype(BF16), w_out[l][aw:].astype(BF16),
                           row(norm2_w), tm=t["out_tm"])

        keys = peer_sub_keys[l].reshape(2 * p_heads, nk, -1).astype(BF16)
        thr, e1, e2 = _route(h2, peer_w_query[l].astype(BF16), keys, tt=t["route_tt"], n_heads=p_heads)
        peer_t = _peer(h2.astype(FP8), (peer_down[l] * PEER_DOWN_SCALE).astype(FP8), peer_up[l].T.astype(FP8), thr, e1, e2,
                       tt=t["peer_tt"], et=t["peer_et"], n_heads=p_heads, nk=nk)
        if l + 1 < depth:
            xt = x1 + peer_t.T * (1.0 / PEER_ACT_SCALE)
        else:
            return _final(x1, peer_t, norm_f_w.reshape(1, -1), tm=t["final_tm"]).reshape(B, S, D)
```
